```python
import functools
import jax, jax.numpy as jnp
from jax import lax
import numpy as np

D_MODEL = 1024
BATCH = 1
SEQ = 16384
DEPTH = 1

NSA_HEADS = 8
NSA_KV_GROUPS = 2
NSA_HEAD_DIM = 64
NSA_WIDTH = NSA_HEADS * NSA_HEAD_DIM
NSA_KV_WIDTH = NSA_KV_GROUPS * NSA_HEAD_DIM
CMP_LEN = 32
CMP_STRIDE = 16
CMP_HIDDEN = 256
SEL_BLOCK = 64
SEL_TOPN = 16
WINDOW = 512
Q_BLOCK = 128
ROPE_THETA = 500000.0
ROPE_DIM = NSA_HEAD_DIM // 4

RET_HEADS = 4
RET_QK_DIM = 128
RET_V_DIM = 256
RET_QK_WIDTH = RET_HEADS * RET_QK_DIM
RET_V_WIDTH = RET_HEADS * RET_V_DIM
RET_CHUNK = 128
RET_ROPE_BASE = 10000.0

RMS_EPS = 1e-6
GN_EPS = 1e-6

PROJ_SIZES = (
    NSA_WIDTH,
    NSA_KV_WIDTH,
    NSA_KV_WIDTH,
    NSA_KV_WIDTH,
    NSA_KV_WIDTH,
    NSA_KV_WIDTH,
    NSA_KV_WIDTH,
    NSA_HEADS * 3,
    NSA_WIDTH,
    RET_QK_WIDTH,
    RET_QK_WIDTH,
    RET_V_WIDTH,
    RET_V_WIDTH,
    D_MODEL,
    D_MODEL,
)
PROJ_TOTAL = sum(PROJ_SIZES)

kernel_name = "hybrid_nsa_retention_gated_block"


def rms_norm(x, g):
    x32 = x.astype(jnp.float32)
    y = x32 * lax.rsqrt(jnp.mean(x32 * x32, axis=-1, keepdims=True) + RMS_EPS)
    return (y * g.astype(jnp.float32)).astype(x.dtype)


def split_points():
    pts, acc = [], 0
    for s in PROJ_SIZES[:-1]:
        acc += s
        pts.append(acc)
    return pts


def apply_rotary(x, ang):
    half = ang.shape[-1]
    r = 2 * half
    cos = jnp.cos(ang)[:, None, :].astype(x.dtype)
    sin = jnp.sin(ang)[:, None, :].astype(x.dtype)
    x1 = x[..., :half]
    x2 = x[..., half:r]
    return jnp.concatenate([x1 * cos - x2 * sin, x2 * cos + x1 * sin, x[..., r:]], axis=-1)


def masked_softmax(s, mask):
    s = jnp.where(mask, s.astype(jnp.float32), -jnp.inf)
    m = jnp.max(s, axis=-1, keepdims=True)
    m = jnp.where(jnp.isfinite(m), m, 0.0)
    p = jnp.where(mask, jnp.exp(s - m), 0.0)
    return p / jnp.maximum(jnp.sum(p, axis=-1, keepdims=True), jnp.finfo(jnp.float32).tiny)


def compress_blocks(kv, pe, w1, w2):
    S, G, dh = kv.shape
    n_cmp = (S - CMP_LEN) // CMP_STRIDE + 1
    tok_idx = jnp.arange(n_cmp)[:, None] * CMP_STRIDE + jnp.arange(CMP_LEN)[None, :]
    blocks = kv[tok_idx] + pe[None, :, None, :]
    flat = blocks.transpose(0, 2, 1, 3).reshape(n_cmp, G, CMP_LEN * dh)
    return jax.nn.gelu(flat @ w1) @ w2


def nsa_sequence(q, k_cmp_raw, v_cmp_raw, k_sel, v_sel, k_win, v_win, gate,
                 cmp_pe_k, cmp_w1_k, cmp_w2_k, cmp_pe_v, cmp_w1_v, cmp_w2_v):
    S = q.shape[0]
    G, H, dh = NSA_KV_GROUPS, NSA_HEADS, NSA_HEAD_DIM
    hpg = H // G
    scale = dh ** -0.5

    kc = compress_blocks(k_cmp_raw, cmp_pe_k, cmp_w1_k, cmp_w2_k)
    vc = compress_blocks(v_cmp_raw, cmp_pe_v, cmp_w1_v, cmp_w2_v)
    n_cmp = kc.shape[0]
    cmp_start = jnp.arange(n_cmp) * CMP_STRIDE
    cmp_end = cmp_start + CMP_LEN - 1

    n_sel = S // SEL_BLOCK
    n_top = min(SEL_TOPN, n_sel)
    sel_start = jnp.arange(n_sel) * SEL_BLOCK
    overlap = ((cmp_start[:, None] < sel_start[None, :] + SEL_BLOCK)
               & (cmp_start[:, None] + CMP_LEN > sel_start[None, :])).astype(jnp.float32)
    kb = k_sel.reshape(n_sel, SEL_BLOCK, G, dh).transpose(2, 0, 1, 3)
    vb = v_sel.reshape(n_sel, SEL_BLOCK, G, dh).transpose(2, 0, 1, 3)

    k_pad = jnp.pad(k_win, ((WINDOW, 0), (0, 0), (0, 0)))
    v_pad = jnp.pad(v_win, ((WINDOW, 0), (0, 0), (0, 0)))

    n_qb = S // Q_BLOCK
    qb = q.reshape(n_qb, Q_BLOCK, G, hpg, dh)
    gb = gate.reshape(n_qb, Q_BLOCK, G, hpg, 3)
    blk_ids = jnp.arange(n_sel)
    g_idx = jnp.arange(G)[None, :, None]

    def block(args):
        qi, gi, b = args
        t = b * Q_BLOCK + jnp.arange(Q_BLOCK)
        s = jnp.einsum('qghd,cgd->qghc', qi, kc) * scale
        mask_c = (cmp_end[None, :] <= t[:, None])[:, None, None, :]
        p_c = masked_softmax(s, mask_c)
        o_c = jnp.einsum('qghc,cgd->qghd', p_c.astype(vc.dtype), vc)
        imp = jnp.einsum('qgc,cn->qgn', jnp.sum(p_c, axis=2), overlap)
        cur = t // SEL_BLOCK
        valid = blk_ids[None, :] <= cur[:, None]
        forced = ((blk_ids[None, :] == 0) | (blk_ids[None, :] == cur[:, None])
                  | (blk_ids[None, :] == cur[:, None] - 1))
        score = jnp.where(forced[:, None, :], jnp.inf,
                          jnp.where(valid[:, None, :], imp, -jnp.inf))
        _, idx = lax.top_k(score, n_top)
        sel_valid = jnp.take_along_axis(
            jnp.broadcast_to(valid[:, None, :], score.shape), idx, axis=-1)
        ks = kb[g_idx, idx]
        vs = vb[g_idx, idx]
        s = jnp.einsum('qghd,qgnbd->qghnb', qi, ks) * scale
        kpos = idx[..., None] * SEL_BLOCK + jnp.arange(SEL_BLOCK)
        mask_s = sel_valid[..., None] & (kpos <= t[:, None, None, None])
        nk = n_top * SEL_BLOCK
        p_s = masked_softmax(s.reshape(Q_BLOCK, G, hpg, nk),
                             mask_s.reshape(Q_BLOCK, G, 1, nk))
        o_s = jnp.einsum('qghk,qgkd->qghd', p_s.astype(vs.dtype),
                         vs.reshape(Q_BLOCK, G, nk, dh))
        kw = lax.dynamic_slice_in_dim(k_pad, b * Q_BLOCK, Q_BLOCK + WINDOW, axis=0)
        vw = lax.dynamic_slice_in_dim(v_pad, b * Q_BLOCK, Q_BLOCK + WINDOW, axis=0)
        wpos = b * Q_BLOCK - WINDOW + jnp.arange(Q_BLOCK + WINDOW)
        diff = t[:, None] - wpos[None, :]
        mask_w = ((wpos[None, :] >= 0) & (diff >= 0) & (diff < WINDOW))[:, None, None, :]
        s = jnp.einsum('qghd,kgd->qghk', qi, kw) * scale
        p_w = masked_softmax(s, mask_w)
        o_w = jnp.einsum('qghk,kgd->qghd', p_w.astype(vw.dtype), vw)
        g = jax.nn.sigmoid(gi.astype(jnp.float32))
        o = g[..., 0:1] * o_c + g[..., 1:2] * o_s + g[..., 2:3] * o_w
        return o.reshape(Q_BLOCK, H * dh).astype(q.dtype)

    out = lax.map(block, (qb, gb, jnp.arange(n_qb)))
    return out.reshape(S, H * dh)


def retention_sequence(q, k, v):
    S = q.shape[0]
    H, dk, dv, C = RET_HEADS, RET_QK_DIM, RET_V_DIM, RET_CHUNK
    nc = S // C
    log_g = jnp.log(1.0 - 2.0 ** (-5.0 - jnp.arange(H, dtype=jnp.float32)))
    n = jnp.arange(C, dtype=jnp.float32)
    diff = n[:, None] - n[None, :]
    decay_mask = jnp.where(diff >= 0, jnp.exp(jnp.maximum(diff, 0.0)[None] * log_g[:, None, None]), 0.0)
    q_dec = jnp.exp((n[None, :] + 1.0) * log_g[:, None])
    k_dec = jnp.exp((C - 1.0 - n[None, :]) * log_g[:, None])
    chunk_dec = jnp.exp(C * log_g)

    qc = q.reshape(nc, C, H, dk)
    kc = k.reshape(nc, C, H, dk)
    vc = v.reshape(nc, C, H, dv)
    att = jnp.einsum('nihd,njhd->nhij', qc, kc) * decay_mask[None]
    o_inner = jnp.einsum('nhij,njhe->nihe', att, vc)
    kv = jnp.einsum('njhd,hj,njhe->nhde', kc, k_dec, vc).astype(jnp.float32)

    def step(R, kv_i):
        return chunk_dec[:, None, None] * R + kv_i, R

    _, R_prev = lax.scan(step, jnp.zeros((H, dk, dv), jnp.float32), kv)
    o_cross = jnp.einsum('nihd,hi,nhde->nihe', qc, q_dec, R_prev)
    o = (o_inner + o_cross).reshape(S, H, dv).astype(jnp.float32)
    mu = jnp.mean(o, axis=-1, keepdims=True)
    var = jnp.mean(jnp.square(o - mu), axis=-1, keepdims=True)
    o = (o - mu) * lax.rsqrt(var + GN_EPS)
    return o.reshape(S, H * dv).astype(q.dtype)


def setup_inputs(seed: int = 0) -> dict:
    key = jax.random.key(seed)
    ks = jax.random.split(key, 16)
    f32 = jnp.float32
    nrm = lambda k, shape, fan_in: jax.random.normal(k, shape, f32) * (fan_in ** -0.5)
    cmp_in = CMP_LEN * NSA_HEAD_DIM
    return {
        "x": jax.random.normal(ks[0], (BATCH, SEQ, D_MODEL), f32),
        "norm_pre": 1.0 + 0.05 * jax.random.normal(ks[1], (D_MODEL,), f32),
        "w_in": nrm(ks[2], (D_MODEL, PROJ_TOTAL), D_MODEL),
        "b_nsa_gate": 0.1 * jax.random.normal(ks[3], (NSA_HEADS * 3,), f32),
        "cmp_pe_k": 0.02 * jax.random.normal(ks[4], (CMP_LEN, NSA_HEAD_DIM), f32),
        "cmp_w1_k": nrm(ks[5], (cmp_in, CMP_HIDDEN), cmp_in),
        "cmp_w2_k": nrm(ks[6], (CMP_HIDDEN, NSA_HEAD_DIM), CMP_HIDDEN),
        "cmp_pe_v": 0.02 * jax.random.normal(ks[7], (CMP_LEN, NSA_HEAD_DIM), f32),
        "cmp_w1_v": nrm(ks[8], (cmp_in, CMP_HIDDEN), cmp_in),
        "cmp_w2_v": nrm(ks[9], (CMP_HIDDEN, NSA_HEAD_DIM), CMP_HIDDEN),
        "w_nsa_o": nrm(ks[10], (NSA_WIDTH, D_MODEL), NSA_WIDTH),
        "w_ret_o": nrm(ks[11], (RET_V_WIDTH, D_MODEL), RET_V_WIDTH),
        "w_out": nrm(ks[12], (D_MODEL, D_MODEL), D_MODEL),
        "norm_post": 1.0 + 0.05 * jax.random.normal(ks[13], (D_MODEL,), f32),
    }


def reference(x, norm_pre, w_in, b_nsa_gate, cmp_pe_k, cmp_w1_k, cmp_w2_k,
              cmp_pe_v, cmp_w1_v, cmp_w2_v, w_nsa_o, w_ret_o, w_out, norm_post):
    B, S, _ = x.shape
    pos = jnp.arange(S, dtype=jnp.float32)
    nsa_inv = 1.0 / (ROPE_THETA ** (jnp.arange(0, ROPE_DIM, 2, dtype=jnp.float32) / ROPE_DIM))
    nsa_ang = pos[:, None] * nsa_inv[None, :]
    ret_inv = 1.0 / (RET_ROPE_BASE ** jnp.linspace(0.0, 1.0, RET_QK_DIM // 2, dtype=jnp.float32))
    ret_ang = pos[:, None] * ret_inv[None, :]

    for _ in range(DEPTH):
        h = rms_norm(x, norm_pre)
        proj = h @ w_in
        (q_n, k_c, v_c, k_s, v_s, k_w, v_w, g_n, z_n,
         q_r, k_r, v_r, z_r, m_a, m_b) = jnp.split(proj, split_points(), axis=-1)

        kvshape = (B, S, NSA_KV_GROUPS, NSA_HEAD_DIM)
        q_n = apply_rotary(q_n.reshape(B, S, NSA_HEADS, NSA_HEAD_DIM), nsa_ang)
        k_c = apply_rotary(k_c.reshape(kvshape), nsa_ang)
        k_s = apply_rotary(k_s.reshape(kvshape), nsa_ang)
        k_w = apply_rotary(k_w.reshape(kvshape), nsa_ang)
        gate_n = (g_n + b_nsa_gate).reshape(B, S, NSA_HEADS, 3)
        nsa_fn = lambda a, b_, c, d, e, f, g, gt: nsa_sequence(
            a, b_, c, d, e, f, g, gt,
            cmp_pe_k, cmp_w1_k, cmp_w2_k, cmp_pe_v, cmp_w1_v, cmp_w2_v)
        o_a = jax.vmap(nsa_fn)(q_n, k_c, v_c.reshape(kvshape), k_s, v_s.reshape(kvshape),
                               k_w, v_w.reshape(kvshape), gate_n)
        y_a = (o_a * jax.nn.silu(z_n)) @ w_nsa_o

        q_r = apply_rotary(q_r.reshape(B, S, RET_HEADS, RET_QK_DIM), ret_ang)
        k_r = apply_rotary(k_r.reshape(B, S, RET_HEADS, RET_QK_DIM), ret_ang) * (RET_QK_DIM ** -0.5)
        o_b = jax.vmap(retention_sequence)(q_r, k_r, v_r.reshape(B, S, RET_HEADS, RET_V_DIM))
        y_b = (o_b * jax.nn.silu(z_r)) @ w_ret_o

        merged = jax.nn.sigmoid(m_a) * y_a + jax.nn.sigmoid(m_b) * y_b
        x = x + rms_norm(merged @ w_out, norm_post)
    return x
```

```python
import functools
import math

import jax
import jax.numpy as jnp
from jax import lax
from jax.experimental import pallas as pl
from jax.experimental.pallas import tpu as pltpu

F32 = jnp.float32
BF16 = jnp.bfloat16

D_MODEL = 1024
NSA_HEADS = 8
NSA_GROUPS = 2
NSA_HPG = NSA_HEADS // NSA_GROUPS
NSA_DH = 64
CMP_LEN = 32
CMP_STRIDE = 16
CMP_HIDDEN = 256
SEL_BLOCK = 64
SEL_TOPN = 16
WINDOW = 512
ROPE_THETA = 500000.0
ROPE_DIM = NSA_DH // 4
NSA_SCALE = NSA_DH ** -0.5
RET_HEADS = 4
RET_DK = 128
RET_DV = 256
RET_ROPE_BASE = 10000.0
RET_SCALE = RET_DK ** -0.5
RMS_EPS = 1e-6
GN_EPS = 1e-6
PROJ_SIZES = (512, 128, 128, 128, 128, 128, 128, 24, 512, 512, 512, 1024, 1024, 1024, 1024)

LANE = 128
V7X_VMEM_LIMIT_BYTES = 56 * 1024 * 1024

ROW_TILE = 256
TQ = 128
TK = 512
BLK_PER_TILE = TK // SEL_BLOCK
WIN_KEYS = WINDOW + TQ
RET_CHUNK = 256
V_ROWS = 80
NEG = -1e30

_PAD_SIZES = tuple(128 if s == 24 else s for s in PROJ_SIZES)
_OFF = [0]
for _s in _PAD_SIZES:
    _OFF.append(_OFF[-1] + _s)
PROJ_PAD = _OFF[-1]
HI_COLS = _OFF[2]


def _dot(a, b):
    return jnp.dot(a, b, preferred_element_type=F32)


def _split(a):
    hi = a.astype(BF16)
    lo = (a - hi.astype(F32)).astype(BF16)
    return hi, lo


def _dot3(a_hi, a_lo, b_hi, b_lo):
    return _dot(a_hi, b_hi) + _dot(a_lo, b_hi) + _dot(a_hi, b_lo)


def _params(sem, vmem_bytes):
    return pltpu.CompilerParams(dimension_semantics=sem, vmem_limit_bytes=vmem_bytes)


def _inproj_kernel(x_ref, g_ref, whi_ref, wlo_ref, bias_ref, cn_ref, san_ref, sbn_ref,
                   cr_ref, sr_ref,
                   qn_ref, kv_ref, gate_ref, zn_ref, qr_ref, kr_ref, vr_ref, zr_ref,
                   ma_ref, mb_ref):
    x = x_ref[...]
    ms = jnp.mean(x * x, axis=-1, keepdims=True)
    h = x * lax.rsqrt(ms + RMS_EPS) * g_ref[...]
    h_hi, h_lo = _split(h)

    def mm(lo, hi):
        return _dot(h_hi, whi_ref[:, lo:hi])

    cn, san, sbn = cn_ref[...], san_ref[...], sbn_ref[...]
    cr, sr = cr_ref[...], sr_ref[...]

    def rot_nsa(p):
        return p * cn + pltpu.roll(p, 8, 1) * san + pltpu.roll(p, LANE - 8, 1) * sbn

    def rot_ret(p):
        return p * cr + pltpu.roll(p, LANE // 2, 1) * sr

    p = (mm(0, HI_COLS) + _dot(h_lo, whi_ref[:, 0:HI_COLS])
         + _dot(h_hi, wlo_ref[...]))
    for c in range(4):
        qn_ref[:, c * LANE:(c + 1) * LANE] = rot_nsa(p[:, c * LANE:(c + 1) * LANE]) * NSA_SCALE
    kv_ref[:, 0:LANE] = rot_nsa(p[:, 4 * LANE:5 * LANE])

    p = mm(_OFF[2], _OFF[8])
    kv_ref[:, 1 * LANE:2 * LANE] = p[:, 0 * LANE:1 * LANE]
    kv_ref[:, 2 * LANE:3 * LANE] = rot_nsa(p[:, 1 * LANE:2 * LANE])
    kv_ref[:, 3 * LANE:4 * LANE] = p[:, 2 * LANE:3 * LANE]
    kv_ref[:, 4 * LANE:5 * LANE] = rot_nsa(p[:, 3 * LANE:4 * LANE])
    kv_ref[:, 5 * LANE:6 * LANE] = p[:, 4 * LANE:5 * LANE]
    gate_ref[...] = jax.nn.sigmoid(p[:, 5 * LANE:6 * LANE] + bias_ref[...])

    z = mm(_OFF[8], _OFF[9])
    zn_ref[...] = z * jax.nn.sigmoid(z)

    p = mm(_OFF[9], _OFF[11])
    for c in range(4):
        qr_ref[:, c * LANE:(c + 1) * LANE] = rot_ret(p[:, c * LANE:(c + 1) * LANE]).astype(BF16)
    for c in range(4):
        kr_ref[:, c * LANE:(c + 1) * LANE] = (
            rot_ret(p[:, (4 + c) * LANE:(5 + c) * LANE]) * RET_SCALE).astype(BF16)

    vr_ref[...] = mm(_OFF[11], _OFF[12]).astype(BF16)
    z = mm(_OFF[12], _OFF[13])
    zr_ref[...] = z * jax.nn.sigmoid(z)
    ma_ref[...] = jax.nn.sigmoid(mm(_OFF[13], _OFF[14]))
    mb_ref[...] = jax.nn.sigmoid(mm(_OFF[14], _OFF[15]))


def _inproj(x2, norm_pre, w_hi, w_lo, bias, tabs):
    S = x2.shape[0]
    tm = ROW_TILE
    row = lambda w: pl.BlockSpec((tm, w), lambda i: (i, 0))
    whole = lambda a: pl.BlockSpec(a.shape, lambda i: (0,) * a.ndim)
    widths = (512, 768, 128, 512, 512, 512, 1024, 1024, 1024, 1024)
    dtypes = (F32, F32, F32, F32, BF16, BF16, BF16, F32, F32, F32)
    return pl.pallas_call(
        _inproj_kernel,
        grid=(S // tm,),
        in_specs=[row(D_MODEL), whole(norm_pre), whole(w_hi), whole(w_lo), whole(bias)]
        + [row(LANE)] * 5,
        out_specs=[row(w) for w in widths],
        out_shape=[jax.ShapeDtypeStruct((S, w), d) for w, d in zip(widths, dtypes)],
        compiler_params=_params(("arbitrary",), V7X_VMEM_LIMIT_BYTES),
        name="inproj",
    )(x2, norm_pre, w_hi, w_lo, bias, *tabs)


def _compress_kernel(flat_ref, pe_ref, w1_ref, w2_ref, o_ref):
    f_hi, f_lo = _split(flat_ref[...] + pe_ref[...])
    w1_hi, w1_lo = _split(w1_ref[...])
    hid = _dot3(f_hi, f_lo, w1_hi, w1_lo)
    act = hid * (0.5 * (1.0 + jnp.tanh(math.sqrt(2.0 / math.pi) * (hid + 0.044715 * (hid * hid * hid)))))
    a_hi, a_lo = _split(act)
    w2_hi, w2_lo = _split(w2_ref[...])
    o_ref[...] = _dot3(a_hi, a_lo, w2_hi, w2_lo)


def _compress(flat, pe_flat, w1, w2):
    _, G, NC, W = flat.shape
    return pl.pallas_call(
        _compress_kernel,
        grid=(2, G),
        in_specs=[
            pl.BlockSpec((None, None, NC, W), lambda k, g: (k, g, 0, 0)),
            pl.BlockSpec((None, 1, W), lambda k, g: (k, 0, 0)),
            pl.BlockSpec((None, W, CMP_HIDDEN), lambda k, g: (k, 0, 0)),
            pl.BlockSpec((None, CMP_HIDDEN, NSA_DH), lambda k, g: (k, 0, 0)),
        ],
        out_specs=pl.BlockSpec((None, None, NC, NSA_DH), lambda k, g: (k, g, 0, 0)),
        out_shape=jax.ShapeDtypeStruct((2, G, NC, NSA_DH), F32),
        compiler_params=_params(("arbitrary", "arbitrary"), V7X_VMEM_LIMIT_BYTES),
        name="compress",
    )(flat, pe_flat, w1, w2)


def _nsa_kernel(qT_ref, gT_ref, kch_ref, kcl_ref, vcT_ref, ovT_ref, ks_ref, vsT_ref,
                kw_ref, vwT_ref, o_ref, sb_ref):
    i = pl.program_id(1)
    NC = kch_ref.shape[0]
    NB = ovT_ref.shape[0]
    heads = range(NSA_HPG)
    lanes = lambda a, h: a[:, h * TQ:(h + 1) * TQ]

    qT = qT_ref[...]
    Q = jnp.concatenate([qT[h * NSA_DH:(h + 1) * NSA_DH, :] for h in heads], axis=1)
    q_hi, q_lo = _split(Q)
    t = i * TQ + lax.broadcasted_iota(jnp.int32, (1, TQ), 1)

    s = _dot(kch_ref[...], q_hi) + _dot(kch_ref[...], q_lo) + _dot(kcl_ref[...], q_hi)
    c_end = lax.broadcasted_iota(jnp.int32, (NC, TQ), 0) * CMP_STRIDE + (CMP_LEN - 1)
    mask_c = c_end <= t
    p_sum = jnp.zeros((NC, TQ), F32)
    p_heads = []
    for h in heads:
        sm = jnp.where(mask_c, lanes(s, h), -jnp.inf)
        m = jnp.max(sm, axis=0, keepdims=True)
        m = jnp.where(m == -jnp.inf, 0.0, m)
        p = jnp.exp(sm - m)
        l = jnp.sum(p, axis=0, keepdims=True)
        pn = p * (1.0 / jnp.maximum(l, jnp.finfo(F32).tiny))
        p_sum = p_sum + pn
        p_heads.append(pn.astype(BF16))
    ocT = _dot(vcT_ref[...], jnp.concatenate(p_heads, axis=1))

    ps_hi, ps_lo = _split(p_sum)
    imp = _dot(ovT_ref[...], ps_hi) + _dot(ovT_ref[...], ps_lo)
    blk = lax.broadcasted_iota(jnp.int32, (NB, TQ), 0)
    blk_f = blk.astype(F32)
    cur = jnp.right_shift(t, SEL_BLOCK.bit_length() - 1)
    valid = blk <= cur
    forced = (blk == 0) | (blk == cur) | (blk == cur - 1)
    score = jnp.where(forced, jnp.inf, jnp.where(valid, imp, -jnp.inf))
    sel = jnp.zeros((NB, TQ), F32)
    for _ in range(min(SEL_TOPN, NB)):
        mx = jnp.max(score, axis=0, keepdims=True)
        idx = jnp.min(jnp.where(score == mx, blk_f, float(NB)), axis=0, keepdims=True)
        hit = blk_f == idx
        sel = jnp.where(hit, 1.0, sel)
        score = jnp.where(hit, -jnp.inf, score)
    sb_ref[...] = jnp.where(valid, jnp.where(sel > 0.0, 0.0, NEG), NEG)

    def sel_tile(j, carry, causal):
        m, acc = carry
        k0 = pl.multiple_of(j * TK, TK)
        s = _dot(ks_ref[pl.ds(k0, TK), :], q_hi)
        sbt = sb_ref[pl.ds(pl.multiple_of(j * BLK_PER_TILE, BLK_PER_TILE), BLK_PER_TILE), :]
        bias = jnp.concatenate(
            [jnp.broadcast_to(sbt[r:r + 1, :], (SEL_BLOCK, TQ)) for r in range(BLK_PER_TILE)],
            axis=0)
        if causal:
            key = k0 + lax.broadcasted_iota(jnp.int32, (TK, TQ), 0)
            bias = jnp.where(key <= t, bias, NEG)
        s = jnp.concatenate([lanes(s, h) + bias for h in heads], axis=1)
        m_new = jnp.maximum(m, jnp.max(s, axis=0, keepdims=True))
        alpha = jnp.exp(m - m_new)
        p = jnp.exp(s - m_new).astype(BF16)
        acc = acc * alpha + _dot(vsT_ref[:, pl.ds(k0, TK)], p)
        return m_new, acc

    jd = lax.div(i, TK // TQ)
    carry = (jnp.full((1, NSA_HPG * TQ), NEG, F32), jnp.zeros((V_ROWS, NSA_HPG * TQ), F32))
    carry = lax.fori_loop(0, jd, functools.partial(sel_tile, causal=False), carry)
    _, acc = sel_tile(jd, carry, True)
    osT = acc[0:NSA_DH, :] * (1.0 / acc[NSA_DH:NSA_DH + 1, :])

    w0 = pl.multiple_of(jnp.maximum(i * TQ - WINDOW, 0), TQ)
    s = _dot(kw_ref[pl.ds(w0, WIN_KEYS), :], q_hi)
    key = w0 + lax.broadcasted_iota(jnp.int32, (WIN_KEYS, TQ), 0)
    mask_w = (key <= t) & (key > t - WINDOW)
    p_heads = []
    for h in heads:
        sm = jnp.where(mask_w, lanes(s, h), NEG)
        m = jnp.max(sm, axis=0, keepdims=True)
        p_heads.append(jnp.exp(sm - m).astype(BF16))
    accw = _dot(vwT_ref[:, pl.ds(w0, WIN_KEYS)], jnp.concatenate(p_heads, axis=1))
    owT = accw[0:NSA_DH, :] * (1.0 / accw[NSA_DH:NSA_DH + 1, :])

    gT = gT_ref[...]
    outs = []
    for h in heads:
        g_c, g_s, g_w = (gT[3 * h + b:3 * h + b + 1, :] for b in range(3))
        outs.append(g_c * lanes(ocT, h)[0:NSA_DH, :] + g_s * lanes(osT, h) + g_w * lanes(owT, h))
    o_ref[...] = jnp.concatenate(outs, axis=0).T


def _nsa(qT, gT, kc_hi, kc_lo, vcT, ovT, ks, vsT, kw, vwT):
    G, S = ks.shape[0], ks.shape[1]
    NC, NB = kc_hi.shape[1], ovT.shape[0]
    per_g = lambda a: pl.BlockSpec((None,) + a.shape[1:], lambda g, i: (g,) + (0,) * (a.ndim - 1))
    return pl.pallas_call(
        _nsa_kernel,
        grid=(G, S // TQ),
        in_specs=[
            pl.BlockSpec((NSA_HPG * NSA_DH, TQ), lambda g, i: (g, i)),
            pl.BlockSpec((None, 16, TQ), lambda g, i: (g, 0, i)),
            per_g(kc_hi), per_g(kc_lo), per_g(vcT),
            pl.BlockSpec(ovT.shape, lambda g, i: (0, 0)),
            per_g(ks), per_g(vsT), per_g(kw), per_g(vwT),
        ],
        out_specs=pl.BlockSpec((TQ, NSA_HPG * NSA_DH), lambda g, i: (i, g)),
        out_shape=jax.ShapeDtypeStruct((S, NSA_HEADS * NSA_DH), F32),
        scratch_shapes=[pltpu.VMEM((NB, TQ), F32)],
        compiler_params=_params(("arbitrary", "arbitrary"), V7X_VMEM_LIMIT_BYTES),
        name="nsa",
    )(qT, gT, kc_hi, kc_lo, vcT, ovT, ks, vsT, kw, vwT)


def _ret_kernel(q_ref, k_ref, v_ref, o_ref, r_ref, dm_ref, qd_ref, kd_ref):
    C = RET_CHUNK
    log_g = [math.log(1.0 - 2.0 ** (-5.0 - h)) for h in range(RET_HEADS)]

    @pl.when(pl.program_id(0) == 0)
    def _():
        r_ref[...] = jnp.zeros(r_ref.shape, F32)
        diff = (lax.broadcasted_iota(jnp.int32, (C, C), 0)
                - lax.broadcasted_iota(jnp.int32, (C, C), 1)).astype(F32)
        n = lax.broadcasted_iota(jnp.int32, (C, RET_DK), 0).astype(F32)
        for h in range(RET_HEADS):
            dm_ref[h] = jnp.where(diff >= 0.0, jnp.exp(jnp.maximum(diff, 0.0) * log_g[h]), 0.0)
            qd_ref[h] = jnp.exp((n + 1.0) * log_g[h])
            kd_ref[h] = jnp.exp((C - 1.0 - n) * log_g[h])

    for h in range(RET_HEADS):
        q = q_ref[:, h * RET_DK:(h + 1) * RET_DK]
        k = k_ref[:, h * RET_DK:(h + 1) * RET_DK]
        v = v_ref[:, h * RET_DV:(h + 1) * RET_DV]
        att = lax.dot_general(q, k, (((1,), (1,)), ((), ())), preferred_element_type=F32)
        o = _dot((att * dm_ref[h]).astype(BF16), v)
        r = r_ref[h]
        r_hi, r_lo = _split(r)
        qd = (q.astype(F32) * qd_ref[h]).astype(BF16)
        o = o + _dot(qd, r_hi) + _dot(qd, r_lo)
        kd = (k.astype(F32) * kd_ref[h]).astype(BF16)
        r_ref[h] = math.exp(C * log_g[h]) * r + lax.dot_general(
            kd, v, (((0,), (0,)), ((), ())), preferred_element_type=F32)
        mu = jnp.mean(o, axis=-1, keepdims=True)
        d = o - mu
        var = jnp.mean(d * d, axis=-1, keepdims=True)
        o_ref[:, h * RET_DV:(h + 1) * RET_DV] = d * lax.rsqrt(var + GN_EPS)


def _retention(qr, kr, vr):
    S = qr.shape[0]
    C = RET_CHUNK
    row = lambda w: pl.BlockSpec((C, w), lambda n: (n, 0))
    return pl.pallas_call(
        _ret_kernel,
        grid=(S // C,),
        in_specs=[row(RET_HEADS * RET_DK), row(RET_HEADS * RET_DK), row(RET_HEADS * RET_DV)],
        out_specs=row(RET_HEADS * RET_DV),
        out_shape=jax.ShapeDtypeStruct((S, RET_HEADS * RET_DV), F32),
        scratch_shapes=[
            pltpu.VMEM((RET_HEADS, RET_DK, RET_DV), F32),
            pltpu.VMEM((RET_HEADS, C, C), F32),
            pltpu.VMEM((RET_HEADS, C, RET_DK), F32),
            pltpu.VMEM((RET_HEADS, C, RET_DK), F32),
        ],
        compiler_params=_params(("arbitrary",), V7X_VMEM_LIMIT_BYTES),
        name="retention",
    )(qr, kr, vr)


def _post_kernel(x_ref, oa_ref, zn_ref, ob_ref, zr_ref, ma_ref, mb_ref,
                 wa_ref, wb_ref, wo_ref, g_ref, out_ref):
    ya = _dot((oa_ref[...] * zn_ref[...]).astype(BF16), wa_ref[...])
    yb = _dot((ob_ref[...] * zr_ref[...]).astype(BF16), wb_ref[...])
    merged = ma_ref[...] * ya + mb_ref[...] * yb
    y = _dot(merged.astype(BF16), wo_ref[...])
    ms = jnp.mean(y * y, axis=-1, keepdims=True)
    out_ref[...] = x_ref[...] + y * lax.rsqrt(ms + RMS_EPS) * g_ref[...]


def _post(x2, oa, zn, ob, zr, ma, mb, wa, wb, wo, g_post):
    S = x2.shape[0]
    tm = ROW_TILE
    row = lambda a: pl.BlockSpec((tm, a.shape[1]), lambda i: (i, 0))
    whole = lambda a: pl.BlockSpec(a.shape, lambda i: (0, 0))
    rows = (x2, oa, zn, ob, zr, ma, mb)
    consts = (wa, wb, wo, g_post)
    return pl.pallas_call(
        _post_kernel,
        grid=(S // tm,),
        in_specs=[row(a) for a in rows] + [whole(a) for a in consts],
        out_specs=row(x2),
        out_shape=jax.ShapeDtypeStruct(x2.shape, x2.dtype),
        compiler_params=_params(("arbitrary",), V7X_VMEM_LIMIT_BYTES),
        name="post",
    )(*rows, *consts)


def _rotary_tables(S):
    pos = jnp.arange(S, dtype=F32)
    nsa_inv = 1.0 / (ROPE_THETA ** (jnp.arange(0, ROPE_DIM, 2, dtype=F32) / ROPE_DIM))
    ang = pos[:, None] * nsa_inv[None, :]
    c, s = jnp.cos(ang), jnp.sin(ang)
    half = ROPE_DIM // 2
    rest = NSA_DH - ROPE_DIM
    one, zero, zh = jnp.ones((S, rest), F32), jnp.zeros((S, rest), F32), jnp.zeros((S, half), F32)
    per_head = lambda parts: jnp.tile(jnp.concatenate(parts, axis=1), (1, LANE // NSA_DH))
    cn = per_head([c, c, one])
    san = per_head([zh, s, zero])
    sbn = per_head([-s, zh, zero])
    ret_inv = 1.0 / (RET_ROPE_BASE ** jnp.linspace(0.0, 1.0, RET_DK // 2, dtype=F32))
    rang = pos[:, None] * ret_inv[None, :]
    rc, rs = jnp.cos(rang), jnp.sin(rang)
    cr = jnp.concatenate([rc, rc], axis=1)
    sr = jnp.concatenate([-rs, rs], axis=1)
    return cn, san, sbn, cr, sr


def _value_rows(vT):
    G, d, n = vT.shape
    return jnp.concatenate(
        [vT, jnp.ones((G, 1, n), vT.dtype), jnp.zeros((G, V_ROWS - d - 1, n), vT.dtype)],
        axis=1).astype(BF16)


def _per_group(a):
    S = a.shape[0]
    return a.reshape(S, NSA_GROUPS, NSA_DH).transpose(1, 0, 2)


def kernel(x, norm_pre, w_in, b_nsa_gate, cmp_pe_k, cmp_w1_k, cmp_w2_k, cmp_pe_v, cmp_w1_v,
           cmp_w2_v, w_nsa_o, w_ret_o, w_out, norm_post):
    B, S, _ = x.shape
    assert B == 1 and S % TK == 0 and S >= WIN_KEYS
    x2 = x.reshape(S, D_MODEL)

    segs, off = [], 0
    for sz, pad in zip(PROJ_SIZES, _PAD_SIZES):
        segs.append(jnp.pad(w_in[:, off:off + sz], ((0, 0), (0, pad - sz))))
        off += sz
    w_pad = jnp.concatenate(segs, axis=1)
    w_hi = w_pad.astype(BF16)
    w_lo = (w_pad[:, :HI_COLS] - w_hi[:, :HI_COLS].astype(F32)).astype(BF16)
    bias = jnp.pad(b_nsa_gate, (0, LANE - b_nsa_gate.shape[0])).reshape(1, LANE)

    qn, kv, gate, zn, qr, kr, vr, zr, ma, mb = _inproj(
        x2, norm_pre.reshape(1, D_MODEL), w_hi, w_lo, bias, _rotary_tables(S))

    NC = S // CMP_STRIDE
    NB = S // SEL_BLOCK
    def chunks(a):
        c = a.reshape(NC, CMP_STRIDE, NSA_GROUPS, NSA_DH).transpose(2, 0, 1, 3)
        c = c.reshape(NSA_GROUPS, NC, CMP_STRIDE * NSA_DH)
        return jnp.concatenate([c, jnp.roll(c, -1, axis=1)], axis=-1)
    flat = jnp.stack([chunks(kv[:, 0:LANE]), chunks(kv[:, LANE:2 * LANE])])
    pe_flat = jnp.stack([cmp_pe_k.reshape(1, -1), cmp_pe_v.reshape(1, -1)])
    cmp = _compress(flat, pe_flat, jnp.stack([cmp_w1_k, cmp_w1_v]), jnp.stack([cmp_w2_k, cmp_w2_v]))
    kc_hi = cmp[0].astype(BF16)
    kc_lo = (cmp[0] - kc_hi.astype(F32)).astype(BF16)
    vcT = _value_rows(cmp[1].transpose(0, 2, 1))

    c0 = jnp.arange(NC)[None, :] * CMP_STRIDE
    n0 = jnp.arange(NB)[:, None] * SEL_BLOCK
    n_cmp = (S - CMP_LEN) // CMP_STRIDE + 1
    ovT = ((c0 < n0 + SEL_BLOCK) & (c0 + CMP_LEN > n0) & (jnp.arange(NC)[None, :] < n_cmp)).astype(BF16)

    ks = _per_group(kv[:, 2 * LANE:3 * LANE]).astype(BF16)
    vsT = _value_rows(_per_group(kv[:, 3 * LANE:4 * LANE]).transpose(0, 2, 1))
    kw = _per_group(kv[:, 4 * LANE:5 * LANE]).astype(BF16)
    vwT = _value_rows(_per_group(kv[:, 5 * LANE:6 * LANE]).transpose(0, 2, 1))
    gT = gate[:, :NSA_HEADS * 3].reshape(S, NSA_GROUPS, NSA_HPG * 3).transpose(1, 2, 0)
    gT = jnp.pad(gT, ((0, 0), (0, 16 - NSA_HPG * 3), (0, 0)))

    oa = _nsa(qn.T, gT, kc_hi, kc_lo, vcT, ovT, ks, vsT, kw, vwT)
    ob = _retention(qr, kr, vr)

    out = _post(x2, oa, zn, ob, zr, ma, mb, w_nsa_o.astype(BF16), w_ret_o.astype(BF16),
                w_out.astype(BF16), norm_post.reshape(1, D_MODEL))
    return out.reshape(B, S, D_MODEL)
```

```python
import functools
import math

import jax
import jax.numpy as jnp
from jax import lax
from jax.experimental import pallas as pl
from jax.experimental.pallas import tpu as pltpu

F32 = jnp.float32
BF16 = jnp.bfloat16

D_MODEL = 1024
NSA_HEADS = 8
NSA_GROUPS = 2
NSA_HPG = NSA_HEADS // NSA_GROUPS
NSA_DH = 64
CMP_LEN = 32
CMP_STRIDE = 16
CMP_HIDDEN = 256
SEL_BLOCK = 64
SEL_TOPN = 16
WINDOW = 512
ROPE_THETA = 500000.0
ROPE_DIM = NSA_DH // 4
NSA_SCALE = NSA_DH ** -0.5
RET_HEADS = 4
RET_DK = 128
RET_DV = 256
RET_ROPE_BASE = 10000.0
RET_SCALE = RET_DK ** -0.5
RMS_EPS = 1e-6
GN_EPS = 1e-6
PROJ_SIZES = (512, 128, 128, 128, 128, 128, 128, 24, 512, 512, 512, 1024, 1024, 1024, 1024)

LANE = 128
V7X_VMEM_LIMIT_BYTES = 56 * 1024 * 1024

ROW_TILE = 256
TQ = 128
TK = 512
BLK_PER_TILE = TK // SEL_BLOCK
WIN_KEYS = WINDOW + TQ
RET_CHUNK = 256
V_ROWS = 80
NEG = -1e30
LOG2E = math.log2(math.e)
K_LANES = LANE

_PAD_SIZES = tuple(128 if s == 24 else s for s in PROJ_SIZES)
_OFF = [0]
for _s in _PAD_SIZES:
    _OFF.append(_OFF[-1] + _s)
PROJ_PAD = _OFF[-1]
HI_COLS = _OFF[2]


def _dot(a, b):
    return jnp.dot(a, b, preferred_element_type=F32)


def _split(a):
    hi = a.astype(BF16)
    lo = (a - hi.astype(F32)).astype(BF16)
    return hi, lo


def _dot3(a_hi, a_lo, b_hi, b_lo):
    return _dot(a_hi, b_hi) + _dot(a_lo, b_hi) + _dot(a_hi, b_lo)


def _params(sem, vmem_bytes):
    return pltpu.CompilerParams(dimension_semantics=sem, vmem_limit_bytes=vmem_bytes)


def _inproj_kernel(x_ref, g_ref, whi_ref, wlo_ref, bias_ref, cn_ref, san_ref, sbn_ref,
                   cr_ref, sr_ref,
                   qn_ref, kv_ref, gate_ref, zn_ref, qr_ref, kr_ref, vr_ref, zr_ref,
                   ma_ref, mb_ref):
    x = x_ref[...]
    ms = jnp.mean(x * x, axis=-1, keepdims=True)
    h = x * lax.rsqrt(ms + RMS_EPS) * g_ref[...]
    h_hi, h_lo = _split(h)

    def mm(lo, hi):
        return _dot(h_hi, whi_ref[:, lo:hi])

    cn, san, sbn = cn_ref[...], san_ref[...], sbn_ref[...]
    cr, sr = cr_ref[...], sr_ref[...]

    def rot_nsa(p):
        return p * cn + pltpu.roll(p, 8, 1) * san + pltpu.roll(p, LANE - 8, 1) * sbn

    def rot_ret(p):
        return p * cr + pltpu.roll(p, LANE // 2, 1) * sr

    p = (mm(0, HI_COLS) + _dot(h_lo, whi_ref[:, 0:HI_COLS])
         + _dot(h_hi, wlo_ref[...]))
    for c in range(4):
        qn_ref[:, c * LANE:(c + 1) * LANE] = rot_nsa(p[:, c * LANE:(c + 1) * LANE]) * NSA_SCALE
    kv_ref[:, 0:LANE] = rot_nsa(p[:, 4 * LANE:5 * LANE])

    p = mm(_OFF[2], _OFF[8])
    kv_ref[:, 1 * LANE:2 * LANE] = p[:, 0 * LANE:1 * LANE]
    kv_ref[:, 2 * LANE:3 * LANE] = rot_nsa(p[:, 1 * LANE:2 * LANE])
    kv_ref[:, 3 * LANE:4 * LANE] = p[:, 2 * LANE:3 * LANE]
    kv_ref[:, 4 * LANE:5 * LANE] = rot_nsa(p[:, 3 * LANE:4 * LANE])
    kv_ref[:, 5 * LANE:6 * LANE] = p[:, 4 * LANE:5 * LANE]
    gate_ref[...] = jax.nn.sigmoid(p[:, 5 * LANE:6 * LANE] + bias_ref[...])

    z = mm(_OFF[8], _OFF[9])
    zn_ref[...] = z * jax.nn.sigmoid(z)

    p = mm(_OFF[9], _OFF[11])
    for c in range(4):
        qr_ref[:, c * LANE:(c + 1) * LANE] = rot_ret(p[:, c * LANE:(c + 1) * LANE]).astype(BF16)
    for c in range(4):
        kr_ref[:, c * LANE:(c + 1) * LANE] = (
            rot_ret(p[:, (4 + c) * LANE:(5 + c) * LANE]) * RET_SCALE).astype(BF16)

    vr_ref[...] = mm(_OFF[11], _OFF[12]).astype(BF16)
    z = mm(_OFF[12], _OFF[13])
    zr_ref[...] = z * jax.nn.sigmoid(z)
    ma_ref[...] = jax.nn.sigmoid(mm(_OFF[13], _OFF[14]))
    mb_ref[...] = jax.nn.sigmoid(mm(_OFF[14], _OFF[15]))


def _inproj(x2, norm_pre, w_hi, w_lo, bias, tabs):
    S = x2.shape[0]
    tm = ROW_TILE
    row = lambda w: pl.BlockSpec((tm, w), lambda i: (i, 0))
    whole = lambda a: pl.BlockSpec(a.shape, lambda i: (0,) * a.ndim)
    widths = (512, 768, 128, 512, 512, 512, 1024, 1024, 1024, 1024)
    dtypes = (F32, F32, F32, F32, BF16, BF16, BF16, F32, F32, F32)
    return pl.pallas_call(
        _inproj_kernel,
        grid=(S // tm,),
        in_specs=[row(D_MODEL), whole(norm_pre), whole(w_hi), whole(w_lo), whole(bias)]
        + [row(LANE)] * 5,
        out_specs=[row(w) for w in widths],
        out_shape=[jax.ShapeDtypeStruct((S, w), d) for w, d in zip(widths, dtypes)],
        compiler_params=_params(("arbitrary",), V7X_VMEM_LIMIT_BYTES),
        name="inproj",
    )(x2, norm_pre, w_hi, w_lo, bias, *tabs)


def _compress_kernel(flat_ref, pe_ref, w1_ref, w2_ref, o_ref):
    f_hi, f_lo = _split(flat_ref[...] + pe_ref[...])
    w1_hi, w1_lo = _split(w1_ref[...])
    hid = _dot3(f_hi, f_lo, w1_hi, w1_lo)
    act = hid * (0.5 * (1.0 + jnp.tanh(math.sqrt(2.0 / math.pi) * (hid + 0.044715 * (hid * hid * hid)))))
    a_hi, a_lo = _split(act)
    w2_hi, w2_lo = _split(w2_ref[...])
    o_ref[...] = _dot3(a_hi, a_lo, w2_hi, w2_lo)


def _compress(flat, pe_flat, w1, w2):
    _, G, NC, W = flat.shape
    return pl.pallas_call(
        _compress_kernel,
        grid=(2, G),
        in_specs=[
            pl.BlockSpec((None, None, NC, W), lambda k, g: (k, g, 0, 0)),
            pl.BlockSpec((None, 1, W), lambda k, g: (k, 0, 0)),
            pl.BlockSpec((None, W, CMP_HIDDEN), lambda k, g: (k, 0, 0)),
            pl.BlockSpec((None, CMP_HIDDEN, NSA_DH), lambda k, g: (k, 0, 0)),
        ],
        out_specs=pl.BlockSpec((None, None, NC, NSA_DH), lambda k, g: (k, g, 0, 0)),
        out_shape=jax.ShapeDtypeStruct((2, G, NC, NSA_DH), F32),
        compiler_params=_params(("arbitrary", "arbitrary"), V7X_VMEM_LIMIT_BYTES),
        name="compress",
    )(flat, pe_flat, w1, w2)


def _nsa_kernel(qT_ref, gT_ref, kch_ref, kcl_ref, vcT_ref, ovT_ref, ks_ref, vsT_ref,
                kw_ref, vwT_ref, o_ref, sb_ref, sa_ref, sbuf_ref):
    i = pl.program_id(1)
    NC = kch_ref.shape[0]
    NB = ovT_ref.shape[0]
    heads = range(NSA_HPG)
    lanes = lambda a, h: a[:, h * TQ:(h + 1) * TQ]
    colmax = lambda a: jnp.max(a, axis=0, keepdims=True)

    qT = qT_ref[...] * LOG2E
    Q = jnp.concatenate([qT[h * NSA_DH:(h + 1) * NSA_DH, :] for h in heads], axis=1)
    q_hi, q_lo = _split(Q)
    t = i * TQ + lax.broadcasted_iota(jnp.int32, (1, TQ), 1)

    s = _dot(kch_ref[...], q_hi) + _dot(kch_ref[...], q_lo) + _dot(kcl_ref[...], q_hi)
    c_end = lax.broadcasted_iota(jnp.int32, (NC, TQ), 0) * CMP_STRIDE + (CMP_LEN - 1)
    mask_c = c_end <= t
    p_sum = jnp.zeros((NC, TQ), F32)
    p_heads = []
    for h in heads:
        sm = jnp.where(mask_c, lanes(s, h), -jnp.inf)
        m = colmax(sm)
        m = jnp.where(m == -jnp.inf, 0.0, m)
        p = jnp.exp2(sm - m)
        l = jnp.sum(p, axis=0, keepdims=True)
        pn = p * (1.0 / jnp.maximum(l, jnp.finfo(F32).tiny))
        p_sum = p_sum + pn
        p_heads.append(pn.astype(BF16))
    ocT = _dot(vcT_ref[...], jnp.concatenate(p_heads, axis=1))

    ps_hi, ps_lo = _split(p_sum)
    imp = _dot(ovT_ref[...], ps_hi) + _dot(ovT_ref[...], ps_lo)
    blk = lax.broadcasted_iota(jnp.int32, (NB, TQ), 0)
    blk_f = blk.astype(F32)
    cur = jnp.right_shift(t, SEL_BLOCK.bit_length() - 1)
    valid = blk <= cur
    forced = (blk == 0) | (blk == cur) | (blk == cur - 1)
    score = jnp.where(forced, jnp.inf, jnp.where(valid, imp, -jnp.inf))
    sel = jnp.zeros((NB, TQ), F32)
    for _ in range(min(SEL_TOPN, NB)):
        mx = colmax(score)
        idx = jnp.min(jnp.where(score == mx, blk_f, float(NB)), axis=0, keepdims=True)
        hit = blk_f == idx
        sel = jnp.where(hit, 1.0, sel)
        score = jnp.where(hit, -jnp.inf, score)
    sb_ref[...] = jnp.where(valid, jnp.where(sel > 0.0, 0.0, NEG), NEG)

    pad_rows = jnp.zeros((K_LANES - NSA_DH - 2 * BLK_PER_TILE, NSA_HPG * TQ), BF16)

    def scores(j):
        k0 = pl.multiple_of(j * TK, TK)
        sbt = sb_ref[pl.ds(pl.multiple_of(j * BLK_PER_TILE, BLK_PER_TILE), BLK_PER_TILE), :]
        rows = jnp.concatenate([sbt, jnp.zeros_like(sbt)], axis=0)
        rows = jnp.concatenate([rows] * NSA_HPG, axis=1).astype(BF16)
        w = jnp.concatenate([q_hi, rows, pad_rows], axis=0)
        return _dot(ks_ref[pl.ds(k0, TK), :], w)

    def attend(j, s, carry, causal):
        m, acc = carry
        k0 = pl.multiple_of(j * TK, TK)
        if causal:
            keep = k0 + lax.broadcasted_iota(jnp.int32, (TK, TQ), 0) <= t
            s = jnp.concatenate([jnp.where(keep, lanes(s, h), NEG) for h in heads], axis=1)
        m_new = jnp.maximum(m, colmax(s))
        alpha = jnp.exp2(m - m_new)
        p = jnp.exp2(s - m_new).astype(BF16)
        acc = acc * alpha + _dot(vsT_ref[:, pl.ds(k0, TK)], p)
        return m_new, acc

    n_pairs = lax.div(lax.div(i, TK // TQ), 2)
    sa_ref[...] = scores(0)

    def pair(pp, carry):
        sbuf_ref[...] = scores(2 * pp + 1)
        carry = attend(2 * pp, sa_ref[...], carry, False)
        sa_ref[...] = scores(2 * pp + 2)
        return attend(2 * pp + 1, sbuf_ref[...], carry, False)

    carry = (jnp.full((1, NSA_HPG * TQ), NEG, F32), jnp.zeros((V_ROWS, NSA_HPG * TQ), F32))
    carry = lax.fori_loop(0, n_pairs, pair, carry)
    sbuf_ref[...] = scores(2 * n_pairs + 1)
    carry = attend(2 * n_pairs, sa_ref[...], carry, True)
    _, acc = attend(2 * n_pairs + 1, sbuf_ref[...], carry, True)
    osT = acc[0:NSA_DH, :] * (1.0 / acc[NSA_DH:NSA_DH + 1, :])

    w0 = pl.multiple_of(jnp.maximum(i * TQ - WINDOW, 0), TQ)
    s = _dot(kw_ref[pl.ds(w0, WIN_KEYS), :], q_hi)
    key = w0 + lax.broadcasted_iota(jnp.int32, (WIN_KEYS, TQ), 0)
    mask_w = (key <= t) & (key > t - WINDOW)
    p_heads = []
    for h in heads:
        sm = jnp.where(mask_w, lanes(s, h), NEG)
        p_heads.append(jnp.exp2(sm - colmax(sm)).astype(BF16))
    accw = _dot(vwT_ref[:, pl.ds(w0, WIN_KEYS)], jnp.concatenate(p_heads, axis=1))
    owT = accw[0:NSA_DH, :] * (1.0 / accw[NSA_DH:NSA_DH + 1, :])

    gT = gT_ref[...]
    outs = []
    for h in heads:
        g_c, g_s, g_w = (gT[3 * h + b:3 * h + b + 1, :] for b in range(3))
        outs.append(g_c * lanes(ocT, h)[0:NSA_DH, :] + g_s * lanes(osT, h) + g_w * lanes(owT, h))
    o_ref[...] = jnp.concatenate(outs, axis=0).T


def _nsa(qT, gT, kc_hi, kc_lo, vcT, ovT, ks, vsT, kw, vwT):
    G, S = ks.shape[0], ks.shape[1]
    NC, NB = kc_hi.shape[1], ovT.shape[0]
    per_g = lambda a: pl.BlockSpec((None,) + a.shape[1:], lambda g, i: (g,) + (0,) * (a.ndim - 1))
    return pl.pallas_call(
        _nsa_kernel,
        grid=(G, S // TQ),
        in_specs=[
            pl.BlockSpec((NSA_HPG * NSA_DH, TQ), lambda g, i: (g, i)),
            pl.BlockSpec((None, 16, TQ), lambda g, i: (g, 0, i)),
            per_g(kc_hi), per_g(kc_lo), per_g(vcT),
            pl.BlockSpec(ovT.shape, lambda g, i: (0, 0)),
            per_g(ks), per_g(vsT), per_g(kw), per_g(vwT),
        ],
        out_specs=pl.BlockSpec((TQ, NSA_HPG * NSA_DH), lambda g, i: (i, g)),
        out_shape=jax.ShapeDtypeStruct((S, NSA_HEADS * NSA_DH), F32),
        scratch_shapes=[pltpu.VMEM((NB, TQ), F32),
                        pltpu.VMEM((TK, NSA_HPG * TQ), F32),
                        pltpu.VMEM((TK, NSA_HPG * TQ), F32)],
        compiler_params=_params(("arbitrary", "arbitrary"), V7X_VMEM_LIMIT_BYTES),
        name="nsa",
    )(qT, gT, kc_hi, kc_lo, vcT, ovT, ks, vsT, kw, vwT)


def _ret_kernel(q_ref, k_ref, v_ref, o_ref, r_ref, dm_ref, qd_ref, kd_ref):
    C = RET_CHUNK
    log_g = [math.log(1.0 - 2.0 ** (-5.0 - h)) for h in range(RET_HEADS)]

    @pl.when(pl.program_id(0) == 0)
    def _():
        r_ref[...] = jnp.zeros(r_ref.shape, F32)
        diff = (lax.broadcasted_iota(jnp.int32, (C, C), 0)
                - lax.broadcasted_iota(jnp.int32, (C, C), 1)).astype(F32)
        n = lax.broadcasted_iota(jnp.int32, (C, RET_DK), 0).astype(F32)
        for h in range(RET_HEADS):
            dm_ref[h] = jnp.where(diff >= 0.0, jnp.exp(jnp.maximum(diff, 0.0) * log_g[h]), 0.0)
            qd_ref[h] = jnp.exp((n + 1.0) * log_g[h])
            kd_ref[h] = jnp.exp((C - 1.0 - n) * log_g[h])

    for h in range(RET_HEADS):
        q = q_ref[:, h * RET_DK:(h + 1) * RET_DK]
        k = k_ref[:, h * RET_DK:(h + 1) * RET_DK]
        v = v_ref[:, h * RET_DV:(h + 1) * RET_DV]
        att = lax.dot_general(q, k, (((1,), (1,)), ((), ())), preferred_element_type=F32)
        o = _dot((att * dm_ref[h]).astype(BF16), v)
        r = r_ref[h]
        r_hi, r_lo = _split(r)
        qd = (q.astype(F32) * qd_ref[h]).astype(BF16)
        o = o + _dot(qd, r_hi) + _dot(qd, r_lo)
        kd = (k.astype(F32) * kd_ref[h]).astype(BF16)
        r_ref[h] = math.exp(C * log_g[h]) * r + lax.dot_general(
            kd, v, (((0,), (0,)), ((), ())), preferred_element_type=F32)
        mu = jnp.mean(o, axis=-1, keepdims=True)
        d = o - mu
        var = jnp.mean(d * d, axis=-1, keepdims=True)
        o_ref[:, h * RET_DV:(h + 1) * RET_DV] = d * lax.rsqrt(var + GN_EPS)


def _retention(qr, kr, vr):
    S = qr.shape[0]
    C = RET_CHUNK
    row = lambda w: pl.BlockSpec((C, w), lambda n: (n, 0))
    return pl.pallas_call(
        _ret_kernel,
        grid=(S // C,),
        in_specs=[row(RET_HEADS * RET_DK), row(RET_HEADS * RET_DK), row(RET_HEADS * RET_DV)],
        out_specs=row(RET_HEADS * RET_DV),
        out_shape=jax.ShapeDtypeStruct((S, RET_HEADS * RET_DV), F32),
        scratch_shapes=[
            pltpu.VMEM((RET_HEADS, RET_DK, RET_DV), F32),
            pltpu.VMEM((RET_HEADS, C, C), F32),
            pltpu.VMEM((RET_HEADS, C, RET_DK), F32),
            pltpu.VMEM((RET_HEADS, C, RET_DK), F32),
        ],
        compiler_params=_params(("arbitrary",), V7X_VMEM_LIMIT_BYTES),
        name="retention",
    )(qr, kr, vr)


def _post_kernel(x_ref, oa_ref, zn_ref, ob_ref, zr_ref, ma_ref, mb_ref,
                 wa_ref, wb_ref, wo_ref, g_ref, out_ref):
    ya = _dot((oa_ref[...] * zn_ref[...]).astype(BF16), wa_ref[...])
    yb = _dot((ob_ref[...] * zr_ref[...]).astype(BF16), wb_ref[...])
    merged = ma_ref[...] * ya + mb_ref[...] * yb
    y = _dot(merged.astype(BF16), wo_ref[...])
    ms = jnp.mean(y * y, axis=-1, keepdims=True)
    out_ref[...] = x_ref[...] + y * lax.rsqrt(ms + RMS_EPS) * g_ref[...]


def _post(x2, oa, zn, ob, zr, ma, mb, wa, wb, wo, g_post):
    S = x2.shape[0]
    tm = ROW_TILE
    row = lambda a: pl.BlockSpec((tm, a.shape[1]), lambda i: (i, 0))
    whole = lambda a: pl.BlockSpec(a.shape, lambda i: (0, 0))
    rows = (x2, oa, zn, ob, zr, ma, mb)
    consts = (wa, wb, wo, g_post)
    return pl.pallas_call(
        _post_kernel,
        grid=(S // tm,),
        in_specs=[row(a) for a in rows] + [whole(a) for a in consts],
        out_specs=row(x2),
        out_shape=jax.ShapeDtypeStruct(x2.shape, x2.dtype),
        compiler_params=_params(("arbitrary",), V7X_VMEM_LIMIT_BYTES),
        name="post",
    )(*rows, *consts)


def _rotary_tables(S):
    pos = jnp.arange(S, dtype=F32)
    nsa_inv = 1.0 / (ROPE_THETA ** (jnp.arange(0, ROPE_DIM, 2, dtype=F32) / ROPE_DIM))
    ang = pos[:, None] * nsa_inv[None, :]
    c, s = jnp.cos(ang), jnp.sin(ang)
    half = ROPE_DIM // 2
    rest = NSA_DH - ROPE_DIM
    one, zero, zh = jnp.ones((S, rest), F32), jnp.zeros((S, rest), F32), jnp.zeros((S, half), F32)
    per_head = lambda parts: jnp.tile(jnp.concatenate(parts, axis=1), (1, LANE // NSA_DH))
    cn = per_head([c, c, one])
    san = per_head([zh, s, zero])
    sbn = per_head([-s, zh, zero])
    ret_inv = 1.0 / (RET_ROPE_BASE ** jnp.linspace(0.0, 1.0, RET_DK // 2, dtype=F32))
    rang = pos[:, None] * ret_inv[None, :]
    rc, rs = jnp.cos(rang), jnp.sin(rang)
    cr = jnp.concatenate([rc, rc], axis=1)
    sr = jnp.concatenate([-rs, rs], axis=1)
    return cn, san, sbn, cr, sr


def _value_rows(vT):
    G, d, n = vT.shape
    return jnp.concatenate(
        [vT, jnp.ones((G, 1, n), vT.dtype), jnp.zeros((G, V_ROWS - d - 1, n), vT.dtype)],
        axis=1).astype(BF16)


def _per_group(a):
    S = a.shape[0]
    return a.reshape(S, NSA_GROUPS, NSA_DH).transpose(1, 0, 2)


def kernel(x, norm_pre, w_in, b_nsa_gate, cmp_pe_k, cmp_w1_k, cmp_w2_k, cmp_pe_v, cmp_w1_v,
           cmp_w2_v, w_nsa_o, w_ret_o, w_out, norm_post):
    B, S, _ = x.shape
    assert B == 1 and S % (2 * TK) == 0 and S >= WIN_KEYS
    x2 = x.reshape(S, D_MODEL)

    segs, off = [], 0
    for sz, pad in zip(PROJ_SIZES, _PAD_SIZES):
        segs.append(jnp.pad(w_in[:, off:off + sz], ((0, 0), (0, pad - sz))))
        off += sz
    w_pad = jnp.concatenate(segs, axis=1)
    w_hi = w_pad.astype(BF16)
    w_lo = (w_pad[:, :HI_COLS] - w_hi[:, :HI_COLS].astype(F32)).astype(BF16)
    bias = jnp.pad(b_nsa_gate, (0, LANE - b_nsa_gate.shape[0])).reshape(1, LANE)

    qn, kv, gate, zn, qr, kr, vr, zr, ma, mb = _inproj(
        x2, norm_pre.reshape(1, D_MODEL), w_hi, w_lo, bias, _rotary_tables(S))

    NC = S // CMP_STRIDE
    NB = S // SEL_BLOCK
    def chunks(a):
        c = a.reshape(NC, CMP_STRIDE, NSA_GROUPS, NSA_DH).transpose(2, 0, 1, 3)
        c = c.reshape(NSA_GROUPS, NC, CMP_STRIDE * NSA_DH)
        return jnp.concatenate([c, jnp.roll(c, -1, axis=1)], axis=-1)
    flat = jnp.stack([chunks(kv[:, 0:LANE]), chunks(kv[:, LANE:2 * LANE])])
    pe_flat = jnp.stack([cmp_pe_k.reshape(1, -1), cmp_pe_v.reshape(1, -1)])
    cmp = _compress(flat, pe_flat, jnp.stack([cmp_w1_k, cmp_w1_v]), jnp.stack([cmp_w2_k, cmp_w2_v]))
    kc_hi = cmp[0].astype(BF16)
    kc_lo = (cmp[0] - kc_hi.astype(F32)).astype(BF16)
    vcT = _value_rows(cmp[1].transpose(0, 2, 1))

    c0 = jnp.arange(NC)[None, :] * CMP_STRIDE
    n0 = jnp.arange(NB)[:, None] * SEL_BLOCK
    n_cmp = (S - CMP_LEN) // CMP_STRIDE + 1
    ovT = ((c0 < n0 + SEL_BLOCK) & (c0 + CMP_LEN > n0) & (jnp.arange(NC)[None, :] < n_cmp)).astype(BF16)

    blk_in_tile = (jnp.arange(S) % TK) // SEL_BLOCK
    onehot = (blk_in_tile[:, None] == jnp.arange(K_LANES - NSA_DH)[None, :]).astype(BF16)
    ks = jnp.concatenate([_per_group(kv[:, 2 * LANE:3 * LANE]).astype(BF16),
                          jnp.broadcast_to(onehot, (NSA_GROUPS,) + onehot.shape)], axis=-1)
    vsT = _value_rows(_per_group(kv[:, 3 * LANE:4 * LANE]).transpose(0, 2, 1))
    kw = _per_group(kv[:, 4 * LANE:5 * LANE]).astype(BF16)
    vwT = _value_rows(_per_group(kv[:, 5 * LANE:6 * LANE]).transpose(0, 2, 1))
    gT = gate[:, :NSA_HEADS * 3].reshape(S, NSA_GROUPS, NSA_HPG * 3).transpose(1, 2, 0)
    gT = jnp.pad(gT, ((0, 0), (0, 16 - NSA_HPG * 3), (0, 0)))

    oa = _nsa(qn.T, gT, kc_hi, kc_lo, vcT, ovT, ks, vsT, kw, vwT)
    ob = _retention(qr, kr, vr)

    out = _post(x2, oa, zn, ob, zr, ma, mb, w_nsa_o.astype(BF16), w_ret_o.astype(BF16),
                w_out.astype(BF16), norm_post.reshape(1, D_MODEL))
    return out.reshape(B, S, D_MODEL)
```

```python
import functools
import math

import jax
import jax.numpy as jnp
from jax import lax
from jax.experimental import pallas as pl
from jax.experimental.pallas import tpu as pltpu

F32 = jnp.float32
BF16 = jnp.bfloat16

D_MODEL = 1024
NSA_HEADS = 8
NSA_GROUPS = 2
NSA_HPG = NSA_HEADS // NSA_GROUPS
NSA_DH = 64
CMP_LEN = 32
CMP_STRIDE = 16
CMP_HIDDEN = 256
SEL_BLOCK = 64
SEL_TOPN = 16
WINDOW = 512
ROPE_THETA = 500000.0
ROPE_DIM = NSA_DH // 4
NSA_SCALE = NSA_DH ** -0.5
RET_HEADS = 4
RET_DK = 128
RET_DV = 256
RET_ROPE_BASE = 10000.0
RET_SCALE = RET_DK ** -0.5
RMS_EPS = 1e-6
GN_EPS = 1e-6
PROJ_SIZES = (512, 128, 128, 128, 128, 128, 128, 24, 512, 512, 512, 1024, 1024, 1024, 1024)

LANE = 128
V7X_VMEM_LIMIT_BYTES = 56 * 1024 * 1024

ROW_TILE = 256
TQ = 128
TK = 512
BLK_PER_TILE = TK // SEL_BLOCK
WIN_KEYS = WINDOW + TQ
RET_CHUNK = 256
V_ROWS = 80
NEG = -1e30
LOG2E = math.log2(math.e)
K_LANES = LANE
SC_PAD = 8
N_SIZE_CLASSES = 4

_PAD_SIZES = tuple(128 if s == 24 else s for s in PROJ_SIZES)
_OFF = [0]
for _s in _PAD_SIZES:
    _OFF.append(_OFF[-1] + _s)
PROJ_PAD = _OFF[-1]
HI_COLS = _OFF[2]


def _dot(a, b):
    return jnp.dot(a, b, preferred_element_type=F32)


def _split(a):
    hi = a.astype(BF16)
    lo = (a - hi.astype(F32)).astype(BF16)
    return hi, lo


def _dot3(a_hi, a_lo, b_hi, b_lo):
    return _dot(a_hi, b_hi) + _dot(a_lo, b_hi) + _dot(a_hi, b_lo)


def _params(sem, vmem_bytes):
    return pltpu.CompilerParams(dimension_semantics=sem, vmem_limit_bytes=vmem_bytes)


def _inproj_kernel(x_ref, g_ref, whi_ref, wlo_ref, bias_ref, cn_ref, san_ref, sbn_ref,
                   cr_ref, sr_ref,
                   qn_ref, kv_ref, gate_ref, zn_ref, qr_ref, kr_ref, vr_ref, zr_ref,
                   ma_ref, mb_ref):
    x = x_ref[...]
    ms = jnp.mean(x * x, axis=-1, keepdims=True)
    h = x * lax.rsqrt(ms + RMS_EPS) * g_ref[...]
    h_hi, h_lo = _split(h)

    def mm(lo, hi):
        return _dot(h_hi, whi_ref[:, lo:hi])

    cn, san, sbn = cn_ref[...], san_ref[...], sbn_ref[...]
    cr, sr = cr_ref[...], sr_ref[...]

    def rot_nsa(p):
        return p * cn + pltpu.roll(p, 8, 1) * san + pltpu.roll(p, LANE - 8, 1) * sbn

    def rot_ret(p):
        return p * cr + pltpu.roll(p, LANE // 2, 1) * sr

    p = (mm(0, HI_COLS) + _dot(h_lo, whi_ref[:, 0:HI_COLS])
         + _dot(h_hi, wlo_ref[...]))
    for c in range(4):
        qn_ref[:, c * LANE:(c + 1) * LANE] = rot_nsa(p[:, c * LANE:(c + 1) * LANE]) * NSA_SCALE
    kv_ref[:, 0:LANE] = rot_nsa(p[:, 4 * LANE:5 * LANE])

    p = mm(_OFF[2], _OFF[8])
    kv_ref[:, 1 * LANE:2 * LANE] = p[:, 0 * LANE:1 * LANE]
    kv_ref[:, 2 * LANE:3 * LANE] = rot_nsa(p[:, 1 * LANE:2 * LANE])
    kv_ref[:, 3 * LANE:4 * LANE] = p[:, 2 * LANE:3 * LANE]
    kv_ref[:, 4 * LANE:5 * LANE] = rot_nsa(p[:, 3 * LANE:4 * LANE])
    kv_ref[:, 5 * LANE:6 * LANE] = p[:, 4 * LANE:5 * LANE]
    gate_ref[...] = jax.nn.sigmoid(p[:, 5 * LANE:6 * LANE] + bias_ref[...])

    z = mm(_OFF[8], _OFF[9])
    zn_ref[...] = z * jax.nn.sigmoid(z)

    p = mm(_OFF[9], _OFF[11])
    for c in range(4):
        qr_ref[:, c * LANE:(c + 1) * LANE] = rot_ret(p[:, c * LANE:(c + 1) * LANE]).astype(BF16)
    for c in range(4):
        kr_ref[:, c * LANE:(c + 1) * LANE] = (
            rot_ret(p[:, (4 + c) * LANE:(5 + c) * LANE]) * RET_SCALE).astype(BF16)

    vr_ref[...] = mm(_OFF[11], _OFF[12]).astype(BF16)
    z = mm(_OFF[12], _OFF[13])
    zr_ref[...] = z * jax.nn.sigmoid(z)
    ma_ref[...] = jax.nn.sigmoid(mm(_OFF[13], _OFF[14]))
    mb_ref[...] = jax.nn.sigmoid(mm(_OFF[14], _OFF[15]))


def _inproj(x2, norm_pre, w_hi, w_lo, bias, tabs):
    S = x2.shape[0]
    tm = ROW_TILE
    row = lambda w: pl.BlockSpec((tm, w), lambda i: (i, 0))
    whole = lambda a: pl.BlockSpec(a.shape, lambda i: (0,) * a.ndim)
    widths = (512, 768, 128, 512, 512, 512, 1024, 1024, 1024, 1024)
    dtypes = (F32, F32, F32, F32, BF16, BF16, BF16, F32, F32, F32)
    return pl.pallas_call(
        _inproj_kernel,
        grid=(S // tm,),
        in_specs=[row(D_MODEL), whole(norm_pre), whole(w_hi), whole(w_lo), whole(bias)]
        + [row(LANE)] * 5,
        out_specs=[row(w) for w in widths],
        out_shape=[jax.ShapeDtypeStruct((S, w), d) for w, d in zip(widths, dtypes)],
        compiler_params=_params(("arbitrary",), V7X_VMEM_LIMIT_BYTES),
        name="inproj",
    )(x2, norm_pre, w_hi, w_lo, bias, *tabs)


def _compress_kernel(flat_ref, pe_ref, w1_ref, w2_ref, o_ref):
    f_hi, f_lo = _split(flat_ref[...] + pe_ref[...])
    w1_hi, w1_lo = _split(w1_ref[...])
    hid = _dot3(f_hi, f_lo, w1_hi, w1_lo)
    act = hid * (0.5 * (1.0 + jnp.tanh(math.sqrt(2.0 / math.pi) * (hid + 0.044715 * (hid * hid * hid)))))
    a_hi, a_lo = _split(act)
    w2_hi, w2_lo = _split(w2_ref[...])
    o_ref[...] = _dot3(a_hi, a_lo, w2_hi, w2_lo)


def _compress(flat, pe_flat, w1, w2):
    _, G, NC, W = flat.shape
    return pl.pallas_call(
        _compress_kernel,
        grid=(2, G),
        in_specs=[
            pl.BlockSpec((None, None, NC, W), lambda k, g: (k, g, 0, 0)),
            pl.BlockSpec((None, 1, W), lambda k, g: (k, 0, 0)),
            pl.BlockSpec((None, W, CMP_HIDDEN), lambda k, g: (k, 0, 0)),
            pl.BlockSpec((None, CMP_HIDDEN, NSA_DH), lambda k, g: (k, 0, 0)),
        ],
        out_specs=pl.BlockSpec((None, None, NC, NSA_DH), lambda k, g: (k, g, 0, 0)),
        out_shape=jax.ShapeDtypeStruct((2, G, NC, NSA_DH), F32),
        compiler_params=_params(("arbitrary", "arbitrary"), V7X_VMEM_LIMIT_BYTES),
        name="compress",
    )(flat, pe_flat, w1, w2)


def _nsa_kernel(qT_ref, gT_ref, kc4_ref, vcT_ref, ks_ref, vsT_ref, kw_ref, vwT_ref, o_ref,
                sb_ref, sa_ref, sbuf_ref, sc_ref, oc_ref, m_ref, acc_ref):
    i = pl.program_id(1)
    NC = kc4_ref.shape[0]
    NB = sb_ref.shape[0]
    n_q = pl.num_programs(1)
    heads = range(NSA_HPG)
    lanes = lambda a, h: a[:, h * TQ:(h + 1) * TQ]
    colmax = lambda a: jnp.max(a, axis=0, keepdims=True)
    colsum = lambda a: jnp.sum(a, axis=0, keepdims=True)

    qT = qT_ref[...] * LOG2E
    Q = jnp.concatenate([qT[h * NSA_DH:(h + 1) * NSA_DH, :] for h in heads], axis=1)
    q_hi, q_lo = _split(Q)
    q4 = jnp.concatenate([q_hi, q_lo, q_hi, q_lo], axis=0)
    t = i * TQ + lax.broadcasted_iota(jnp.int32, (1, TQ), 1)
    cur = jnp.right_shift(t, SEL_BLOCK.bit_length() - 1)

    def compressed_and_select(ncp, nbp):
        s = _dot(kc4_ref[0:ncp, :], q4)
        c_end = lax.broadcasted_iota(jnp.int32, (ncp, TQ), 0) * CMP_STRIDE + (CMP_LEN - 1)
        mask_c = c_end <= t
        p_heads, inv = [], []
        for h in heads:
            sm = jnp.where(mask_c, lanes(s, h), -jnp.inf)
            m = colmax(sm)
            m = jnp.where(m == -jnp.inf, 0.0, m)
            p = jnp.exp2(sm - m)
            inv.append(1.0 / jnp.maximum(colsum(p), jnp.finfo(F32).tiny))
            sc_ref[h, 0:SC_PAD, :] = jnp.zeros((SC_PAD, TQ), F32)
            sc_ref[h, SC_PAD:SC_PAD + ncp, :] = p
            p_heads.append(p.astype(BF16))
        oc_ref[...] = (_dot(vcT_ref[:, 0:ncp], jnp.concatenate(p_heads, axis=1))
                       * jnp.concatenate(inv, axis=1))
        per_sel = SEL_BLOCK // CMP_STRIDE
        imp = jnp.zeros((nbp, TQ), F32)
        for h in heads:
            tot = sc_ref[h, pl.ds(SC_PAD - 1, nbp, stride=per_sel), :]
            for r in range(per_sel):
                tot = tot + sc_ref[h, pl.ds(SC_PAD + r, nbp, stride=per_sel), :]
            imp = imp + tot * inv[h]
        blk = lax.broadcasted_iota(jnp.int32, (nbp, TQ), 0)
        blk_f = blk.astype(F32)
        valid = blk <= cur
        forced = (blk == 0) | (blk == cur) | (blk == cur - 1)
        free = valid & jnp.logical_not(forced)
        score = jnp.where(free, imp, -jnp.inf)
        for _ in range(min(SEL_TOPN, NB) - 3):
            mx = colmax(score)
            idx = jnp.min(jnp.where(score == mx, blk_f, float(NB)), axis=0, keepdims=True)
            score = jnp.where(blk_f == idx, -jnp.inf, score)
        picked = valid & (forced | (score == -jnp.inf))
        sb_ref[0:nbp, :] = jnp.where(picked, 0.0, NEG)
        if nbp < NB:
            sb_ref[nbp:NB, :] = jnp.full((NB - nbp, TQ), NEG, F32)

    size_class = lax.div(i * N_SIZE_CLASSES, n_q)
    for k in range(N_SIZE_CLASSES):
        pl.when(size_class == k)(functools.partial(
            compressed_and_select, NC * (k + 1) // N_SIZE_CLASSES, NB * (k + 1) // N_SIZE_CLASSES))

    pad_rows = jnp.zeros((K_LANES - NSA_DH - 2 * BLK_PER_TILE, NSA_HPG * TQ), BF16)

    def scores(j):
        k0 = pl.multiple_of(j * TK, TK)
        sbt = sb_ref[pl.ds(pl.multiple_of(j * BLK_PER_TILE, BLK_PER_TILE), BLK_PER_TILE), :]
        rows = jnp.concatenate([sbt, jnp.zeros_like(sbt)], axis=0)
        rows = jnp.concatenate([rows] * NSA_HPG, axis=1).astype(BF16)
        w = jnp.concatenate([q_hi, rows, pad_rows], axis=0)
        return _dot(ks_ref[pl.ds(k0, TK), :], w)

    def attend(j, s, carry, causal):
        m, acc = carry
        k0 = pl.multiple_of(j * TK, TK)
        if causal:
            keep = k0 + lax.broadcasted_iota(jnp.int32, (TK, TQ), 0) <= t
            s = jnp.concatenate([jnp.where(keep, lanes(s, h), NEG) for h in heads], axis=1)
        m_new = jnp.maximum(m, colmax(s))
        alpha = jnp.exp2(m - m_new)
        p = jnp.exp2(s - m_new).astype(BF16)
        acc = acc * alpha + _dot(vsT_ref[:, pl.ds(k0, TK)], p)
        return m_new, acc

    jd = lax.div(i, TK // TQ)
    n_pairs = lax.div(jd, 2)
    sa_ref[...] = scores(0)

    def pair(pp, carry):
        sbuf_ref[...] = scores(2 * pp + 1)
        carry = attend(2 * pp, sa_ref[...], carry, False)
        sa_ref[...] = scores(2 * pp + 2)
        return attend(2 * pp + 1, sbuf_ref[...], carry, False)

    carry = (jnp.full((1, NSA_HPG * TQ), NEG, F32), jnp.zeros((V_ROWS, NSA_HPG * TQ), F32))
    carry = lax.fori_loop(0, n_pairs, pair, carry)
    sbuf_ref[...] = scores(2 * n_pairs + 1)
    m_ref[...], acc_ref[...] = attend(2 * n_pairs, sa_ref[...], carry, True)

    @pl.when(jd != 2 * n_pairs)
    def _():
        m_ref[...], acc_ref[...] = attend(
            2 * n_pairs + 1, sbuf_ref[...], (m_ref[...], acc_ref[...]), True)

    acc = acc_ref[...]
    osT = acc[0:NSA_DH, :] * (1.0 / acc[NSA_DH:NSA_DH + 1, :])

    w0 = pl.multiple_of(jnp.maximum(i * TQ - WINDOW, 0), TQ)
    s = _dot(kw_ref[pl.ds(w0, WIN_KEYS), :], q_hi)
    key = w0 + lax.broadcasted_iota(jnp.int32, (WIN_KEYS, TQ), 0)
    mask_w = (key <= t) & (key > t - WINDOW)
    p_heads = []
    for h in heads:
        sm = jnp.where(mask_w, lanes(s, h), NEG)
        p_heads.append(jnp.exp2(sm - colmax(sm)).astype(BF16))
    accw = _dot(vwT_ref[:, pl.ds(w0, WIN_KEYS)], jnp.concatenate(p_heads, axis=1))
    owT = accw[0:NSA_DH, :] * (1.0 / accw[NSA_DH:NSA_DH + 1, :])

    gT = gT_ref[...]
    ocT = oc_ref[...]
    outs = []
    for h in heads:
        g_c, g_s, g_w = (gT[3 * h + b:3 * h + b + 1, :] for b in range(3))
        outs.append(g_c * lanes(ocT, h)[0:NSA_DH, :] + g_s * lanes(osT, h) + g_w * lanes(owT, h))
    o_ref[...] = jnp.concatenate(outs, axis=0).T


def _nsa(qT, gT, kc4, vcT, ks, vsT, kw, vwT):
    G, S = ks.shape[0], ks.shape[1]
    NC, NB = kc4.shape[1], S // SEL_BLOCK
    per_g = lambda a: pl.BlockSpec((None,) + a.shape[1:], lambda g, i: (g,) + (0,) * (a.ndim - 1))
    wide = NSA_HPG * TQ
    return pl.pallas_call(
        _nsa_kernel,
        grid=(G, S // TQ),
        in_specs=[
            pl.BlockSpec((NSA_HPG * NSA_DH, TQ), lambda g, i: (g, i)),
            pl.BlockSpec((None, 16, TQ), lambda g, i: (g, 0, i)),
            per_g(kc4), per_g(vcT), per_g(ks), per_g(vsT), per_g(kw), per_g(vwT),
        ],
        out_specs=pl.BlockSpec((TQ, NSA_HPG * NSA_DH), lambda g, i: (i, g)),
        out_shape=jax.ShapeDtypeStruct((S, NSA_HEADS * NSA_DH), F32),
        scratch_shapes=[pltpu.VMEM((NB, TQ), F32),
                        pltpu.VMEM((TK, wide), F32),
                        pltpu.VMEM((TK, wide), F32),
                        pltpu.VMEM((NSA_HPG, SC_PAD + NC, TQ), F32),
                        pltpu.VMEM((V_ROWS, wide), F32),
                        pltpu.VMEM((1, wide), F32),
                        pltpu.VMEM((V_ROWS, wide), F32)],
        compiler_params=_params(("arbitrary", "arbitrary"), V7X_VMEM_LIMIT_BYTES),
        name="nsa",
    )(qT, gT, kc4, vcT, ks, vsT, kw, vwT)


def _ret_kernel(q_ref, k_ref, v_ref, o_ref, r_ref, dm_ref, qd_ref, kd_ref):
    C = RET_CHUNK
    log_g = [math.log(1.0 - 2.0 ** (-5.0 - h)) for h in range(RET_HEADS)]

    @pl.when(pl.program_id(0) == 0)
    def _():
        r_ref[...] = jnp.zeros(r_ref.shape, F32)
        diff = (lax.broadcasted_iota(jnp.int32, (C, C), 0)
                - lax.broadcasted_iota(jnp.int32, (C, C), 1)).astype(F32)
        n = lax.broadcasted_iota(jnp.int32, (C, RET_DK), 0).astype(F32)
        for h in range(RET_HEADS):
            dm_ref[h] = jnp.where(diff >= 0.0, jnp.exp(jnp.maximum(diff, 0.0) * log_g[h]), 0.0)
            qd_ref[h] = jnp.exp((n + 1.0) * log_g[h])
            kd_ref[h] = jnp.exp((C - 1.0 - n) * log_g[h])

    for h in range(RET_HEADS):
        q = q_ref[:, h * RET_DK:(h + 1) * RET_DK]
        k = k_ref[:, h * RET_DK:(h + 1) * RET_DK]
        v = v_ref[:, h * RET_DV:(h + 1) * RET_DV]
        att = lax.dot_general(q, k, (((1,), (1,)), ((), ())), preferred_element_type=F32)
        o = _dot((att * dm_ref[h]).astype(BF16), v)
        r = r_ref[h]
        r_hi, r_lo = _split(r)
        qd = (q.astype(F32) * qd_ref[h]).astype(BF16)
        o = o + _dot(qd, r_hi) + _dot(qd, r_lo)
        kd = (k.astype(F32) * kd_ref[h]).astype(BF16)
        r_ref[h] = math.exp(C * log_g[h]) * r + lax.dot_general(
            kd, v, (((0,), (0,)), ((), ())), preferred_element_type=F32)
        mu = jnp.mean(o, axis=-1, keepdims=True)
        d = o - mu
        var = jnp.mean(d * d, axis=-1, keepdims=True)
        o_ref[:, h * RET_DV:(h + 1) * RET_DV] = d * lax.rsqrt(var + GN_EPS)


def _retention(qr, kr, vr):
    S = qr.shape[0]
    C = RET_CHUNK
    row = lambda w: pl.BlockSpec((C, w), lambda n: (n, 0))
    return pl.pallas_call(
        _ret_kernel,
        grid=(S // C,),
        in_specs=[row(RET_HEADS * RET_DK), row(RET_HEADS * RET_DK), row(RET_HEADS * RET_DV)],
        out_specs=row(RET_HEADS * RET_DV),
        out_shape=jax.ShapeDtypeStruct((S, RET_HEADS * RET_DV), F32),
        scratch_shapes=[
            pltpu.VMEM((RET_HEADS, RET_DK, RET_DV), F32),
            pltpu.VMEM((RET_HEADS, C, C), F32),
            pltpu.VMEM((RET_HEADS, C, RET_DK), F32),
            pltpu.VMEM((RET_HEADS, C, RET_DK), F32),
        ],
        compiler_params=_params(("arbitrary",), V7X_VMEM_LIMIT_BYTES),
        name="retention",
    )(qr, kr, vr)


def _post_kernel(x_ref, oa_ref, zn_ref, ob_ref, zr_ref, ma_ref, mb_ref,
                 wa_ref, wb_ref, wo_ref, g_ref, out_ref):
    ya = _dot((oa_ref[...] * zn_ref[...]).astype(BF16), wa_ref[...])
    yb = _dot((ob_ref[...] * zr_ref[...]).astype(BF16), wb_ref[...])
    merged = ma_ref[...] * ya + mb_ref[...] * yb
    y = _dot(merged.astype(BF16), wo_ref[...])
    ms = jnp.mean(y * y, axis=-1, keepdims=True)
    out_ref[...] = x_ref[...] + y * lax.rsqrt(ms + RMS_EPS) * g_ref[...]


def _post(x2, oa, zn, ob, zr, ma, mb, wa, wb, wo, g_post):
    S = x2.shape[0]
    tm = ROW_TILE
    row = lambda a: pl.BlockSpec((tm, a.shape[1]), lambda i: (i, 0))
    whole = lambda a: pl.BlockSpec(a.shape, lambda i: (0, 0))
    rows = (x2, oa, zn, ob, zr, ma, mb)
    consts = (wa, wb, wo, g_post)
    return pl.pallas_call(
        _post_kernel,
        grid=(S // tm,),
        in_specs=[row(a) for a in rows] + [whole(a) for a in consts],
        out_specs=row(x2),
        out_shape=jax.ShapeDtypeStruct(x2.shape, x2.dtype),
        compiler_params=_params(("arbitrary",), V7X_VMEM_LIMIT_BYTES),
        name="post",
    )(*rows, *consts)


def _rotary_tables(S):
    pos = jnp.arange(S, dtype=F32)
    nsa_inv = 1.0 / (ROPE_THETA ** (jnp.arange(0, ROPE_DIM, 2, dtype=F32) / ROPE_DIM))
    ang = pos[:, None] * nsa_inv[None, :]
    c, s = jnp.cos(ang), jnp.sin(ang)
    half = ROPE_DIM // 2
    rest = NSA_DH - ROPE_DIM
    one, zero, zh = jnp.ones((S, rest), F32), jnp.zeros((S, rest), F32), jnp.zeros((S, half), F32)
    per_head = lambda parts: jnp.tile(jnp.concatenate(parts, axis=1), (1, LANE // NSA_DH))
    cn = per_head([c, c, one])
    san = per_head([zh, s, zero])
    sbn = per_head([-s, zh, zero])
    ret_inv = 1.0 / (RET_ROPE_BASE ** jnp.linspace(0.0, 1.0, RET_DK // 2, dtype=F32))
    rang = pos[:, None] * ret_inv[None, :]
    rc, rs = jnp.cos(rang), jnp.sin(rang)
    cr = jnp.concatenate([rc, rc], axis=1)
    sr = jnp.concatenate([-rs, rs], axis=1)
    return cn, san, sbn, cr, sr


def _value_rows(vT):
    G, d, n = vT.shape
    return jnp.concatenate(
        [vT, jnp.ones((G, 1, n), vT.dtype), jnp.zeros((G, V_ROWS - d - 1, n), vT.dtype)],
        axis=1).astype(BF16)


def _per_group(a):
    S = a.shape[0]
    return a.reshape(S, NSA_GROUPS, NSA_DH).transpose(1, 0, 2)


def kernel(x, norm_pre, w_in, b_nsa_gate, cmp_pe_k, cmp_w1_k, cmp_w2_k, cmp_pe_v, cmp_w1_v,
           cmp_w2_v, w_nsa_o, w_ret_o, w_out, norm_post):
    B, S, _ = x.shape
    assert B == 1 and S % (2 * TK) == 0 and S >= WIN_KEYS
    assert S % (SEL_BLOCK * 8 * N_SIZE_CLASSES) == 0 and (S // TQ) % N_SIZE_CLASSES == 0
    x2 = x.reshape(S, D_MODEL)

    segs, off = [], 0
    for sz, pad in zip(PROJ_SIZES, _PAD_SIZES):
        segs.append(jnp.pad(w_in[:, off:off + sz], ((0, 0), (0, pad - sz))))
        off += sz
    w_pad = jnp.concatenate(segs, axis=1)
    w_hi = w_pad.astype(BF16)
    w_lo = (w_pad[:, :HI_COLS] - w_hi[:, :HI_COLS].astype(F32)).astype(BF16)
    bias = jnp.pad(b_nsa_gate, (0, LANE - b_nsa_gate.shape[0])).reshape(1, LANE)

    qn, kv, gate, zn, qr, kr, vr, zr, ma, mb = _inproj(
        x2, norm_pre.reshape(1, D_MODEL), w_hi, w_lo, bias, _rotary_tables(S))

    NC = S // CMP_STRIDE
    def chunks(a):
        c = a.reshape(NC, CMP_STRIDE, NSA_GROUPS, NSA_DH).transpose(2, 0, 1, 3)
        c = c.reshape(NSA_GROUPS, NC, CMP_STRIDE * NSA_DH)
        return jnp.concatenate([c, jnp.roll(c, -1, axis=1)], axis=-1)
    flat = jnp.stack([chunks(kv[:, 0:LANE]), chunks(kv[:, LANE:2 * LANE])])
    pe_flat = jnp.stack([cmp_pe_k.reshape(1, -1), cmp_pe_v.reshape(1, -1)])
    cmp = _compress(flat, pe_flat, jnp.stack([cmp_w1_k, cmp_w1_v]), jnp.stack([cmp_w2_k, cmp_w2_v]))
    kc_hi = cmp[0].astype(BF16)
    kc_lo = (cmp[0] - kc_hi.astype(F32)).astype(BF16)
    kc4 = jnp.concatenate([kc_hi, kc_hi, kc_lo, kc_lo], axis=-1)
    vcT = _value_rows(cmp[1].transpose(0, 2, 1))

    blk_in_tile = (jnp.arange(S) % TK) // SEL_BLOCK
    onehot = (blk_in_tile[:, None] == jnp.arange(K_LANES - NSA_DH)[None, :]).astype(BF16)
    ks = jnp.concatenate([_per_group(kv[:, 2 * LANE:3 * LANE]).astype(BF16),
                          jnp.broadcast_to(onehot, (NSA_GROUPS,) + onehot.shape)], axis=-1)
    vsT = _value_rows(_per_group(kv[:, 3 * LANE:4 * LANE]).transpose(0, 2, 1))
    kw = _per_group(kv[:, 4 * LANE:5 * LANE]).astype(BF16)
    vwT = _value_rows(_per_group(kv[:, 5 * LANE:6 * LANE]).transpose(0, 2, 1))
    gT = gate[:, :NSA_HEADS * 3].reshape(S, NSA_GROUPS, NSA_HPG * 3).transpose(1, 2, 0)
    gT = jnp.pad(gT, ((0, 0), (0, 16 - NSA_HPG * 3), (0, 0)))

    oa = _nsa(qn.T, gT, kc4, vcT, ks, vsT, kw, vwT)
    ob = _retention(qr, kr, vr)

    out = _post(x2, oa, zn, ob, zr, ma, mb, w_nsa_o.astype(BF16), w_ret_o.astype(BF16),
                w_out.astype(BF16), norm_post.reshape(1, D_MODEL))
    return out.reshape(B, S, D_MODEL)
```

```python
import functools
import math

import jax
import jax.numpy as jnp
from jax import lax
from jax.experimental import pallas as pl
from jax.experimental.pallas import tpu as pltpu

F32 = jnp.float32
BF16 = jnp.bfloat16

D_MODEL = 1024
NSA_HEADS = 8
NSA_GROUPS = 2
NSA_HPG = NSA_HEADS // NSA_GROUPS
NSA_DH = 64
CMP_LEN = 32
CMP_STRIDE = 16
CMP_HIDDEN = 256
SEL_BLOCK = 64
SEL_TOPN = 16
WINDOW = 512
ROPE_THETA = 500000.0
ROPE_DIM = NSA_DH // 4
NSA_SCALE = NSA_DH ** -0.5
RET_HEADS = 4
RET_DK = 128
RET_DV = 256
RET_ROPE_BASE = 10000.0
RET_SCALE = RET_DK ** -0.5
RMS_EPS = 1e-6
GN_EPS = 1e-6
PROJ_SIZES = (512, 128, 128, 128, 128, 128, 128, 24, 512, 512, 512, 1024, 1024, 1024, 1024)

LANE = 128
V7X_VMEM_LIMIT_BYTES = 56 * 1024 * 1024

ROW_TILE = 256
TQ = 128
TK = 512
BLK_PER_TILE = TK // SEL_BLOCK
WIN_KEYS = WINDOW + TQ
RET_CHUNK = 256
V_ROWS = 80
NEG = -1e30
LOG2E = math.log2(math.e)
K_LANES = LANE
SC_PAD = 8
N_SIZE_CLASSES = 4

_PAD_SIZES = tuple(128 if s == 24 else s for s in PROJ_SIZES)
_OFF = [0]
for _s in _PAD_SIZES:
    _OFF.append(_OFF[-1] + _s)
PROJ_PAD = _OFF[-1]
HI_COLS = _OFF[2]


def _dot(a, b):
    return jnp.dot(a, b, preferred_element_type=F32)


def _split(a):
    hi = a.astype(BF16)
    lo = (a - hi.astype(F32)).astype(BF16)
    return hi, lo


def _dot3(a_hi, a_lo, b_hi, b_lo):
    return _dot(a_hi, b_hi) + _dot(a_lo, b_hi) + _dot(a_hi, b_lo)


def _params(sem, vmem_bytes):
    return pltpu.CompilerParams(dimension_semantics=sem, vmem_limit_bytes=vmem_bytes)


def _inproj_kernel(x_ref, g_ref, whi_ref, wlo_ref, bias_ref, cn_ref, san_ref, sbn_ref,
                   cr_ref, sr_ref,
                   qn_ref, kcv_ref, kv_ref, gate_ref, zn_ref, qr_ref, kr_ref, vr_ref, zr_ref,
                   ma_ref, mb_ref):
    x = x_ref[...]
    ms = jnp.mean(x * x, axis=-1, keepdims=True)
    h = x * lax.rsqrt(ms + RMS_EPS) * g_ref[...]
    h_hi, h_lo = _split(h)

    def mm(lo, hi):
        return _dot(h_hi, whi_ref[:, lo:hi])

    cn, san, sbn = cn_ref[...], san_ref[...], sbn_ref[...]
    cr, sr = cr_ref[...], sr_ref[...]

    def rot_nsa(p):
        return p * cn + pltpu.roll(p, 8, 1) * san + pltpu.roll(p, LANE - 8, 1) * sbn

    def rot_ret(p):
        return p * cr + pltpu.roll(p, LANE // 2, 1) * sr

    p = (mm(0, HI_COLS) + _dot(h_lo, whi_ref[:, 0:HI_COLS])
         + _dot(h_hi, wlo_ref[...]))
    for c in range(4):
        qn_ref[:, c * LANE:(c + 1) * LANE] = rot_nsa(p[:, c * LANE:(c + 1) * LANE]) * NSA_SCALE
    kcv_ref[:, 0:LANE] = rot_nsa(p[:, 4 * LANE:5 * LANE])

    p = mm(_OFF[2], _OFF[8])
    kcv_ref[:, LANE:2 * LANE] = p[:, 0 * LANE:1 * LANE]
    kv_ref[:, 0 * LANE:1 * LANE] = rot_nsa(p[:, 1 * LANE:2 * LANE]).astype(BF16)
    kv_ref[:, 1 * LANE:2 * LANE] = p[:, 2 * LANE:3 * LANE].astype(BF16)
    kv_ref[:, 2 * LANE:3 * LANE] = rot_nsa(p[:, 3 * LANE:4 * LANE]).astype(BF16)
    kv_ref[:, 3 * LANE:4 * LANE] = p[:, 4 * LANE:5 * LANE].astype(BF16)
    gate_ref[...] = jax.nn.sigmoid(p[:, 5 * LANE:6 * LANE] + bias_ref[...])

    z = mm(_OFF[8], _OFF[9])
    zn_ref[...] = (z * jax.nn.sigmoid(z)).astype(BF16)

    p = mm(_OFF[9], _OFF[11])
    for c in range(4):
        qr_ref[:, c * LANE:(c + 1) * LANE] = rot_ret(p[:, c * LANE:(c + 1) * LANE]).astype(BF16)
    for c in range(4):
        kr_ref[:, c * LANE:(c + 1) * LANE] = (
            rot_ret(p[:, (4 + c) * LANE:(5 + c) * LANE]) * RET_SCALE).astype(BF16)

    vr_ref[...] = mm(_OFF[11], _OFF[12]).astype(BF16)
    z = mm(_OFF[12], _OFF[13])
    zr_ref[...] = (z * jax.nn.sigmoid(z)).astype(BF16)
    ma_ref[...] = jax.nn.sigmoid(mm(_OFF[13], _OFF[14])).astype(BF16)
    mb_ref[...] = jax.nn.sigmoid(mm(_OFF[14], _OFF[15])).astype(BF16)


def _inproj(x2, norm_pre, w_hi, w_lo, bias, tabs):
    S = x2.shape[0]
    tm = ROW_TILE
    row = lambda w: pl.BlockSpec((tm, w), lambda i: (i, 0))
    whole = lambda a: pl.BlockSpec(a.shape, lambda i: (0,) * a.ndim)
    widths = (512, 256, 512, 128, 512, 512, 512, 1024, 1024, 1024, 1024)
    dtypes = (F32, F32, BF16, F32, BF16, BF16, BF16, BF16, BF16, BF16, BF16)
    return pl.pallas_call(
        _inproj_kernel,
        grid=(S // tm,),
        in_specs=[row(D_MODEL), whole(norm_pre), whole(w_hi), whole(w_lo), whole(bias)]
        + [row(LANE)] * 5,
        out_specs=[row(w) for w in widths],
        out_shape=[jax.ShapeDtypeStruct((S, w), d) for w, d in zip(widths, dtypes)],
        compiler_params=_params(("arbitrary",), V7X_VMEM_LIMIT_BYTES),
        name="inproj",
    )(x2, norm_pre, w_hi, w_lo, bias, *tabs)


def _compress_kernel(flat_ref, pe_ref, w1_ref, w2_ref, o_ref):
    f_hi, f_lo = _split(flat_ref[...] + pe_ref[...])
    w1_hi, w1_lo = _split(w1_ref[...])
    hid = _dot3(f_hi, f_lo, w1_hi, w1_lo)
    act = hid * (0.5 * (1.0 + jnp.tanh(math.sqrt(2.0 / math.pi) * (hid + 0.044715 * (hid * hid * hid)))))
    a_hi, a_lo = _split(act)
    w2_hi, w2_lo = _split(w2_ref[...])
    o_ref[...] = _dot3(a_hi, a_lo, w2_hi, w2_lo)


def _compress(flat, pe_flat, w1, w2):
    _, G, NC, W = flat.shape
    return pl.pallas_call(
        _compress_kernel,
        grid=(2, G),
        in_specs=[
            pl.BlockSpec((None, None, NC, W), lambda k, g: (k, g, 0, 0)),
            pl.BlockSpec((None, 1, W), lambda k, g: (k, 0, 0)),
            pl.BlockSpec((None, W, CMP_HIDDEN), lambda k, g: (k, 0, 0)),
            pl.BlockSpec((None, CMP_HIDDEN, NSA_DH), lambda k, g: (k, 0, 0)),
        ],
        out_specs=pl.BlockSpec((None, None, NC, NSA_DH), lambda k, g: (k, g, 0, 0)),
        out_shape=jax.ShapeDtypeStruct((2, G, NC, NSA_DH), F32),
        compiler_params=_params(("arbitrary", "arbitrary"), V7X_VMEM_LIMIT_BYTES),
        name="compress",
    )(flat, pe_flat, w1, w2)


def _nsa_kernel(qT_ref, gT_ref, kc4_ref, vcT_ref, ks_ref, vsT_ref, kw_ref, vwT_ref, o_ref,
                sb_ref, sa_ref, sbuf_ref, sc_ref, oc_ref, m_ref, acc_ref):
    i = pl.program_id(1)
    NC = kc4_ref.shape[0]
    NB = sb_ref.shape[0]
    n_q = pl.num_programs(1)
    heads = range(NSA_HPG)
    lanes = lambda a, h: a[:, h * TQ:(h + 1) * TQ]
    colmax = lambda a: jnp.max(a, axis=0, keepdims=True)
    colsum = lambda a: jnp.sum(a, axis=0, keepdims=True)

    qT = qT_ref[...] * LOG2E
    Q = jnp.concatenate([qT[h * NSA_DH:(h + 1) * NSA_DH, :] for h in heads], axis=1)
    q_hi, q_lo = _split(Q)
    q4 = jnp.concatenate([q_hi, q_lo, q_hi, q_lo], axis=0)
    t = i * TQ + lax.broadcasted_iota(jnp.int32, (1, TQ), 1)
    cur = jnp.right_shift(t, SEL_BLOCK.bit_length() - 1)

    def compressed_and_select(ncp, nbp):
        s = _dot(kc4_ref[0:ncp, :], q4)
        c_end = lax.broadcasted_iota(jnp.int32, (ncp, TQ), 0) * CMP_STRIDE + (CMP_LEN - 1)
        mask_c = c_end <= t
        p_heads, inv = [], []
        for h in heads:
            sm = jnp.where(mask_c, lanes(s, h), -jnp.inf)
            m = colmax(sm)
            m = jnp.where(m == -jnp.inf, 0.0, m)
            p = jnp.exp2(sm - m)
            inv.append(1.0 / jnp.maximum(colsum(p), jnp.finfo(F32).tiny))
            sc_ref[h, 0:SC_PAD, :] = jnp.zeros((SC_PAD, TQ), F32)
            sc_ref[h, SC_PAD:SC_PAD + ncp, :] = p
            p_heads.append(p.astype(BF16))
        oc_ref[...] = (_dot(vcT_ref[:, 0:ncp], jnp.concatenate(p_heads, axis=1))
                       * jnp.concatenate(inv, axis=1))
        per_sel = SEL_BLOCK // CMP_STRIDE
        imp = jnp.zeros((nbp, TQ), F32)
        for h in heads:
            tot = sc_ref[h, pl.ds(SC_PAD - 1, nbp, stride=per_sel), :]
            for r in range(per_sel):
                tot = tot + sc_ref[h, pl.ds(SC_PAD + r, nbp, stride=per_sel), :]
            imp = imp + tot * inv[h]
        blk = lax.broadcasted_iota(jnp.int32, (nbp, TQ), 0)
        blk_f = blk.astype(F32)
        valid = blk <= cur
        forced = (blk == 0) | (blk == cur) | (blk == cur - 1)
        free = valid & jnp.logical_not(forced)
        score = jnp.where(free, imp, -jnp.inf)
        for _ in range(min(SEL_TOPN, NB) - 3):
            mx = colmax(score)
            idx = jnp.min(jnp.where(score == mx, blk_f, float(NB)), axis=0, keepdims=True)
            score = jnp.where(blk_f == idx, -jnp.inf, score)
        picked = valid & (forced | (score == -jnp.inf))
        sb_ref[0:nbp, :] = jnp.where(picked, 0.0, NEG)
        if nbp < NB:
            sb_ref[nbp:NB, :] = jnp.full((NB - nbp, TQ), NEG, F32)

    size_class = lax.div(i * N_SIZE_CLASSES, n_q)
    for k in range(N_SIZE_CLASSES):
        pl.when(size_class == k)(functools.partial(
            compressed_and_select, NC * (k + 1) // N_SIZE_CLASSES, NB * (k + 1) // N_SIZE_CLASSES))

    pad_rows = jnp.zeros((K_LANES - NSA_DH - 2 * BLK_PER_TILE, NSA_HPG * TQ), BF16)

    def scores(j):
        k0 = pl.multiple_of(j * TK, TK)
        sbt = sb_ref[pl.ds(pl.multiple_of(j * BLK_PER_TILE, BLK_PER_TILE), BLK_PER_TILE), :]
        rows = jnp.concatenate([sbt, jnp.zeros_like(sbt)], axis=0)
        rows = jnp.concatenate([rows] * NSA_HPG, axis=1).astype(BF16)
        w = jnp.concatenate([q_hi, rows, pad_rows], axis=0)
        return _dot(ks_ref[pl.ds(k0, TK), :], w)

    def attend(j, s, carry, causal):
        m, acc = carry
        k0 = pl.multiple_of(j * TK, TK)
        if causal:
            keep = k0 + lax.broadcasted_iota(jnp.int32, (TK, TQ), 0) <= t
            s = jnp.concatenate([jnp.where(keep, lanes(s, h), NEG) for h in heads], axis=1)
        m_new = jnp.maximum(m, colmax(s))
        alpha = jnp.exp2(m - m_new)
        p = jnp.exp2(s - m_new).astype(BF16)
        acc = acc * alpha + _dot(vsT_ref[:, pl.ds(k0, TK)], p)
        return m_new, acc

    jd = lax.div(i, TK // TQ)
    n_pairs = lax.div(jd, 2)
    sa_ref[...] = scores(0)

    def pair(pp, carry):
        sbuf_ref[...] = scores(2 * pp + 1)
        carry = attend(2 * pp, sa_ref[...], carry, False)
        sa_ref[...] = scores(2 * pp + 2)
        return attend(2 * pp + 1, sbuf_ref[...], carry, False)

    carry = (jnp.full((1, NSA_HPG * TQ), NEG, F32), jnp.zeros((V_ROWS, NSA_HPG * TQ), F32))
    n_quads = lax.div(n_pairs, 2)
    carry = lax.fori_loop(0, n_quads, lambda qq, c: pair(2 * qq + 1, pair(2 * qq, c)), carry)
    carry = lax.fori_loop(2 * n_quads, n_pairs, pair, carry)
    sbuf_ref[...] = scores(2 * n_pairs + 1)
    m_ref[...], acc_ref[...] = attend(2 * n_pairs, sa_ref[...], carry, True)

    @pl.when(jd != 2 * n_pairs)
    def _():
        m_ref[...], acc_ref[...] = attend(
            2 * n_pairs + 1, sbuf_ref[...], (m_ref[...], acc_ref[...]), True)

    acc = acc_ref[...]
    osT = acc[0:NSA_DH, :] * (1.0 / acc[NSA_DH:NSA_DH + 1, :])

    w0 = pl.multiple_of(jnp.maximum(i * TQ - WINDOW, 0), TQ)
    s = _dot(kw_ref[pl.ds(w0, WIN_KEYS), :], q_hi)
    key = w0 + lax.broadcasted_iota(jnp.int32, (WIN_KEYS, TQ), 0)
    mask_w = (key <= t) & (key > t - WINDOW)
    p_heads = []
    for h in heads:
        sm = jnp.where(mask_w, lanes(s, h), NEG)
        p_heads.append(jnp.exp2(sm - colmax(sm)).astype(BF16))
    accw = _dot(vwT_ref[:, pl.ds(w0, WIN_KEYS)], jnp.concatenate(p_heads, axis=1))
    owT = accw[0:NSA_DH, :] * (1.0 / accw[NSA_DH:NSA_DH + 1, :])

    gT = gT_ref[...]
    ocT = oc_ref[...]
    outs = []
    for h in heads:
        g_c, g_s, g_w = (gT[3 * h + b:3 * h + b + 1, :] for b in range(3))
        outs.append(g_c * lanes(ocT, h)[0:NSA_DH, :] + g_s * lanes(osT, h) + g_w * lanes(owT, h))
    o_ref[...] = jnp.concatenate(outs, axis=0).T


def _nsa(qT, gT, kc4, vcT, ks, vsT, kw, vwT):
    G, S = ks.shape[0], ks.shape[1]
    NC, NB = kc4.shape[1], S // SEL_BLOCK
    per_g = lambda a: pl.BlockSpec((None,) + a.shape[1:], lambda g, i: (g,) + (0,) * (a.ndim - 1))
    wide = NSA_HPG * TQ
    return pl.pallas_call(
        _nsa_kernel,
        grid=(G, S // TQ),
        in_specs=[
            pl.BlockSpec((NSA_HPG * NSA_DH, TQ), lambda g, i: (g, i)),
            pl.BlockSpec((None, 16, TQ), lambda g, i: (g, 0, i)),
            per_g(kc4), per_g(vcT), per_g(ks), per_g(vsT), per_g(kw), per_g(vwT),
        ],
        out_specs=pl.BlockSpec((TQ, NSA_HPG * NSA_DH), lambda g, i: (i, g)),
        out_shape=jax.ShapeDtypeStruct((S, NSA_HEADS * NSA_DH), F32),
        scratch_shapes=[pltpu.VMEM((NB, TQ), F32),
                        pltpu.VMEM((TK, wide), F32),
                        pltpu.VMEM((TK, wide), F32),
                        pltpu.VMEM((NSA_HPG, SC_PAD + NC, TQ), F32),
                        pltpu.VMEM((V_ROWS, wide), F32),
                        pltpu.VMEM((1, wide), F32),
                        pltpu.VMEM((V_ROWS, wide), F32)],
        compiler_params=_params(("arbitrary", "arbitrary"), V7X_VMEM_LIMIT_BYTES),
        name="nsa",
    )(qT, gT, kc4, vcT, ks, vsT, kw, vwT)


def _ret_kernel(q_ref, k_ref, v_ref, o_ref, r_ref, dm_ref, qd_ref, kd_ref):
    C = RET_CHUNK
    log_g = [math.log(1.0 - 2.0 ** (-5.0 - h)) for h in range(RET_HEADS)]

    @pl.when(pl.program_id(0) == 0)
    def _():
        r_ref[...] = jnp.zeros(r_ref.shape, F32)
        diff = (lax.broadcasted_iota(jnp.int32, (C, C), 0)
                - lax.broadcasted_iota(jnp.int32, (C, C), 1)).astype(F32)
        n = lax.broadcasted_iota(jnp.int32, (C, RET_DK), 0).astype(F32)
        for h in range(RET_HEADS):
            dm_ref[h] = jnp.where(diff >= 0.0, jnp.exp(jnp.maximum(diff, 0.0) * log_g[h]), 0.0)
            qd_ref[h] = jnp.exp((n + 1.0) * log_g[h])
            kd_ref[h] = jnp.exp((C - 1.0 - n) * log_g[h])

    for h in range(RET_HEADS):
        q = q_ref[:, h * RET_DK:(h + 1) * RET_DK]
        k = k_ref[:, h * RET_DK:(h + 1) * RET_DK]
        v = v_ref[:, h * RET_DV:(h + 1) * RET_DV]
        att = lax.dot_general(q, k, (((1,), (1,)), ((), ())), preferred_element_type=F32)
        o = _dot((att * dm_ref[h]).astype(BF16), v)
        r = r_ref[h]
        r_hi, r_lo = _split(r)
        qd = (q.astype(F32) * qd_ref[h]).astype(BF16)
        o = o + _dot(qd, r_hi) + _dot(qd, r_lo)
        kd = (k.astype(F32) * kd_ref[h]).astype(BF16)
        r_ref[h] = math.exp(C * log_g[h]) * r + lax.dot_general(
            kd, v, (((0,), (0,)), ((), ())), preferred_element_type=F32)
        mu = jnp.mean(o, axis=-1, keepdims=True)
        d = o - mu
        var = jnp.mean(d * d, axis=-1, keepdims=True)
        o_ref[:, h * RET_DV:(h + 1) * RET_DV] = (d * lax.rsqrt(var + GN_EPS)).astype(BF16)


def _retention(qr, kr, vr):
    S = qr.shape[0]
    C = RET_CHUNK
    row = lambda w: pl.BlockSpec((C, w), lambda n: (n, 0))
    return pl.pallas_call(
        _ret_kernel,
        grid=(S // C,),
        in_specs=[row(RET_HEADS * RET_DK), row(RET_HEADS * RET_DK), row(RET_HEADS * RET_DV)],
        out_specs=row(RET_HEADS * RET_DV),
        out_shape=jax.ShapeDtypeStruct((S, RET_HEADS * RET_DV), BF16),
        scratch_shapes=[
            pltpu.VMEM((RET_HEADS, RET_DK, RET_DV), F32),
            pltpu.VMEM((RET_HEADS, C, C), F32),
            pltpu.VMEM((RET_HEADS, C, RET_DK), F32),
            pltpu.VMEM((RET_HEADS, C, RET_DK), F32),
        ],
        compiler_params=_params(("arbitrary",), V7X_VMEM_LIMIT_BYTES),
        name="retention",
    )(qr, kr, vr)


def _post_kernel(x_ref, oa_ref, zn_ref, ob_ref, zr_ref, ma_ref, mb_ref,
                 wa_ref, wb_ref, wo_ref, g_ref, out_ref):
    f32 = lambda r: r[...].astype(F32)
    ya = _dot((oa_ref[...] * f32(zn_ref)).astype(BF16), wa_ref[...])
    yb = _dot((f32(ob_ref) * f32(zr_ref)).astype(BF16), wb_ref[...])
    merged = f32(ma_ref) * ya + f32(mb_ref) * yb
    y = _dot(merged.astype(BF16), wo_ref[...])
    ms = jnp.mean(y * y, axis=-1, keepdims=True)
    out_ref[...] = x_ref[...] + y * lax.rsqrt(ms + RMS_EPS) * g_ref[...]


def _post(x2, oa, zn, ob, zr, ma, mb, wa, wb, wo, g_post):
    S = x2.shape[0]
    tm = ROW_TILE
    row = lambda a: pl.BlockSpec((tm, a.shape[1]), lambda i: (i, 0))
    whole = lambda a: pl.BlockSpec(a.shape, lambda i: (0, 0))
    rows = (x2, oa, zn, ob, zr, ma, mb)
    consts = (wa, wb, wo, g_post)
    return pl.pallas_call(
        _post_kernel,
        grid=(S // tm,),
        in_specs=[row(a) for a in rows] + [whole(a) for a in consts],
        out_specs=row(x2),
        out_shape=jax.ShapeDtypeStruct(x2.shape, x2.dtype),
        compiler_params=_params(("arbitrary",), V7X_VMEM_LIMIT_BYTES),
        name="post",
    )(*rows, *consts)


def _rotary_tables(S):
    pos = jnp.arange(S, dtype=F32)
    nsa_inv = 1.0 / (ROPE_THETA ** (jnp.arange(0, ROPE_DIM, 2, dtype=F32) / ROPE_DIM))
    ang = pos[:, None] * nsa_inv[None, :]
    c, s = jnp.cos(ang), jnp.sin(ang)
    half = ROPE_DIM // 2
    rest = NSA_DH - ROPE_DIM
    one, zero, zh = jnp.ones((S, rest), F32), jnp.zeros((S, rest), F32), jnp.zeros((S, half), F32)
    per_head = lambda parts: jnp.tile(jnp.concatenate(parts, axis=1), (1, LANE // NSA_DH))
    cn = per_head([c, c, one])
    san = per_head([zh, s, zero])
    sbn = per_head([-s, zh, zero])
    ret_inv = 1.0 / (RET_ROPE_BASE ** jnp.linspace(0.0, 1.0, RET_DK // 2, dtype=F32))
    rang = pos[:, None] * ret_inv[None, :]
    rc, rs = jnp.cos(rang), jnp.sin(rang)
    cr = jnp.concatenate([rc, rc], axis=1)
    sr = jnp.concatenate([-rs, rs], axis=1)
    return cn, san, sbn, cr, sr


def _value_rows(vT):
    G, d, n = vT.shape
    return jnp.concatenate(
        [vT, jnp.ones((G, 1, n), vT.dtype), jnp.zeros((G, V_ROWS - d - 1, n), vT.dtype)],
        axis=1).astype(BF16)


def _per_group(a):
    S = a.shape[0]
    return a.reshape(S, NSA_GROUPS, NSA_DH).transpose(1, 0, 2)


def kernel(x, norm_pre, w_in, b_nsa_gate, cmp_pe_k, cmp_w1_k, cmp_w2_k, cmp_pe_v, cmp_w1_v,
           cmp_w2_v, w_nsa_o, w_ret_o, w_out, norm_post):
    B, S, _ = x.shape
    assert B == 1 and S % (2 * TK) == 0 and S >= WIN_KEYS
    assert S % (SEL_BLOCK * 8 * N_SIZE_CLASSES) == 0 and (S // TQ) % N_SIZE_CLASSES == 0
    x2 = x.reshape(S, D_MODEL)

    w_bf = w_in.astype(BF16)
    segs, off = [], 0
    for sz, pad in zip(PROJ_SIZES, _PAD_SIZES):
        segs.append(jnp.pad(w_bf[:, off:off + sz], ((0, 0), (0, pad - sz))))
        off += sz
    w_hi = jnp.concatenate(segs, axis=1)
    w_lo = (w_in[:, :HI_COLS] - w_bf[:, :HI_COLS].astype(F32)).astype(BF16)
    bias = jnp.pad(b_nsa_gate, (0, LANE - b_nsa_gate.shape[0])).reshape(1, LANE)

    qn, kcv, kv, gate, zn, qr, kr, vr, zr, ma, mb = _inproj(
        x2, norm_pre.reshape(1, D_MODEL), w_hi, w_lo, bias, _rotary_tables(S))

    NC = S // CMP_STRIDE
    def chunks(a):
        c = a.reshape(NC, CMP_STRIDE, NSA_GROUPS, NSA_DH).transpose(2, 0, 1, 3)
        c = c.reshape(NSA_GROUPS, NC, CMP_STRIDE * NSA_DH)
        return jnp.concatenate([c, jnp.roll(c, -1, axis=1)], axis=-1)
    flat = jnp.stack([chunks(kcv[:, 0:LANE]), chunks(kcv[:, LANE:2 * LANE])])
    pe_flat = jnp.stack([cmp_pe_k.reshape(1, -1), cmp_pe_v.reshape(1, -1)])
    cmp = _compress(flat, pe_flat, jnp.stack([cmp_w1_k, cmp_w1_v]), jnp.stack([cmp_w2_k, cmp_w2_v]))
    kc_hi = cmp[0].astype(BF16)
    kc_lo = (cmp[0] - kc_hi.astype(F32)).astype(BF16)
    kc4 = jnp.concatenate([kc_hi, kc_hi, kc_lo, kc_lo], axis=-1)
    vcT = _value_rows(cmp[1].transpose(0, 2, 1))

    blk_in_tile = (jnp.arange(S) % TK) // SEL_BLOCK
    onehot = (blk_in_tile[:, None] == jnp.arange(K_LANES - NSA_DH)[None, :]).astype(BF16)
    ks = jnp.concatenate([_per_group(kv[:, 0:LANE]),
                          jnp.broadcast_to(onehot, (NSA_GROUPS,) + onehot.shape)], axis=-1)
    vsT = _value_rows(_per_group(kv[:, LANE:2 * LANE]).transpose(0, 2, 1))
    kw = _per_group(kv[:, 2 * LANE:3 * LANE])
    vwT = _value_rows(_per_group(kv[:, 3 * LANE:4 * LANE]).transpose(0, 2, 1))
    gT = gate[:, :NSA_HEADS * 3].reshape(S, NSA_GROUPS, NSA_HPG * 3).transpose(1, 2, 0)
    gT = jnp.pad(gT, ((0, 0), (0, 16 - NSA_HPG * 3), (0, 0)))

    oa = _nsa(qn.T, gT, kc4, vcT, ks, vsT, kw, vwT)
    ob = _retention(qr, kr, vr)

    out = _post(x2, oa, zn, ob, zr, ma, mb, w_nsa_o.astype(BF16), w_ret_o.astype(BF16),
                w_out.astype(BF16), norm_post.reshape(1, D_MODEL))
    return out.reshape(B, S, D_MODEL)
```

```python
import functools
import math

import jax
import jax.numpy as jnp
from jax import lax
from jax.experimental import pallas as pl
from jax.experimental.pallas import tpu as pltpu

F32 = jnp.float32
BF16 = jnp.bfloat16

D_MODEL = 1024
NSA_HEADS = 8
NSA_GROUPS = 2
NSA_HPG = NSA_HEADS // NSA_GROUPS
NSA_DH = 64
CMP_LEN = 32
CMP_STRIDE = 16
CMP_HIDDEN = 256
SEL_BLOCK = 64
SEL_TOPN = 16
WINDOW = 512
ROPE_THETA = 500000.0
ROPE_DIM = NSA_DH // 4
NSA_SCALE = NSA_DH ** -0.5
RET_HEADS = 4
RET_DK = 128
RET_DV = 256
RET_ROPE_BASE = 10000.0
RET_SCALE = RET_DK ** -0.5
RMS_EPS = 1e-6
GN_EPS = 1e-6
PROJ_SIZES = (512, 128, 128, 128, 128, 128, 128, 24, 512, 512, 512, 1024, 1024, 1024, 1024)

LANE = 128
V7X_VMEM_LIMIT_BYTES = 56 * 1024 * 1024

ROW_TILE = 256
TQ = 128
TK = 512
BLK_PER_TILE = TK // SEL_BLOCK
WIN_KEYS = WINDOW + TQ
RET_CHUNK = 256
V_ROWS = 80
NEG = -1e30
LOG2E = math.log2(math.e)
K_LANES = LANE
GATE_ROWS = 16
SC_PAD = 8
N_SIZE_CLASSES = 4

_PAD_SIZES = tuple(128 if s == 24 else s for s in PROJ_SIZES)
_OFF = [0]
for _s in _PAD_SIZES:
    _OFF.append(_OFF[-1] + _s)
PROJ_PAD = _OFF[-1]
HI_COLS = _OFF[2]


def _dot(a, b):
    return jnp.dot(a, b, preferred_element_type=F32)


def _split(a):
    hi = a.astype(BF16)
    lo = (a - hi.astype(F32)).astype(BF16)
    return hi, lo


def _dot3(a_hi, a_lo, b_hi, b_lo):
    return _dot(a_hi, b_hi) + _dot(a_lo, b_hi) + _dot(a_hi, b_lo)


def _params(sem, vmem_bytes):
    return pltpu.CompilerParams(dimension_semantics=sem, vmem_limit_bytes=vmem_bytes)


def _inproj_kernel(x_ref, g_ref, whi_ref, wlo_ref, bias_ref, cn_ref, san_ref, sbn_ref,
                   cr_ref, sr_ref,
                   qT_ref, kc_ref, vc_ref, ks_ref, vsT_ref, kw_ref, vwT_ref, gT_ref,
                   zn_ref, qr_ref, kr_ref, vr_ref, zr_ref, ma_ref, mb_ref):
    tm = x_ref.shape[0]
    x = x_ref[...]
    ms = jnp.mean(x * x, axis=-1, keepdims=True)
    h = x * lax.rsqrt(ms + RMS_EPS) * g_ref[...]
    h_hi, h_lo = _split(h)

    def mm(lo, hi):
        return _dot(h_hi, whi_ref[:, lo:hi])

    cn, san, sbn = cn_ref[...], san_ref[...], sbn_ref[...]
    cr, sr = cr_ref[...], sr_ref[...]

    def rot_nsa(p):
        return p * cn + pltpu.roll(p, 8, 1) * san + pltpu.roll(p, LANE - 8, 1) * sbn

    def rot_ret(p):
        return p * cr + pltpu.roll(p, LANE // 2, 1) * sr

    lane = lax.broadcasted_iota(jnp.int32, (tm, LANE), 1)
    first_group = lane < NSA_DH

    def key_rows(p, fill):
        swapped = pltpu.roll(p, NSA_DH, 1)
        return [jnp.where(first_group, pg, fill).astype(BF16) for pg in (p, swapped)]

    ones_row = jnp.where(lax.broadcasted_iota(jnp.int32, (V_ROWS - NSA_DH, tm), 0) == 0, 1.0, 0.0)

    def value_rows(p):
        pT = p.T
        return [jnp.concatenate([pT[g * NSA_DH:(g + 1) * NSA_DH, :], ones_row], axis=0).astype(BF16)
                for g in range(NSA_GROUPS)]

    p = (mm(0, HI_COLS) + _dot(h_lo, whi_ref[:, 0:HI_COLS])
         + _dot(h_hi, wlo_ref[...]))
    for c in range(4):
        qT_ref[c * LANE:(c + 1) * LANE, :] = (rot_nsa(p[:, c * LANE:(c + 1) * LANE]) * NSA_SCALE).T
    kc_ref[...] = rot_nsa(p[:, 4 * LANE:5 * LANE])

    p = mm(_OFF[2], _OFF[8])
    vc_ref[...] = p[:, 0 * LANE:1 * LANE]
    pos = pl.program_id(0) * tm + lax.broadcasted_iota(jnp.int32, (tm, LANE), 0)
    blk_in_tile = jnp.right_shift(jnp.bitwise_and(pos, TK - 1), SEL_BLOCK.bit_length() - 1)
    onehot = jnp.where(lane - NSA_DH == blk_in_tile, 1.0, 0.0)
    ks = key_rows(rot_nsa(p[:, 1 * LANE:2 * LANE]), onehot)
    vs = value_rows(p[:, 2 * LANE:3 * LANE])
    kw = key_rows(rot_nsa(p[:, 3 * LANE:4 * LANE]), 0.0)
    vw = value_rows(p[:, 4 * LANE:5 * LANE])
    gT = jax.nn.sigmoid(p[:, 5 * LANE:6 * LANE] + bias_ref[...]).T
    for g in range(NSA_GROUPS):
        ks_ref[g], vsT_ref[g], kw_ref[g], vwT_ref[g] = ks[g], vs[g], kw[g], vw[g]
        gT_ref[g] = gT[g * GATE_ROWS:(g + 1) * GATE_ROWS, :]

    z = mm(_OFF[8], _OFF[9])
    zn_ref[...] = (z * jax.nn.sigmoid(z)).astype(BF16)

    p = mm(_OFF[9], _OFF[11])
    for c in range(4):
        qr_ref[:, c * LANE:(c + 1) * LANE] = rot_ret(p[:, c * LANE:(c + 1) * LANE]).astype(BF16)
    for c in range(4):
        kr_ref[:, c * LANE:(c + 1) * LANE] = (
            rot_ret(p[:, (4 + c) * LANE:(5 + c) * LANE]) * RET_SCALE).astype(BF16)

    vr_ref[...] = mm(_OFF[11], _OFF[12]).astype(BF16)
    z = mm(_OFF[12], _OFF[13])
    zr_ref[...] = (z * jax.nn.sigmoid(z)).astype(BF16)
    ma_ref[...] = jax.nn.sigmoid(mm(_OFF[13], _OFF[14])).astype(BF16)
    mb_ref[...] = jax.nn.sigmoid(mm(_OFF[14], _OFF[15])).astype(BF16)


def _inproj(x2, norm_pre, w_hi, w_lo, bias, tabs):
    S = x2.shape[0]
    tm = ROW_TILE
    G = NSA_GROUPS
    row = lambda w: pl.BlockSpec((tm, w), lambda i: (i, 0))
    whole = lambda a: pl.BlockSpec(a.shape, lambda i: (0,) * a.ndim)
    col = lambda r: pl.BlockSpec((r, tm), lambda i: (0, i))
    g_row = lambda w: pl.BlockSpec((G, tm, w), lambda i: (0, i, 0))
    g_col = lambda r: pl.BlockSpec((G, r, tm), lambda i: (0, 0, i))
    sds = jax.ShapeDtypeStruct
    outs = [
        (col(NSA_HEADS * NSA_DH), sds((NSA_HEADS * NSA_DH, S), F32)),
        (row(LANE), sds((S, LANE), F32)),
        (row(LANE), sds((S, LANE), F32)),
        (g_row(K_LANES), sds((G, S, K_LANES), BF16)),
        (g_col(V_ROWS), sds((G, V_ROWS, S), BF16)),
        (g_row(K_LANES), sds((G, S, K_LANES), BF16)),
        (g_col(V_ROWS), sds((G, V_ROWS, S), BF16)),
        (g_col(GATE_ROWS), sds((G, GATE_ROWS, S), F32)),
        (row(512), sds((S, 512), BF16)),
        (row(512), sds((S, 512), BF16)), (row(512), sds((S, 512), BF16)),
        (row(1024), sds((S, 1024), BF16)), (row(1024), sds((S, 1024), BF16)),
        (row(1024), sds((S, 1024), BF16)), (row(1024), sds((S, 1024), BF16)),
    ]
    return pl.pallas_call(
        _inproj_kernel,
        grid=(S // tm,),
        in_specs=[row(D_MODEL), whole(norm_pre), whole(w_hi), whole(w_lo), whole(bias)]
        + [row(LANE)] * 5,
        out_specs=[o[0] for o in outs],
        out_shape=[o[1] for o in outs],
        compiler_params=_params(("arbitrary",), V7X_VMEM_LIMIT_BYTES),
        name="inproj",
    )(x2, norm_pre, w_hi, w_lo, bias, *tabs)


def _compress_kernel(x_ref, pe_ref, w1x_ref, w1_ref, w2_ref, o_ref, *, keys):
    NC = x_ref.shape[0]
    prod = lambda a, b: _dot3(*_split(a), *_split(b))
    both = prod(x_ref[...], w1x_ref[...])
    nxt = pltpu.roll(both[:, CMP_HIDDEN:], NC - 1, 0)
    pe_term = prod(pe_ref[...], w1_ref[...])[0:1, :]
    hid = both[:, :CMP_HIDDEN] + nxt + pe_term
    act = hid * (0.5 * (1.0 + jnp.tanh(math.sqrt(2.0 / math.pi) * (hid + 0.044715 * (hid * hid * hid)))))
    out = prod(act, w2_ref[...])
    if keys:
        hi, lo = _split(out)
        o_ref[...] = jnp.concatenate([hi, hi, lo, lo], axis=1)
    else:
        ones_row = jnp.where(lax.broadcasted_iota(jnp.int32, (V_ROWS - NSA_DH, NC), 0) == 0, 1.0, 0.0)
        o_ref[...] = jnp.concatenate([out.T, ones_row], axis=0).astype(BF16)


def _compress(raw, pe, w1, w2, keys):
    S = raw.shape[0]
    NC = S // CMP_STRIDE
    G = NSA_GROUPS
    half = CMP_LEN // 2
    x = raw.reshape(NC, CMP_STRIDE * G * NSA_DH)
    w1r = w1.reshape(2, half, 1, NSA_DH, CMP_HIDDEN)
    sel = (jnp.arange(G)[:, None] == jnp.arange(G)[None, :]).astype(F32)
    w1x = w1r[None] * sel[:, None, None, :, None, None]
    w1x = w1x.reshape(G, 2, half * G * NSA_DH, CMP_HIDDEN).transpose(0, 2, 1, 3)
    w1x = w1x.reshape(G, half * G * NSA_DH, 2 * CMP_HIDDEN)
    pe8 = jnp.pad(pe.reshape(1, -1), ((0, 7), (0, 0)))
    if keys:
        out_spec = pl.BlockSpec((None, NC, 4 * NSA_DH), lambda g: (g, 0, 0))
        out_shape = jax.ShapeDtypeStruct((G, NC, 4 * NSA_DH), BF16)
    else:
        out_spec = pl.BlockSpec((None, V_ROWS, NC), lambda g: (g, 0, 0))
        out_shape = jax.ShapeDtypeStruct((G, V_ROWS, NC), BF16)
    whole = lambda a: pl.BlockSpec(a.shape, lambda g: (0,) * a.ndim)
    return pl.pallas_call(
        functools.partial(_compress_kernel, keys=keys),
        grid=(G,),
        in_specs=[whole(x), whole(pe8),
                  pl.BlockSpec((None,) + w1x.shape[1:], lambda g: (g, 0, 0)),
                  whole(w1), whole(w2)],
        out_specs=out_spec,
        out_shape=out_shape,
        compiler_params=_params(("arbitrary",), V7X_VMEM_LIMIT_BYTES),
        name="compress_k" if keys else "compress_v",
    )(x, pe8, w1x, w1, w2)


def _nsa_kernel(qT_ref, gT_ref, kc4_ref, vcT_ref, ks_ref, vsT_ref, kw_ref, vwT_ref, o_ref,
                sb_ref, sa_ref, sbuf_ref, sc_ref, oc_ref, m_ref, acc_ref):
    i = pl.program_id(1)
    NC = kc4_ref.shape[0]
    NB = sb_ref.shape[0]
    n_q = pl.num_programs(1)
    heads = range(NSA_HPG)
    lanes = lambda a, h: a[:, h * TQ:(h + 1) * TQ]
    colmax = lambda a: jnp.max(a, axis=0, keepdims=True)
    colsum = lambda a: jnp.sum(a, axis=0, keepdims=True)

    qT = qT_ref[...] * LOG2E
    Q = jnp.concatenate([qT[h * NSA_DH:(h + 1) * NSA_DH, :] for h in heads], axis=1)
    q_hi, q_lo = _split(Q)
    q4 = jnp.concatenate([q_hi, q_lo, q_hi, q_lo], axis=0)
    t = i * TQ + lax.broadcasted_iota(jnp.int32, (1, TQ), 1)
    cur = jnp.right_shift(t, SEL_BLOCK.bit_length() - 1)

    def compressed_and_select(ncp, nbp):
        s = _dot(kc4_ref[0:ncp, :], q4)
        c_end = lax.broadcasted_iota(jnp.int32, (ncp, TQ), 0) * CMP_STRIDE + (CMP_LEN - 1)
        mask_c = c_end <= t
        p_heads, inv = [], []
        for h in heads:
            sm = jnp.where(mask_c, lanes(s, h), -jnp.inf)
            m = colmax(sm)
            m = jnp.where(m == -jnp.inf, 0.0, m)
            p = jnp.exp2(sm - m)
            inv.append(1.0 / jnp.maximum(colsum(p), jnp.finfo(F32).tiny))
            sc_ref[h, 0:SC_PAD, :] = jnp.zeros((SC_PAD, TQ), F32)
            sc_ref[h, SC_PAD:SC_PAD + ncp, :] = p
            p_heads.append(p.astype(BF16))
        oc_ref[...] = (_dot(vcT_ref[:, 0:ncp], jnp.concatenate(p_heads, axis=1))
                       * jnp.concatenate(inv, axis=1))
        per_sel = SEL_BLOCK // CMP_STRIDE
        imp = jnp.zeros((nbp, TQ), F32)
        for h in heads:
            tot = sc_ref[h, pl.ds(SC_PAD - 1, nbp, stride=per_sel), :]
            for r in range(per_sel):
                tot = tot + sc_ref[h, pl.ds(SC_PAD + r, nbp, stride=per_sel), :]
            imp = imp + tot * inv[h]
        blk = lax.broadcasted_iota(jnp.int32, (nbp, TQ), 0)
        blk_f = blk.astype(F32)
        valid = blk <= cur
        forced = (blk == 0) | (blk == cur) | (blk == cur - 1)
        free = valid & jnp.logical_not(forced)
        score = jnp.where(free, imp, -jnp.inf)
        for _ in range(min(SEL_TOPN, NB) - 3):
            mx = colmax(score)
            idx = jnp.min(jnp.where(score == mx, blk_f, float(NB)), axis=0, keepdims=True)
            score = jnp.where(blk_f == idx, -jnp.inf, score)
        picked = valid & (forced | (score == -jnp.inf))
        sb_ref[0:nbp, :] = jnp.where(picked, 0.0, NEG)
        if nbp < NB:
            sb_ref[nbp:NB, :] = jnp.full((NB - nbp, TQ), NEG, F32)

    size_class = lax.div(i * N_SIZE_CLASSES, n_q)
    for k in range(N_SIZE_CLASSES):
        pl.when(size_class == k)(functools.partial(
            compressed_and_select, NC * (k + 1) // N_SIZE_CLASSES, NB * (k + 1) // N_SIZE_CLASSES))

    pad_rows = jnp.zeros((K_LANES - NSA_DH - 2 * BLK_PER_TILE, NSA_HPG * TQ), BF16)

    def scores(j):
        k0 = pl.multiple_of(j * TK, TK)
        sbt = sb_ref[pl.ds(pl.multiple_of(j * BLK_PER_TILE, BLK_PER_TILE), BLK_PER_TILE), :]
        rows = jnp.concatenate([sbt, jnp.zeros_like(sbt)], axis=0)
        rows = jnp.concatenate([rows] * NSA_HPG, axis=1).astype(BF16)
        w = jnp.concatenate([q_hi, rows, pad_rows], axis=0)
        return _dot(ks_ref[pl.ds(k0, TK), :], w)

    def attend(j, s, carry, causal):
        m, acc = carry
        k0 = pl.multiple_of(j * TK, TK)
        if causal:
            keep = k0 + lax.broadcasted_iota(jnp.int32, (TK, TQ), 0) <= t
            s = jnp.concatenate([jnp.where(keep, lanes(s, h), NEG) for h in heads], axis=1)
        m_new = jnp.maximum(m, colmax(s))
        alpha = jnp.exp2(m - m_new)
        p = jnp.exp2(s - m_new).astype(BF16)
        acc = acc * alpha + _dot(vsT_ref[:, pl.ds(k0, TK)], p)
        return m_new, acc

    jd = lax.div(i, TK // TQ)
    n_pairs = lax.div(jd, 2)
    sa_ref[...] = scores(0)

    def pair(pp, carry):
        sbuf_ref[...] = scores(2 * pp + 1)
        carry = attend(2 * pp, sa_ref[...], carry, False)
        sa_ref[...] = scores(2 * pp + 2)
        return attend(2 * pp + 1, sbuf_ref[...], carry, False)

    carry = (jnp.full((1, NSA_HPG * TQ), NEG, F32), jnp.zeros((V_ROWS, NSA_HPG * TQ), F32))
    n_quads = lax.div(n_pairs, 2)
    carry = lax.fori_loop(0, n_quads, lambda qq, c: pair(2 * qq + 1, pair(2 * qq, c)), carry)
    carry = lax.fori_loop(2 * n_quads, n_pairs, pair, carry)
    sbuf_ref[...] = scores(2 * n_pairs + 1)
    m_ref[...], acc_ref[...] = attend(2 * n_pairs, sa_ref[...], carry, True)

    @pl.when(jd != 2 * n_pairs)
    def _():
        m_ref[...], acc_ref[...] = attend(
            2 * n_pairs + 1, sbuf_ref[...], (m_ref[...], acc_ref[...]), True)

    acc = acc_ref[...]
    osT = acc[0:NSA_DH, :] * (1.0 / acc[NSA_DH:NSA_DH + 1, :])

    w0 = pl.multiple_of(jnp.maximum(i * TQ - WINDOW, 0), TQ)
    q_pad = jnp.concatenate([q_hi, jnp.zeros((K_LANES - NSA_DH, NSA_HPG * TQ), BF16)], axis=0)
    s = _dot(kw_ref[pl.ds(w0, WIN_KEYS), :], q_pad)
    key = w0 + lax.broadcasted_iota(jnp.int32, (WIN_KEYS, TQ), 0)
    mask_w = (key <= t) & (key > t - WINDOW)
    p_heads = []
    for h in heads:
        sm = jnp.where(mask_w, lanes(s, h), NEG)
        p_heads.append(jnp.exp2(sm - colmax(sm)).astype(BF16))
    accw = _dot(vwT_ref[:, pl.ds(w0, WIN_KEYS)], jnp.concatenate(p_heads, axis=1))
    owT = accw[0:NSA_DH, :] * (1.0 / accw[NSA_DH:NSA_DH + 1, :])

    gT = gT_ref[...]
    ocT = oc_ref[...]
    outs = []
    for h in heads:
        g_c, g_s, g_w = (gT[3 * h + b:3 * h + b + 1, :] for b in range(3))
        outs.append(g_c * lanes(ocT, h)[0:NSA_DH, :] + g_s * lanes(osT, h) + g_w * lanes(owT, h))
    o_ref[...] = jnp.concatenate(outs, axis=0).T


def _nsa(qT, gT, kc4, vcT, ks, vsT, kw, vwT):
    G, S = ks.shape[0], ks.shape[1]
    NC, NB = kc4.shape[1], S // SEL_BLOCK
    per_g = lambda a: pl.BlockSpec((None,) + a.shape[1:], lambda g, i: (g,) + (0,) * (a.ndim - 1))
    wide = NSA_HPG * TQ
    return pl.pallas_call(
        _nsa_kernel,
        grid=(G, S // TQ),
        in_specs=[
            pl.BlockSpec((NSA_HPG * NSA_DH, TQ), lambda g, i: (g, i)),
            pl.BlockSpec((None, GATE_ROWS, TQ), lambda g, i: (g, 0, i)),
            per_g(kc4), per_g(vcT), per_g(ks), per_g(vsT), per_g(kw), per_g(vwT),
        ],
        out_specs=pl.BlockSpec((TQ, NSA_HPG * NSA_DH), lambda g, i: (i, g)),
        out_shape=jax.ShapeDtypeStruct((S, NSA_HEADS * NSA_DH), F32),
        scratch_shapes=[pltpu.VMEM((NB, TQ), F32),
                        pltpu.VMEM((TK, wide), F32),
                        pltpu.VMEM((TK, wide), F32),
                        pltpu.VMEM((NSA_HPG, SC_PAD + NC, TQ), F32),
                        pltpu.VMEM((V_ROWS, wide), F32),
                        pltpu.VMEM((1, wide), F32),
                        pltpu.VMEM((V_ROWS, wide), F32)],
        compiler_params=_params(("arbitrary", "arbitrary"), V7X_VMEM_LIMIT_BYTES),
        name="nsa",
    )(qT, gT, kc4, vcT, ks, vsT, kw, vwT)


def _ret_kernel(q_ref, k_ref, v_ref, o_ref, r_ref, dm_ref, qd_ref, kd_ref):
    C = RET_CHUNK
    log_g = [math.log(1.0 - 2.0 ** (-5.0 - h)) for h in range(RET_HEADS)]

    @pl.when(pl.program_id(0) == 0)
    def _():
        r_ref[...] = jnp.zeros(r_ref.shape, F32)
        diff = (lax.broadcasted_iota(jnp.int32, (C, C), 0)
                - lax.broadcasted_iota(jnp.int32, (C, C), 1)).astype(F32)
        n = lax.broadcasted_iota(jnp.int32, (C, RET_DK), 0).astype(F32)
        for h in range(RET_HEADS):
            dm_ref[h] = jnp.where(diff >= 0.0, jnp.exp(jnp.maximum(diff, 0.0) * log_g[h]), 0.0)
            qd_ref[h] = jnp.exp((n + 1.0) * log_g[h])
            kd_ref[h] = jnp.exp((C - 1.0 - n) * log_g[h])

    for h in range(RET_HEADS):
        q = q_ref[:, h * RET_DK:(h + 1) * RET_DK]
        k = k_ref[:, h * RET_DK:(h + 1) * RET_DK]
        v = v_ref[:, h * RET_DV:(h + 1) * RET_DV]
        att = lax.dot_general(q, k, (((1,), (1,)), ((), ())), preferred_element_type=F32)
        o = _dot((att * dm_ref[h]).astype(BF16), v)
        r = r_ref[h]
        r_hi, r_lo = _split(r)
        qd = (q.astype(F32) * qd_ref[h]).astype(BF16)
        o = o + _dot(qd, r_hi) + _dot(qd, r_lo)
        kd = (k.astype(F32) * kd_ref[h]).astype(BF16)
        r_ref[h] = math.exp(C * log_g[h]) * r + lax.dot_general(
            kd, v, (((0,), (0,)), ((), ())), preferred_element_type=F32)
        mu = jnp.mean(o, axis=-1, keepdims=True)
        d = o - mu
        var = jnp.mean(d * d, axis=-1, keepdims=True)
        o_ref[:, h * RET_DV:(h + 1) * RET_DV] = (d * lax.rsqrt(var + GN_EPS)).astype(BF16)


def _retention(qr, kr, vr):
    S = qr.shape[0]
    C = RET_CHUNK
    row = lambda w: pl.BlockSpec((C, w), lambda n: (n, 0))
    return pl.pallas_call(
        _ret_kernel,
        grid=(S // C,),
        in_specs=[row(RET_HEADS * RET_DK), row(RET_HEADS * RET_DK), row(RET_HEADS * RET_DV)],
        out_specs=row(RET_HEADS * RET_DV),
        out_shape=jax.ShapeDtypeStruct((S, RET_HEADS * RET_DV), BF16),
        scratch_shapes=[
            pltpu.VMEM((RET_HEADS, RET_DK, RET_DV), F32),
            pltpu.VMEM((RET_HEADS, C, C), F32),
            pltpu.VMEM((RET_HEADS, C, RET_DK), F32),
            pltpu.VMEM((RET_HEADS, C, RET_DK), F32),
        ],
        compiler_params=_params(("arbitrary",), V7X_VMEM_LIMIT_BYTES),
        name="retention",
    )(qr, kr, vr)


def _post_kernel(x_ref, oa_ref, zn_ref, ob_ref, zr_ref, ma_ref, mb_ref,
                 wa_ref, wb_ref, wo_ref, g_ref, out_ref):
    f32 = lambda r: r[...].astype(F32)
    ya = _dot((oa_ref[...] * f32(zn_ref)).astype(BF16), wa_ref[...])
    yb = _dot((f32(ob_ref) * f32(zr_ref)).astype(BF16), wb_ref[...])
    merged = f32(ma_ref) * ya + f32(mb_ref) * yb
    y = _dot(merged.astype(BF16), wo_ref[...])
    ms = jnp.mean(y * y, axis=-1, keepdims=True)
    out_ref[...] = x_ref[...] + y * lax.rsqrt(ms + RMS_EPS) * g_ref[...]


def _post(x2, oa, zn, ob, zr, ma, mb, wa, wb, wo, g_post):
    S = x2.shape[0]
    tm = ROW_TILE
    row = lambda a: pl.BlockSpec((tm, a.shape[1]), lambda i: (i, 0))
    whole = lambda a: pl.BlockSpec(a.shape, lambda i: (0, 0))
    rows = (x2, oa, zn, ob, zr, ma, mb)
    consts = (wa, wb, wo, g_post)
    return pl.pallas_call(
        _post_kernel,
        grid=(S // tm,),
        in_specs=[row(a) for a in rows] + [whole(a) for a in consts],
        out_specs=row(x2),
        out_shape=jax.ShapeDtypeStruct(x2.shape, x2.dtype),
        compiler_params=_params(("arbitrary",), V7X_VMEM_LIMIT_BYTES),
        name="post",
    )(*rows, *consts)


def _rotary_tables(S):
    pos = jnp.arange(S, dtype=F32)
    nsa_inv = 1.0 / (ROPE_THETA ** (jnp.arange(0, ROPE_DIM, 2, dtype=F32) / ROPE_DIM))
    ang = pos[:, None] * nsa_inv[None, :]
    c, s = jnp.cos(ang), jnp.sin(ang)
    half = ROPE_DIM // 2
    rest = NSA_DH - ROPE_DIM
    one, zero, zh = jnp.ones((S, rest), F32), jnp.zeros((S, rest), F32), jnp.zeros((S, half), F32)
    per_head = lambda parts: jnp.tile(jnp.concatenate(parts, axis=1), (1, LANE // NSA_DH))
    cn = per_head([c, c, one])
    san = per_head([zh, s, zero])
    sbn = per_head([-s, zh, zero])
    ret_inv = 1.0 / (RET_ROPE_BASE ** jnp.linspace(0.0, 1.0, RET_DK // 2, dtype=F32))
    rang = pos[:, None] * ret_inv[None, :]
    rc, rs = jnp.cos(rang), jnp.sin(rang)
    cr = jnp.concatenate([rc, rc], axis=1)
    sr = jnp.concatenate([-rs, rs], axis=1)
    return cn, san, sbn, cr, sr


def kernel(x, norm_pre, w_in, b_nsa_gate, cmp_pe_k, cmp_w1_k, cmp_w2_k, cmp_pe_v, cmp_w1_v,
           cmp_w2_v, w_nsa_o, w_ret_o, w_out, norm_post):
    B, S, _ = x.shape
    assert B == 1 and S % (2 * TK) == 0 and S >= WIN_KEYS
    assert S % (SEL_BLOCK * 8 * N_SIZE_CLASSES) == 0 and (S // TQ) % N_SIZE_CLASSES == 0
    x2 = x.reshape(S, D_MODEL)

    w_bf = w_in.astype(BF16)
    gate_cols = lambda a: jnp.pad(
        a.reshape(a.shape[:-1] + (NSA_GROUPS, NSA_HPG * 3)),
        [(0, 0)] * (a.ndim - 1) + [(0, 0), (0, GATE_ROWS - NSA_HPG * 3)]).reshape(a.shape[:-1] + (-1,))
    segs, off = [], 0
    for sz, pad in zip(PROJ_SIZES, _PAD_SIZES):
        seg = w_bf[:, off:off + sz]
        if sz == NSA_HEADS * 3:
            seg = gate_cols(seg)
        segs.append(jnp.pad(seg, ((0, 0), (0, pad - seg.shape[1]))))
        off += sz
    w_hi = jnp.concatenate(segs, axis=1)
    w_lo = (w_in[:, :HI_COLS] - w_bf[:, :HI_COLS].astype(F32)).astype(BF16)
    bias = gate_cols(b_nsa_gate)
    bias = jnp.pad(bias, (0, LANE - bias.shape[0])).reshape(1, LANE)

    (qT, kc, vc, ks, vsT, kw, vwT, gT, zn, qr, kr, vr, zr, ma, mb) = _inproj(
        x2, norm_pre.reshape(1, D_MODEL), w_hi, w_lo, bias, _rotary_tables(S))

    kc4 = _compress(kc, cmp_pe_k, cmp_w1_k, cmp_w2_k, keys=True)
    vcT = _compress(vc, cmp_pe_v, cmp_w1_v, cmp_w2_v, keys=False)

    oa = _nsa(qT, gT, kc4, vcT, ks, vsT, kw, vwT)
    ob = _retention(qr, kr, vr)

    out = _post(x2, oa, zn, ob, zr, ma, mb, w_nsa_o.astype(BF16), w_ret_o.astype(BF16),
                w_out.astype(BF16), norm_post.reshape(1, D_MODEL))
    return out.reshape(B, S, D_MODEL)
```

```python
import functools
import math

import jax
import jax.numpy as jnp
from jax import lax
from jax.experimental import pallas as pl
from jax.experimental.pallas import tpu as pltpu

F32 = jnp.float32
BF16 = jnp.bfloat16

D_MODEL = 1024
NSA_HEADS = 8
NSA_GROUPS = 2
NSA_HPG = NSA_HEADS // NSA_GROUPS
NSA_DH = 64
CMP_LEN = 32
CMP_STRIDE = 16
CMP_HIDDEN = 256
SEL_BLOCK = 64
SEL_TOPN = 16
WINDOW = 512
ROPE_THETA = 500000.0
ROPE_DIM = NSA_DH // 4
NSA_SCALE = NSA_DH ** -0.5
RET_HEADS = 4
RET_DK = 128
RET_DV = 256
RET_ROPE_BASE = 10000.0
RET_SCALE = RET_DK ** -0.5
RMS_EPS = 1e-6
GN_EPS = 1e-6
PROJ_SIZES = (512, 128, 128, 128, 128, 128, 128, 24, 512, 512, 512, 1024, 1024, 1024, 1024)

LANE = 128
V7X_VMEM_LIMIT_BYTES = 56 * 1024 * 1024

ROW_TILE = 256
POST_ROW_TILE = 512
TQ = 128
TK = 512
BLK_PER_TILE = TK // SEL_BLOCK
WIN_KEYS = WINDOW + TQ
RET_CHUNK = 256
V_ROWS = 80
NEG = -1e30
LOG2E = math.log2(math.e)
K_LANES = LANE
GATE_ROWS = 16
SC_PAD = 8
N_SIZE_CLASSES = 4

_PAD_SIZES = tuple(128 if s == 24 else s for s in PROJ_SIZES)
_OFF = [0]
for _s in _PAD_SIZES:
    _OFF.append(_OFF[-1] + _s)
PROJ_PAD = _OFF[-1]


def _dot(a, b):
    return jnp.dot(a, b, preferred_element_type=F32)


def _split(a):
    hi = a.astype(BF16)
    lo = (a - hi.astype(F32)).astype(BF16)
    return hi, lo


def _dot3(a_hi, a_lo, b_hi, b_lo):
    return _dot(a_hi, b_hi) + _dot(a_lo, b_hi) + _dot(a_hi, b_lo)


def _params(sem, vmem_bytes):
    return pltpu.CompilerParams(dimension_semantics=sem, vmem_limit_bytes=vmem_bytes)


def _inproj_kernel(x_ref, g_ref, whi_ref, bias_ref, cn_ref, san_ref, sbn_ref,
                   cr_ref, sr_ref,
                   qT_ref, kc_ref, vc_ref, ks_ref, vsT_ref, kw_ref, vwT_ref, gT_ref,
                   zn_ref, qr_ref, kr_ref, vr_ref, zr_ref, ma_ref, mb_ref):
    tm = x_ref.shape[0]
    x = x_ref[...]
    ms = jnp.mean(x * x, axis=-1, keepdims=True)
    h = x * lax.rsqrt(ms + RMS_EPS) * g_ref[...]
    h_hi = h.astype(BF16)

    def mm(lo, hi):
        return _dot(h_hi, whi_ref[:, lo:hi])

    cn, san, sbn = cn_ref[...], san_ref[...], sbn_ref[...]
    cr, sr = cr_ref[...], sr_ref[...]

    def rot_nsa(p):
        return p * cn + pltpu.roll(p, 8, 1) * san + pltpu.roll(p, LANE - 8, 1) * sbn

    def rot_ret(p):
        return p * cr + pltpu.roll(p, LANE // 2, 1) * sr

    lane = lax.broadcasted_iota(jnp.int32, (tm, LANE), 1)
    first_group = lane < NSA_DH

    def key_rows(p, fill):
        swapped = pltpu.roll(p, NSA_DH, 1)
        return [jnp.where(first_group, pg, fill).astype(BF16) for pg in (p, swapped)]

    ones_row = jnp.where(lax.broadcasted_iota(jnp.int32, (V_ROWS - NSA_DH, tm), 0) == 0, 1.0, 0.0)

    def value_rows(p):
        pT = p.T
        return [jnp.concatenate([pT[g * NSA_DH:(g + 1) * NSA_DH, :], ones_row], axis=0).astype(BF16)
                for g in range(NSA_GROUPS)]

    p = mm(0, _OFF[2])
    for c in range(4):
        qT_ref[c * LANE:(c + 1) * LANE, :] = (rot_nsa(p[:, c * LANE:(c + 1) * LANE]) * NSA_SCALE).T
    kc_ref[...] = rot_nsa(p[:, 4 * LANE:5 * LANE])

    p = mm(_OFF[2], _OFF[8])
    vc_ref[...] = p[:, 0 * LANE:1 * LANE]
    pos = pl.program_id(0) * tm + lax.broadcasted_iota(jnp.int32, (tm, LANE), 0)
    blk_in_tile = jnp.right_shift(jnp.bitwise_and(pos, TK - 1), SEL_BLOCK.bit_length() - 1)
    onehot = jnp.where(lane - NSA_DH == blk_in_tile, 1.0, 0.0)
    ks = key_rows(rot_nsa(p[:, 1 * LANE:2 * LANE]), onehot)
    vs = value_rows(p[:, 2 * LANE:3 * LANE])
    kw = key_rows(rot_nsa(p[:, 3 * LANE:4 * LANE]), 0.0)
    vw = value_rows(p[:, 4 * LANE:5 * LANE])
    gT = jax.nn.sigmoid(p[:, 5 * LANE:6 * LANE] + bias_ref[...]).T
    for g in range(NSA_GROUPS):
        ks_ref[g], vsT_ref[g], kw_ref[g], vwT_ref[g] = ks[g], vs[g], kw[g], vw[g]
        gT_ref[g] = gT[g * GATE_ROWS:(g + 1) * GATE_ROWS, :]

    z = mm(_OFF[8], _OFF[9])
    zn_ref[...] = (z * jax.nn.sigmoid(z)).astype(BF16)

    p = mm(_OFF[9], _OFF[11])
    for c in range(4):
        qr_ref[:, c * LANE:(c + 1) * LANE] = rot_ret(p[:, c * LANE:(c + 1) * LANE]).astype(BF16)
    for c in range(4):
        kr_ref[:, c * LANE:(c + 1) * LANE] = (
            rot_ret(p[:, (4 + c) * LANE:(5 + c) * LANE]) * RET_SCALE).astype(BF16)

    vr_ref[...] = mm(_OFF[11], _OFF[12]).astype(BF16)
    z = mm(_OFF[12], _OFF[13])
    zr_ref[...] = (z * jax.nn.sigmoid(z)).astype(BF16)
    ma_ref[...] = jax.nn.sigmoid(mm(_OFF[13], _OFF[14])).astype(BF16)
    mb_ref[...] = jax.nn.sigmoid(mm(_OFF[14], _OFF[15])).astype(BF16)


def _inproj(x2, norm_pre, w_hi, bias, tabs):
    S = x2.shape[0]
    tm = ROW_TILE
    G = NSA_GROUPS
    row = lambda w: pl.BlockSpec((tm, w), lambda i: (i, 0))
    whole = lambda a: pl.BlockSpec(a.shape, lambda i: (0,) * a.ndim)
    col = lambda r: pl.BlockSpec((r, tm), lambda i: (0, i))
    g_row = lambda w: pl.BlockSpec((G, tm, w), lambda i: (0, i, 0))
    g_col = lambda r: pl.BlockSpec((G, r, tm), lambda i: (0, 0, i))
    sds = jax.ShapeDtypeStruct
    outs = [
        (col(NSA_HEADS * NSA_DH), sds((NSA_HEADS * NSA_DH, S), F32)),
        (row(LANE), sds((S, LANE), F32)),
        (row(LANE), sds((S, LANE), F32)),
        (g_row(K_LANES), sds((G, S, K_LANES), BF16)),
        (g_col(V_ROWS), sds((G, V_ROWS, S), BF16)),
        (g_row(K_LANES), sds((G, S, K_LANES), BF16)),
        (g_col(V_ROWS), sds((G, V_ROWS, S), BF16)),
        (g_col(GATE_ROWS), sds((G, GATE_ROWS, S), F32)),
        (row(512), sds((S, 512), BF16)),
        (row(512), sds((S, 512), BF16)), (row(512), sds((S, 512), BF16)),
        (row(1024), sds((S, 1024), BF16)), (row(1024), sds((S, 1024), BF16)),
        (row(1024), sds((S, 1024), BF16)), (row(1024), sds((S, 1024), BF16)),
    ]
    return pl.pallas_call(
        _inproj_kernel,
        grid=(S // tm,),
        in_specs=[row(D_MODEL), whole(norm_pre), whole(w_hi), whole(bias)]
        + [row(LANE)] * 5,
        out_specs=[o[0] for o in outs],
        out_shape=[o[1] for o in outs],
        compiler_params=_params(("arbitrary",), V7X_VMEM_LIMIT_BYTES),
        name="inproj",
    )(x2, norm_pre, w_hi, bias, *tabs)


def _compress_kernel(x_ref, pe_ref, w1x_ref, w1_ref, w2_ref, o_ref, *, keys):
    NC = x_ref.shape[0]
    prod = lambda a, b: _dot3(*_split(a), *_split(b))
    both = prod(x_ref[...], w1x_ref[...])
    nxt = pltpu.roll(both[:, CMP_HIDDEN:], NC - 1, 0)
    pe_term = prod(pe_ref[...], w1_ref[...])[0:1, :]
    hid = both[:, :CMP_HIDDEN] + nxt + pe_term
    act = hid * (0.5 * (1.0 + jnp.tanh(math.sqrt(2.0 / math.pi) * (hid + 0.044715 * (hid * hid * hid)))))
    out = prod(act, w2_ref[...])
    if keys:
        hi, lo = _split(out)
        o_ref[...] = jnp.concatenate([hi, hi, lo, lo], axis=1)
    else:
        ones_row = jnp.where(lax.broadcasted_iota(jnp.int32, (V_ROWS - NSA_DH, NC), 0) == 0, 1.0, 0.0)
        o_ref[...] = jnp.concatenate([out.T, ones_row], axis=0).astype(BF16)


def _compress(raw, pe, w1, w2, keys):
    S = raw.shape[0]
    NC = S // CMP_STRIDE
    G = NSA_GROUPS
    half = CMP_LEN // 2
    x = raw.reshape(NC, CMP_STRIDE * G * NSA_DH)
    w1r = w1.reshape(2, half, 1, NSA_DH, CMP_HIDDEN)
    sel = (jnp.arange(G)[:, None] == jnp.arange(G)[None, :]).astype(F32)
    w1x = w1r[None] * sel[:, None, None, :, None, None]
    w1x = w1x.reshape(G, 2, half * G * NSA_DH, CMP_HIDDEN).transpose(0, 2, 1, 3)
    w1x = w1x.reshape(G, half * G * NSA_DH, 2 * CMP_HIDDEN)
    pe8 = jnp.pad(pe.reshape(1, -1), ((0, 7), (0, 0)))
    if keys:
        out_spec = pl.BlockSpec((None, NC, 4 * NSA_DH), lambda g: (g, 0, 0))
        out_shape = jax.ShapeDtypeStruct((G, NC, 4 * NSA_DH), BF16)
    else:
        out_spec = pl.BlockSpec((None, V_ROWS, NC), lambda g: (g, 0, 0))
        out_shape = jax.ShapeDtypeStruct((G, V_ROWS, NC), BF16)
    whole = lambda a: pl.BlockSpec(a.shape, lambda g: (0,) * a.ndim)
    return pl.pallas_call(
        functools.partial(_compress_kernel, keys=keys),
        grid=(G,),
        in_specs=[whole(x), whole(pe8),
                  pl.BlockSpec((None,) + w1x.shape[1:], lambda g: (g, 0, 0)),
                  whole(w1), whole(w2)],
        out_specs=out_spec,
        out_shape=out_shape,
        compiler_params=_params(("arbitrary",), V7X_VMEM_LIMIT_BYTES),
        name="compress_k" if keys else "compress_v",
    )(x, pe8, w1x, w1, w2)


def _nsa_kernel(qT_ref, gT_ref, kc4_ref, vcT_ref, ks_ref, vsT_ref, kw_ref, vwT_ref, o_ref,
                sb_ref, sa_ref, sbuf_ref, sc_ref, oc_ref, m_ref, acc_ref):
    i = pl.program_id(1)
    NC = kc4_ref.shape[0]
    NB = sb_ref.shape[0]
    n_q = pl.num_programs(1)
    heads = range(NSA_HPG)
    lanes = lambda a, h: a[:, h * TQ:(h + 1) * TQ]
    colmax = lambda a: jnp.max(a, axis=0, keepdims=True)
    colsum = lambda a: jnp.sum(a, axis=0, keepdims=True)

    qT = qT_ref[...] * LOG2E
    Q = jnp.concatenate([qT[h * NSA_DH:(h + 1) * NSA_DH, :] for h in heads], axis=1)
    q_hi, q_lo = _split(Q)
    q4 = jnp.concatenate([q_hi, q_lo, q_hi, q_lo], axis=0)
    t = i * TQ + lax.broadcasted_iota(jnp.int32, (1, TQ), 1)
    cur = jnp.right_shift(t, SEL_BLOCK.bit_length() - 1)

    def compressed_and_select(ncp, nbp):
        s = _dot(kc4_ref[0:ncp, :], q4)
        c_end = lax.broadcasted_iota(jnp.int32, (ncp, TQ), 0) * CMP_STRIDE + (CMP_LEN - 1)
        mask_c = c_end <= t
        p_heads, inv = [], []
        for h in heads:
            sm = jnp.where(mask_c, lanes(s, h), -jnp.inf)
            m = colmax(sm)
            m = jnp.where(m == -jnp.inf, 0.0, m)
            p = jnp.exp2(sm - m)
            inv.append(1.0 / jnp.maximum(colsum(p), jnp.finfo(F32).tiny))
            sc_ref[h, 0:SC_PAD, :] = jnp.zeros((SC_PAD, TQ), F32)
            sc_ref[h, SC_PAD:SC_PAD + ncp, :] = p
            p_heads.append(p.astype(BF16))
        oc_ref[...] = (_dot(vcT_ref[:, 0:ncp], jnp.concatenate(p_heads, axis=1))
                       * jnp.concatenate(inv, axis=1))
        per_sel = SEL_BLOCK // CMP_STRIDE
        imp = jnp.zeros((nbp, TQ), F32)
        for h in heads:
            tot = sc_ref[h, pl.ds(SC_PAD - 1, nbp, stride=per_sel), :]
            for r in range(per_sel):
                tot = tot + sc_ref[h, pl.ds(SC_PAD + r, nbp, stride=per_sel), :]
            imp = imp + tot * inv[h]
        blk = lax.broadcasted_iota(jnp.int32, (nbp, TQ), 0)
        blk_f = blk.astype(F32)
        valid = blk <= cur
        forced = (blk == 0) | (blk == cur) | (blk == cur - 1)
        free = valid & jnp.logical_not(forced)
        score = jnp.where(free, imp, -jnp.inf)
        for _ in range(min(SEL_TOPN, NB) - 3):
            mx = colmax(score)
            idx = jnp.min(jnp.where(score == mx, blk_f, float(NB)), axis=0, keepdims=True)
            score = jnp.where(blk_f == idx, -jnp.inf, score)
        picked = valid & (forced | (score == -jnp.inf))
        sb_ref[0:nbp, :] = jnp.where(picked, 0.0, NEG)
        if nbp < NB:
            sb_ref[nbp:NB, :] = jnp.full((NB - nbp, TQ), NEG, F32)

    size_class = lax.div(i * N_SIZE_CLASSES, n_q)
    for k in range(N_SIZE_CLASSES):
        pl.when(size_class == k)(functools.partial(
            compressed_and_select, NC * (k + 1) // N_SIZE_CLASSES, NB * (k + 1) // N_SIZE_CLASSES))

    pad_rows = jnp.zeros((K_LANES - NSA_DH - 2 * BLK_PER_TILE, NSA_HPG * TQ), BF16)

    def scores(j):
        k0 = pl.multiple_of(j * TK, TK)
        sbt = sb_ref[pl.ds(pl.multiple_of(j * BLK_PER_TILE, BLK_PER_TILE), BLK_PER_TILE), :]
        rows = jnp.concatenate([sbt, jnp.zeros_like(sbt)], axis=0)
        rows = jnp.concatenate([rows] * NSA_HPG, axis=1).astype(BF16)
        w = jnp.concatenate([q_hi, rows, pad_rows], axis=0)
        return _dot(ks_ref[pl.ds(k0, TK), :], w)

    def attend(j, s, carry, causal):
        m, acc = carry
        k0 = pl.multiple_of(j * TK, TK)
        if causal:
            keep = k0 + lax.broadcasted_iota(jnp.int32, (TK, TQ), 0) <= t
            s = jnp.concatenate([jnp.where(keep, lanes(s, h), NEG) for h in heads], axis=1)
        m_new = jnp.maximum(m, colmax(s))
        alpha = jnp.exp2(m - m_new)
        p = jnp.exp2(s - m_new).astype(BF16)
        acc = acc * alpha + _dot(vsT_ref[:, pl.ds(k0, TK)], p)
        return m_new, acc

    jd = lax.div(i, TK // TQ)
    n_pairs = lax.div(jd, 2)
    sa_ref[...] = scores(0)

    def pair(pp, carry):
        sbuf_ref[...] = scores(2 * pp + 1)
        carry = attend(2 * pp, sa_ref[...], carry, False)
        sa_ref[...] = scores(2 * pp + 2)
        return attend(2 * pp + 1, sbuf_ref[...], carry, False)

    carry = (jnp.full((1, NSA_HPG * TQ), NEG, F32), jnp.zeros((V_ROWS, NSA_HPG * TQ), F32))
    quad = lambda qq, c: pair(2 * qq + 1, pair(2 * qq, c))
    n_octs, n_quads = lax.div(n_pairs, 4), lax.div(n_pairs, 2)
    carry = lax.fori_loop(0, n_octs, lambda oo, c: quad(2 * oo + 1, quad(2 * oo, c)), carry)
    carry = lax.fori_loop(2 * n_octs, n_quads, quad, carry)
    carry = lax.fori_loop(2 * n_quads, n_pairs, pair, carry)
    sbuf_ref[...] = scores(2 * n_pairs + 1)

    w0 = pl.multiple_of(jnp.maximum(i * TQ - WINDOW, 0), TQ)
    q_pad = jnp.concatenate([q_hi, jnp.zeros((K_LANES - NSA_DH, NSA_HPG * TQ), BF16)], axis=0)
    s = _dot(kw_ref[pl.ds(w0, WIN_KEYS), :], q_pad)
    key = w0 + lax.broadcasted_iota(jnp.int32, (WIN_KEYS, TQ), 0)
    mask_w = (key <= t) & (key > t - WINDOW)
    p_heads = []
    for h in heads:
        sm = jnp.where(mask_w, lanes(s, h), NEG)
        p_heads.append(jnp.exp2(sm - colmax(sm)).astype(BF16))
    accw = _dot(vwT_ref[:, pl.ds(w0, WIN_KEYS)], jnp.concatenate(p_heads, axis=1))
    owT = accw[0:NSA_DH, :] * (1.0 / accw[NSA_DH:NSA_DH + 1, :])

    m_ref[...], acc_ref[...] = attend(2 * n_pairs, sa_ref[...], carry, True)

    @pl.when(jd != 2 * n_pairs)
    def _():
        m_ref[...], acc_ref[...] = attend(
            2 * n_pairs + 1, sbuf_ref[...], (m_ref[...], acc_ref[...]), True)

    acc = acc_ref[...]
    osT = acc[0:NSA_DH, :] * (1.0 / acc[NSA_DH:NSA_DH + 1, :])

    gT = gT_ref[...]
    ocT = oc_ref[...]
    outs = []
    for h in heads:
        g_c, g_s, g_w = (gT[3 * h + b:3 * h + b + 1, :] for b in range(3))
        outs.append(g_c * lanes(ocT, h)[0:NSA_DH, :] + g_s * lanes(osT, h) + g_w * lanes(owT, h))
    o_ref[...] = jnp.concatenate(outs, axis=0).T


def _nsa(qT, gT, kc4, vcT, ks, vsT, kw, vwT):
    G, S = ks.shape[0], ks.shape[1]
    NC, NB = kc4.shape[1], S // SEL_BLOCK
    per_g = lambda a: pl.BlockSpec((None,) + a.shape[1:], lambda g, i: (g,) + (0,) * (a.ndim - 1))
    wide = NSA_HPG * TQ
    return pl.pallas_call(
        _nsa_kernel,
        grid=(G, S // TQ),
        in_specs=[
            pl.BlockSpec((NSA_HPG * NSA_DH, TQ), lambda g, i: (g, i)),
            pl.BlockSpec((None, GATE_ROWS, TQ), lambda g, i: (g, 0, i)),
            per_g(kc4), per_g(vcT), per_g(ks), per_g(vsT), per_g(kw), per_g(vwT),
        ],
        out_specs=pl.BlockSpec((TQ, NSA_HPG * NSA_DH), lambda g, i: (i, g)),
        out_shape=jax.ShapeDtypeStruct((S, NSA_HEADS * NSA_DH), F32),
        scratch_shapes=[pltpu.VMEM((NB, TQ), F32),
                        pltpu.VMEM((TK, wide), F32),
                        pltpu.VMEM((TK, wide), F32),
                        pltpu.VMEM((NSA_HPG, SC_PAD + NC, TQ), F32),
                        pltpu.VMEM((V_ROWS, wide), F32),
                        pltpu.VMEM((1, wide), F32),
                        pltpu.VMEM((V_ROWS, wide), F32)],
        compiler_params=_params(("arbitrary", "arbitrary"), V7X_VMEM_LIMIT_BYTES),
        name="nsa",
    )(qT, gT, kc4, vcT, ks, vsT, kw, vwT)


def _ret_kernel(q_ref, k_ref, v_ref, o_ref, r_ref, dm_ref, qd_ref, kd_ref):
    C = RET_CHUNK
    log_g = [math.log(1.0 - 2.0 ** (-5.0 - h)) for h in range(RET_HEADS)]

    @pl.when(pl.program_id(0) == 0)
    def _():
        r_ref[...] = jnp.zeros(r_ref.shape, F32)
        diff = (lax.broadcasted_iota(jnp.int32, (C, C), 0)
                - lax.broadcasted_iota(jnp.int32, (C, C), 1)).astype(F32)
        n = lax.broadcasted_iota(jnp.int32, (C, RET_DK), 0).astype(F32)
        for h in range(RET_HEADS):
            dm_ref[h] = jnp.where(diff >= 0.0, jnp.exp(jnp.maximum(diff, 0.0) * log_g[h]), 0.0)
            qd_ref[h] = jnp.exp((n + 1.0) * log_g[h])
            kd_ref[h] = jnp.exp((C - 1.0 - n) * log_g[h])

    for h in range(RET_HEADS):
        q = q_ref[:, h * RET_DK:(h + 1) * RET_DK]
        k = k_ref[:, h * RET_DK:(h + 1) * RET_DK]
        v = v_ref[:, h * RET_DV:(h + 1) * RET_DV]
        att = lax.dot_general(q, k, (((1,), (1,)), ((), ())), preferred_element_type=F32)
        o = _dot((att * dm_ref[h]).astype(BF16), v)
        r = r_ref[h]
        r_hi, r_lo = _split(r)
        qd = (q.astype(F32) * qd_ref[h]).astype(BF16)
        o = o + _dot(qd, r_hi) + _dot(qd, r_lo)
        kd = (k.astype(F32) * kd_ref[h]).astype(BF16)
        r_ref[h] = math.exp(C * log_g[h]) * r + lax.dot_general(
            kd, v, (((0,), (0,)), ((), ())), preferred_element_type=F32)
        mu = jnp.mean(o, axis=-1, keepdims=True)
        d = o - mu
        var = jnp.mean(d * d, axis=-1, keepdims=True)
        o_ref[:, h * RET_DV:(h + 1) * RET_DV] = (d * lax.rsqrt(var + GN_EPS)).astype(BF16)


def _retention(qr, kr, vr):
    S = qr.shape[0]
    C = RET_CHUNK
    row = lambda w: pl.BlockSpec((C, w), lambda n: (n, 0))
    return pl.pallas_call(
        _ret_kernel,
        grid=(S // C,),
        in_specs=[row(RET_HEADS * RET_DK), row(RET_HEADS * RET_DK), row(RET_HEADS * RET_DV)],
        out_specs=row(RET_HEADS * RET_DV),
        out_shape=jax.ShapeDtypeStruct((S, RET_HEADS * RET_DV), BF16),
        scratch_shapes=[
            pltpu.VMEM((RET_HEADS, RET_DK, RET_DV), F32),
            pltpu.VMEM((RET_HEADS, C, C), F32),
            pltpu.VMEM((RET_HEADS, C, RET_DK), F32),
            pltpu.VMEM((RET_HEADS, C, RET_DK), F32),
        ],
        compiler_params=_params(("arbitrary",), V7X_VMEM_LIMIT_BYTES),
        name="retention",
    )(qr, kr, vr)


def _post_kernel(x_ref, oa_ref, zn_ref, ob_ref, zr_ref, ma_ref, mb_ref,
                 wa_ref, wb_ref, wo_ref, g_ref, out_ref):
    f32 = lambda r: r[...].astype(F32)
    ya = _dot((oa_ref[...] * f32(zn_ref)).astype(BF16), wa_ref[...])
    yb = _dot((f32(ob_ref) * f32(zr_ref)).astype(BF16), wb_ref[...])
    merged = f32(ma_ref) * ya + f32(mb_ref) * yb
    y = _dot(merged.astype(BF16), wo_ref[...])
    ms = jnp.mean(y * y, axis=-1, keepdims=True)
    out_ref[...] = x_ref[...] + y * lax.rsqrt(ms + RMS_EPS) * g_ref[...]


def _post(x2, oa, zn, ob, zr, ma, mb, wa, wb, wo, g_post):
    S = x2.shape[0]
    tm = POST_ROW_TILE
    row = lambda a: pl.BlockSpec((tm, a.shape[1]), lambda i: (i, 0))
    whole = lambda a: pl.BlockSpec(a.shape, lambda i: (0, 0))
    rows = (x2, oa, zn, ob, zr, ma, mb)
    consts = (wa, wb, wo, g_post)
    return pl.pallas_call(
        _post_kernel,
        grid=(S // tm,),
        in_specs=[row(a) for a in rows] + [whole(a) for a in consts],
        out_specs=row(x2),
        out_shape=jax.ShapeDtypeStruct(x2.shape, x2.dtype),
        compiler_params=_params(("arbitrary",), V7X_VMEM_LIMIT_BYTES),
        name="post",
    )(*rows, *consts)


def _rotary_tables(S):
    pos = jnp.arange(S, dtype=F32)
    nsa_inv = 1.0 / (ROPE_THETA ** (jnp.arange(0, ROPE_DIM, 2, dtype=F32) / ROPE_DIM))
    ang = pos[:, None] * nsa_inv[None, :]
    c, s = jnp.cos(ang), jnp.sin(ang)
    half = ROPE_DIM // 2
    rest = NSA_DH - ROPE_DIM
    one, zero, zh = jnp.ones((S, rest), F32), jnp.zeros((S, rest), F32), jnp.zeros((S, half), F32)
    per_head = lambda parts: jnp.tile(jnp.concatenate(parts, axis=1), (1, LANE // NSA_DH))
    cn = per_head([c, c, one])
    san = per_head([zh, s, zero])
    sbn = per_head([-s, zh, zero])
    ret_inv = 1.0 / (RET_ROPE_BASE ** jnp.linspace(0.0, 1.0, RET_DK // 2, dtype=F32))
    rang = pos[:, None] * ret_inv[None, :]
    rc, rs = jnp.cos(rang), jnp.sin(rang)
    cr = jnp.concatenate([rc, rc], axis=1)
    sr = jnp.concatenate([-rs, rs], axis=1)
    return cn, san, sbn, cr, sr


def kernel(x, norm_pre, w_in, b_nsa_gate, cmp_pe_k, cmp_w1_k, cmp_w2_k, cmp_pe_v, cmp_w1_v,
           cmp_w2_v, w_nsa_o, w_ret_o, w_out, norm_post):
    B, S, _ = x.shape
    assert B == 1 and S % (2 * TK) == 0 and S >= WIN_KEYS
    assert S % (SEL_BLOCK * 8 * N_SIZE_CLASSES) == 0 and (S // TQ) % N_SIZE_CLASSES == 0
    x2 = x.reshape(S, D_MODEL)

    w_bf = w_in.astype(BF16)
    gate_cols = lambda a: jnp.pad(
        a.reshape(a.shape[:-1] + (NSA_GROUPS, NSA_HPG * 3)),
        [(0, 0)] * (a.ndim - 1) + [(0, 0), (0, GATE_ROWS - NSA_HPG * 3)]).reshape(a.shape[:-1] + (-1,))
    segs, off = [], 0
    for sz, pad in zip(PROJ_SIZES, _PAD_SIZES):
        seg = w_bf[:, off:off + sz]
        if sz == NSA_HEADS * 3:
            seg = gate_cols(seg)
        segs.append(jnp.pad(seg, ((0, 0), (0, pad - seg.shape[1]))))
        off += sz
    w_hi = jnp.concatenate(segs, axis=1)
    bias = gate_cols(b_nsa_gate)
    bias = jnp.pad(bias, (0, LANE - bias.shape[0])).reshape(1, LANE)

    (qT, kc, vc, ks, vsT, kw, vwT, gT, zn, qr, kr, vr, zr, ma, mb) = _inproj(
        x2, norm_pre.reshape(1, D_MODEL), w_hi, bias, _rotary_tables(S))

    kc4 = _compress(kc, cmp_pe_k, cmp_w1_k, cmp_w2_k, keys=True)
    vcT = _compress(vc, cmp_pe_v, cmp_w1_v, cmp_w2_v, keys=False)

    oa = _nsa(qT, gT, kc4, vcT, ks, vsT, kw, vwT)
    ob = _retention(qr, kr, vr)

    out = _post(x2, oa, zn, ob, zr, ma, mb, w_nsa_o.astype(BF16), w_ret_o.astype(BF16),
                w_out.astype(BF16), norm_post.reshape(1, D_MODEL))
    return out.reshape(B, S, D_MODEL)
```

```python
import functools
import math

import jax
import jax.numpy as jnp
from jax import lax
from jax.experimental import pallas as pl
from jax.experimental.pallas import tpu as pltpu

F32 = jnp.float32
BF16 = jnp.bfloat16

D_MODEL = 1024
NSA_HEADS = 8
NSA_GROUPS = 2
NSA_HPG = NSA_HEADS // NSA_GROUPS
NSA_DH = 64
CMP_LEN = 32
CMP_STRIDE = 16
CMP_HIDDEN = 256
SEL_BLOCK = 64
SEL_TOPN = 16
WINDOW = 512
ROPE_THETA = 500000.0
ROPE_DIM = NSA_DH // 4
NSA_SCALE = NSA_DH ** -0.5
RET_HEADS = 4
RET_DK = 128
RET_DV = 256
RET_ROPE_BASE = 10000.0
RET_SCALE = RET_DK ** -0.5
RMS_EPS = 1e-6
GN_EPS = 1e-6
PROJ_SIZES = (512, 128, 128, 128, 128, 128, 128, 24, 512, 512, 512, 1024, 1024, 1024, 1024)

LANE = 128
V7X_VMEM_LIMIT_BYTES = 56 * 1024 * 1024

ROW_TILE = 256
POST_ROW_TILE = 512
TQ = 128
TK = 512
BLK_PER_TILE = TK // SEL_BLOCK
WIN_KEYS = WINDOW + TQ
RET_CHUNK = 256
V_ROWS = 80
NEG = -1e30
LOG2E = math.log2(math.e)
K_LANES = LANE
GATE_ROWS = 16
SC_PAD = 8
N_SIZE_CLASSES = 8

_PAD_SIZES = tuple(128 if s == 24 else s for s in PROJ_SIZES)
_OFF = [0]
for _s in _PAD_SIZES:
    _OFF.append(_OFF[-1] + _s)
PROJ_PAD = _OFF[-1]


def _dot(a, b):
    return jnp.dot(a, b, preferred_element_type=F32)


def _split(a):
    hi = a.astype(BF16)
    lo = (a - hi.astype(F32)).astype(BF16)
    return hi, lo


def _dot3(a_hi, a_lo, b_hi, b_lo):
    return _dot(a_hi, b_hi) + _dot(a_lo, b_hi) + _dot(a_hi, b_lo)


def _params(sem, vmem_bytes):
    return pltpu.CompilerParams(dimension_semantics=sem, vmem_limit_bytes=vmem_bytes)


def _inproj_kernel(x_ref, g_ref, whi_ref, bias_ref, freq_ref,
                   qT_ref, kc_ref, vc_ref, ks_ref, vsT_ref, kw_ref, vwT_ref, gT_ref,
                   zn_ref, qr_ref, kr_ref, vr_ref, zr_ref, ma_ref, mb_ref):
    tm = x_ref.shape[0]
    x = x_ref[...]
    ms = jnp.mean(x * x, axis=-1, keepdims=True)
    h = x * lax.rsqrt(ms + RMS_EPS) * g_ref[...]
    h_hi = h.astype(BF16)

    def mm(lo, hi):
        return _dot(h_hi, whi_ref[:, lo:hi])

    lane = lax.broadcasted_iota(jnp.int32, (tm, LANE), 1)
    row = pl.program_id(0) * tm + lax.broadcasted_iota(jnp.int32, (tm, LANE), 0)
    first_group = lane < NSA_DH
    pos = row.astype(F32)

    half = ROPE_DIM // 2
    dim = jnp.bitwise_and(lane, NSA_DH - 1)
    ang = pos * freq_ref[0:1, :]
    cn, sn = jnp.cos(ang), jnp.sin(ang)
    san = jnp.where((dim >= half) & (dim < ROPE_DIM), sn, 0.0)
    sbn = jnp.where(dim < half, -sn, 0.0)
    ang = pos * freq_ref[1:2, :]
    cr, sr = jnp.cos(ang), jnp.sin(ang)
    sr = jnp.where(lane < LANE // 2, -sr, sr)

    def rot_nsa(p):
        return p * cn + pltpu.roll(p, half, 1) * san + pltpu.roll(p, LANE - half, 1) * sbn

    def rot_ret(p):
        return p * cr + pltpu.roll(p, LANE // 2, 1) * sr

    def key_rows(p, fill):
        swapped = pltpu.roll(p, NSA_DH, 1)
        return [jnp.where(first_group, pg, fill).astype(BF16) for pg in (p, swapped)]

    ones_row = jnp.where(lax.broadcasted_iota(jnp.int32, (V_ROWS - NSA_DH, tm), 0) == 0, 1.0, 0.0)

    def value_rows(p):
        pT = p.T
        return [jnp.concatenate([pT[g * NSA_DH:(g + 1) * NSA_DH, :], ones_row], axis=0).astype(BF16)
                for g in range(NSA_GROUPS)]

    p = mm(0, _OFF[2])
    for c in range(4):
        qT_ref[c * LANE:(c + 1) * LANE, :] = (rot_nsa(p[:, c * LANE:(c + 1) * LANE]) * NSA_SCALE).T
    kc_ref[...] = rot_nsa(p[:, 4 * LANE:5 * LANE])

    p = mm(_OFF[2], _OFF[8])
    vc_ref[...] = p[:, 0 * LANE:1 * LANE]
    blk_in_tile = jnp.right_shift(jnp.bitwise_and(row, TK - 1), SEL_BLOCK.bit_length() - 1)
    onehot = jnp.where(lane - NSA_DH == blk_in_tile, 1.0, 0.0)
    ks = key_rows(rot_nsa(p[:, 1 * LANE:2 * LANE]), onehot)
    vs = value_rows(p[:, 2 * LANE:3 * LANE])
    kw = key_rows(rot_nsa(p[:, 3 * LANE:4 * LANE]), 0.0)
    vw = value_rows(p[:, 4 * LANE:5 * LANE])
    gT = jax.nn.sigmoid(p[:, 5 * LANE:6 * LANE] + bias_ref[...]).T
    for g in range(NSA_GROUPS):
        ks_ref[g], vsT_ref[g], kw_ref[g], vwT_ref[g] = ks[g], vs[g], kw[g], vw[g]
        gT_ref[g] = gT[g * GATE_ROWS:(g + 1) * GATE_ROWS, :]

    z = mm(_OFF[8], _OFF[9])
    zn_ref[...] = (z * jax.nn.sigmoid(z)).astype(BF16)

    p = mm(_OFF[9], _OFF[11])
    for c in range(4):
        qr_ref[:, c * LANE:(c + 1) * LANE] = rot_ret(p[:, c * LANE:(c + 1) * LANE]).astype(BF16)
    for c in range(4):
        kr_ref[:, c * LANE:(c + 1) * LANE] = (
            rot_ret(p[:, (4 + c) * LANE:(5 + c) * LANE]) * RET_SCALE).astype(BF16)

    vr_ref[...] = mm(_OFF[11], _OFF[12]).astype(BF16)
    z = mm(_OFF[12], _OFF[13])
    zr_ref[...] = (z * jax.nn.sigmoid(z)).astype(BF16)
    ma_ref[...] = jax.nn.sigmoid(mm(_OFF[13], _OFF[14])).astype(BF16)
    mb_ref[...] = jax.nn.sigmoid(mm(_OFF[14], _OFF[15])).astype(BF16)


def _inproj(x2, norm_pre, w_hi, bias, freqs):
    S = x2.shape[0]
    tm = ROW_TILE
    G = NSA_GROUPS
    row = lambda w: pl.BlockSpec((tm, w), lambda i: (i, 0))
    whole = lambda a: pl.BlockSpec(a.shape, lambda i: (0,) * a.ndim)
    col = lambda r: pl.BlockSpec((r, tm), lambda i: (0, i))
    g_row = lambda w: pl.BlockSpec((G, tm, w), lambda i: (0, i, 0))
    g_col = lambda r: pl.BlockSpec((G, r, tm), lambda i: (0, 0, i))
    sds = jax.ShapeDtypeStruct
    outs = [
        (col(NSA_HEADS * NSA_DH), sds((NSA_HEADS * NSA_DH, S), F32)),
        (row(LANE), sds((S, LANE), F32)),
        (row(LANE), sds((S, LANE), F32)),
        (g_row(K_LANES), sds((G, S, K_LANES), BF16)),
        (g_col(V_ROWS), sds((G, V_ROWS, S), BF16)),
        (g_row(K_LANES), sds((G, S, K_LANES), BF16)),
        (g_col(V_ROWS), sds((G, V_ROWS, S), BF16)),
        (g_col(GATE_ROWS), sds((G, GATE_ROWS, S), F32)),
        (row(512), sds((S, 512), BF16)),
        (row(512), sds((S, 512), BF16)), (row(512), sds((S, 512), BF16)),
        (row(1024), sds((S, 1024), BF16)), (row(1024), sds((S, 1024), BF16)),
        (row(1024), sds((S, 1024), BF16)), (row(1024), sds((S, 1024), BF16)),
    ]
    return pl.pallas_call(
        _inproj_kernel,
        grid=(S // tm,),
        in_specs=[row(D_MODEL), whole(norm_pre), whole(w_hi), whole(bias), whole(freqs)],
        out_specs=[o[0] for o in outs],
        out_shape=[o[1] for o in outs],
        compiler_params=_params(("arbitrary",), V7X_VMEM_LIMIT_BYTES),
        name="inproj",
    )(x2, norm_pre, w_hi, bias, freqs)


def _compress_kernel(x_ref, pe_ref, w1x_ref, w1_ref, w2_ref, o_ref, *, keys):
    NC = x_ref.shape[0]
    prod = lambda a, b: _dot3(*_split(a), *_split(b))
    both = prod(x_ref[...], w1x_ref[...])
    nxt = pltpu.roll(both[:, CMP_HIDDEN:], NC - 1, 0)
    pe_term = prod(pe_ref[...], w1_ref[...])[0:1, :]
    hid = both[:, :CMP_HIDDEN] + nxt + pe_term
    act = hid * (0.5 * (1.0 + jnp.tanh(math.sqrt(2.0 / math.pi) * (hid + 0.044715 * (hid * hid * hid)))))
    out = prod(act, w2_ref[...])
    if keys:
        hi, lo = _split(out)
        o_ref[...] = jnp.concatenate([hi, hi, lo, lo], axis=1)
    else:
        ones_row = jnp.where(lax.broadcasted_iota(jnp.int32, (V_ROWS - NSA_DH, NC), 0) == 0, 1.0, 0.0)
        o_ref[...] = jnp.concatenate([out.T, ones_row], axis=0).astype(BF16)


def _compress(raw, pe, w1, w2, keys):
    S = raw.shape[0]
    NC = S // CMP_STRIDE
    G = NSA_GROUPS
    half = CMP_LEN // 2
    x = raw.reshape(NC, CMP_STRIDE * G * NSA_DH)
    w1r = w1.reshape(2, half, 1, NSA_DH, CMP_HIDDEN)
    sel = (jnp.arange(G)[:, None] == jnp.arange(G)[None, :]).astype(F32)
    w1x = w1r[None] * sel[:, None, None, :, None, None]
    w1x = w1x.reshape(G, 2, half * G * NSA_DH, CMP_HIDDEN).transpose(0, 2, 1, 3)
    w1x = w1x.reshape(G, half * G * NSA_DH, 2 * CMP_HIDDEN)
    pe8 = jnp.pad(pe.reshape(1, -1), ((0, 7), (0, 0)))
    if keys:
        out_spec = pl.BlockSpec((None, NC, 4 * NSA_DH), lambda g: (g, 0, 0))
        out_shape = jax.ShapeDtypeStruct((G, NC, 4 * NSA_DH), BF16)
    else:
        out_spec = pl.BlockSpec((None, V_ROWS, NC), lambda g: (g, 0, 0))
        out_shape = jax.ShapeDtypeStruct((G, V_ROWS, NC), BF16)
    whole = lambda a: pl.BlockSpec(a.shape, lambda g: (0,) * a.ndim)
    return pl.pallas_call(
        functools.partial(_compress_kernel, keys=keys),
        grid=(G,),
        in_specs=[whole(x), whole(pe8),
                  pl.BlockSpec((None,) + w1x.shape[1:], lambda g: (g, 0, 0)),
                  whole(w1), whole(w2)],
        out_specs=out_spec,
        out_shape=out_shape,
        compiler_params=_params(("arbitrary",), V7X_VMEM_LIMIT_BYTES),
        name="compress_k" if keys else "compress_v",
    )(x, pe8, w1x, w1, w2)


def _nsa_kernel(qT_ref, gT_ref, kc4_ref, vcT_ref, ks_ref, vsT_ref, kw_ref, vwT_ref, o_ref,
                sb_ref, sa_ref, sbuf_ref, sc_ref, oc_ref, m_ref, acc_ref):
    i = pl.program_id(1)
    NC = kc4_ref.shape[0]
    NB = sb_ref.shape[0]
    n_q = pl.num_programs(1)
    heads = range(NSA_HPG)
    lanes = lambda a, h: a[:, h * TQ:(h + 1) * TQ]
    colmax = lambda a: jnp.max(a, axis=0, keepdims=True)
    colsum = lambda a: jnp.sum(a, axis=0, keepdims=True)

    qT = qT_ref[...] * LOG2E
    Q = jnp.concatenate([qT[h * NSA_DH:(h + 1) * NSA_DH, :] for h in heads], axis=1)
    q_hi, q_lo = _split(Q)
    q4 = jnp.concatenate([q_hi, q_lo, q_hi, q_lo], axis=0)
    t = i * TQ + lax.broadcasted_iota(jnp.int32, (1, TQ), 1)
    cur = jnp.right_shift(t, SEL_BLOCK.bit_length() - 1)

    def compressed_and_select(ncp, nbp):
        s = _dot(kc4_ref[0:ncp, :], q4)
        c_end = lax.broadcasted_iota(jnp.int32, (ncp, TQ), 0) * CMP_STRIDE + (CMP_LEN - 1)
        mask_c = c_end <= t
        p_heads, inv = [], []
        for h in heads:
            sm = jnp.where(mask_c, lanes(s, h), -jnp.inf)
            m = colmax(sm)
            m = jnp.where(m == -jnp.inf, 0.0, m)
            p = jnp.exp2(sm - m)
            inv.append(1.0 / jnp.maximum(colsum(p), jnp.finfo(F32).tiny))
            sc_ref[h, 0:SC_PAD, :] = jnp.zeros((SC_PAD, TQ), F32)
            sc_ref[h, SC_PAD:SC_PAD + ncp, :] = p
            p_heads.append(p.astype(BF16))
        oc_ref[...] = (_dot(vcT_ref[:, 0:ncp], jnp.concatenate(p_heads, axis=1))
                       * jnp.concatenate(inv, axis=1))
        per_sel = SEL_BLOCK // CMP_STRIDE
        imp = jnp.zeros((nbp, TQ), F32)
        for h in heads:
            tot = sc_ref[h, pl.ds(SC_PAD - 1, nbp, stride=per_sel), :]
            for r in range(per_sel):
                tot = tot + sc_ref[h, pl.ds(SC_PAD + r, nbp, stride=per_sel), :]
            imp = imp + tot * inv[h]
        blk = lax.broadcasted_iota(jnp.int32, (nbp, TQ), 0)
        blk_f = blk.astype(F32)
        valid = blk <= cur
        forced = (blk == 0) | (blk == cur) | (blk == cur - 1)
        free = valid & jnp.logical_not(forced)
        score = jnp.where(free, imp, -jnp.inf)
        for _ in range(min(SEL_TOPN, NB) - 3):
            mx = colmax(score)
            idx = jnp.min(jnp.where(score == mx, blk_f, float(NB)), axis=0, keepdims=True)
            score = jnp.where(blk_f == idx, -jnp.inf, score)
        picked = valid & (forced | (score == -jnp.inf))
        sb_ref[0:nbp, :] = jnp.where(picked, 0.0, NEG)
        if nbp < NB:
            sb_ref[nbp:NB, :] = jnp.full((NB - nbp, TQ), NEG, F32)

    size_class = lax.div(i * N_SIZE_CLASSES, n_q)
    for k in range(N_SIZE_CLASSES):
        pl.when(size_class == k)(functools.partial(
            compressed_and_select, NC * (k + 1) // N_SIZE_CLASSES, NB * (k + 1) // N_SIZE_CLASSES))

    pad_rows = jnp.zeros((K_LANES - NSA_DH - 2 * BLK_PER_TILE, NSA_HPG * TQ), BF16)

    def scores(j):
        k0 = pl.multiple_of(j * TK, TK)
        sbt = sb_ref[pl.ds(pl.multiple_of(j * BLK_PER_TILE, BLK_PER_TILE), BLK_PER_TILE), :]
        rows = jnp.concatenate([sbt, jnp.zeros_like(sbt)], axis=0)
        rows = jnp.concatenate([rows] * NSA_HPG, axis=1).astype(BF16)
        w = jnp.concatenate([q_hi, rows, pad_rows], axis=0)
        return _dot(ks_ref[pl.ds(k0, TK), :], w)

    def attend(j, s, carry, causal):
        m, acc = carry
        k0 = pl.multiple_of(j * TK, TK)
        if causal:
            keep = k0 + lax.broadcasted_iota(jnp.int32, (TK, TQ), 0) <= t
            s = jnp.concatenate([jnp.where(keep, lanes(s, h), NEG) for h in heads], axis=1)
        m_new = jnp.maximum(m, colmax(s))
        alpha = jnp.exp2(m - m_new)
        p = jnp.exp2(s - m_new).astype(BF16)
        acc = acc * alpha + _dot(vsT_ref[:, pl.ds(k0, TK)], p)
        return m_new, acc

    jd = lax.div(i, TK // TQ)
    n_pairs = lax.div(jd, 2)
    sa_ref[...] = scores(0)

    def pair(pp, carry):
        sbuf_ref[...] = scores(2 * pp + 1)
        carry = attend(2 * pp, sa_ref[...], carry, False)
        sa_ref[...] = scores(2 * pp + 2)
        return attend(2 * pp + 1, sbuf_ref[...], carry, False)

    carry = (jnp.full((1, NSA_HPG * TQ), NEG, F32), jnp.zeros((V_ROWS, NSA_HPG * TQ), F32))
    quad = lambda qq, c: pair(2 * qq + 1, pair(2 * qq, c))
    n_octs, n_quads = lax.div(n_pairs, 4), lax.div(n_pairs, 2)
    carry = lax.fori_loop(0, n_octs, lambda oo, c: quad(2 * oo + 1, quad(2 * oo, c)), carry)
    carry = lax.fori_loop(2 * n_octs, n_quads, quad, carry)
    carry = lax.fori_loop(2 * n_quads, n_pairs, pair, carry)
    sbuf_ref[...] = scores(2 * n_pairs + 1)

    w0 = pl.multiple_of(jnp.maximum(i * TQ - WINDOW, 0), TQ)
    q_pad = jnp.concatenate([q_hi, jnp.zeros((K_LANES - NSA_DH, NSA_HPG * TQ), BF16)], axis=0)
    s = _dot(kw_ref[pl.ds(w0, WIN_KEYS), :], q_pad)
    key = w0 + lax.broadcasted_iota(jnp.int32, (WIN_KEYS, TQ), 0)
    mask_w = (key <= t) & (key > t - WINDOW)
    p_heads = []
    for h in heads:
        sm = jnp.where(mask_w, lanes(s, h), NEG)
        p_heads.append(jnp.exp2(sm - colmax(sm)).astype(BF16))
    accw = _dot(vwT_ref[:, pl.ds(w0, WIN_KEYS)], jnp.concatenate(p_heads, axis=1))
    owT = accw[0:NSA_DH, :] * (1.0 / accw[NSA_DH:NSA_DH + 1, :])

    m_ref[...], acc_ref[...] = attend(2 * n_pairs, sa_ref[...], carry, True)

    @pl.when(jd != 2 * n_pairs)
    def _():
        m_ref[...], acc_ref[...] = attend(
            2 * n_pairs + 1, sbuf_ref[...], (m_ref[...], acc_ref[...]), True)

    acc = acc_ref[...]
    osT = acc[0:NSA_DH, :] * (1.0 / acc[NSA_DH:NSA_DH + 1, :])

    gT = gT_ref[...]
    ocT = oc_ref[...]
    outs = []
    for h in heads:
        g_c, g_s, g_w = (gT[3 * h + b:3 * h + b + 1, :] for b in range(3))
        outs.append(g_c * lanes(ocT, h)[0:NSA_DH, :] + g_s * lanes(osT, h) + g_w * lanes(owT, h))
    o_ref[...] = jnp.concatenate(outs, axis=0).T


def _nsa(qT, gT, kc4, vcT, ks, vsT, kw, vwT):
    G, S = ks.shape[0], ks.shape[1]
    NC, NB = kc4.shape[1], S // SEL_BLOCK
    per_g = lambda a: pl.BlockSpec((None,) + a.shape[1:], lambda g, i: (g,) + (0,) * (a.ndim - 1))
    wide = NSA_HPG * TQ
    return pl.pallas_call(
        _nsa_kernel,
        grid=(G, S // TQ),
        in_specs=[
            pl.BlockSpec((NSA_HPG * NSA_DH, TQ), lambda g, i: (g, i)),
            pl.BlockSpec((None, GATE_ROWS, TQ), lambda g, i: (g, 0, i)),
            per_g(kc4), per_g(vcT), per_g(ks), per_g(vsT), per_g(kw), per_g(vwT),
        ],
        out_specs=pl.BlockSpec((TQ, NSA_HPG * NSA_DH), lambda g, i: (i, g)),
        out_shape=jax.ShapeDtypeStruct((S, NSA_HEADS * NSA_DH), F32),
        scratch_shapes=[pltpu.VMEM((NB, TQ), F32),
                        pltpu.VMEM((TK, wide), F32),
                        pltpu.VMEM((TK, wide), F32),
                        pltpu.VMEM((NSA_HPG, SC_PAD + NC, TQ), F32),
                        pltpu.VMEM((V_ROWS, wide), F32),
                        pltpu.VMEM((1, wide), F32),
                        pltpu.VMEM((V_ROWS, wide), F32)],
        compiler_params=_params(("arbitrary", "arbitrary"), V7X_VMEM_LIMIT_BYTES),
        name="nsa",
    )(qT, gT, kc4, vcT, ks, vsT, kw, vwT)


def _ret_kernel(q_ref, k_ref, v_ref, o_ref, r_ref, dm_ref, qd_ref, kd_ref):
    C = RET_CHUNK
    log_g = [math.log(1.0 - 2.0 ** (-5.0 - h)) for h in range(RET_HEADS)]

    @pl.when(pl.program_id(0) == 0)
    def _():
        r_ref[...] = jnp.zeros(r_ref.shape, F32)
        diff = (lax.broadcasted_iota(jnp.int32, (C, C), 0)
                - lax.broadcasted_iota(jnp.int32, (C, C), 1)).astype(F32)
        n = lax.broadcasted_iota(jnp.int32, (C, RET_DK), 0).astype(F32)
        for h in range(RET_HEADS):
            dm_ref[h] = jnp.where(diff >= 0.0, jnp.exp(jnp.maximum(diff, 0.0) * log_g[h]), 0.0)
            qd_ref[h] = jnp.exp((n + 1.0) * log_g[h])
            kd_ref[h] = jnp.exp((C - 1.0 - n) * log_g[h])

    for h in range(RET_HEADS):
        q = q_ref[:, h * RET_DK:(h + 1) * RET_DK]
        k = k_ref[:, h * RET_DK:(h + 1) * RET_DK]
        v = v_ref[:, h * RET_DV:(h + 1) * RET_DV]
        att = lax.dot_general(q, k, (((1,), (1,)), ((), ())), preferred_element_type=F32)
        o = _dot((att * dm_ref[h]).astype(BF16), v)
        r = r_ref[h]
        qd = (q.astype(F32) * qd_ref[h]).astype(BF16)
        o = o + _dot(qd, r.astype(BF16))
        kd = (k.astype(F32) * kd_ref[h]).astype(BF16)
        r_ref[h] = math.exp(C * log_g[h]) * r + lax.dot_general(
            kd, v, (((0,), (0,)), ((), ())), preferred_element_type=F32)
        mu = jnp.mean(o, axis=-1, keepdims=True)
        d = o - mu
        var = jnp.mean(d * d, axis=-1, keepdims=True)
        o_ref[:, h * RET_DV:(h + 1) * RET_DV] = (d * lax.rsqrt(var + GN_EPS)).astype(BF16)


def _retention(qr, kr, vr):
    S = qr.shape[0]
    C = RET_CHUNK
    row = lambda w: pl.BlockSpec((C, w), lambda n: (n, 0))
    return pl.pallas_call(
        _ret_kernel,
        grid=(S // C,),
        in_specs=[row(RET_HEADS * RET_DK), row(RET_HEADS * RET_DK), row(RET_HEADS * RET_DV)],
        out_specs=row(RET_HEADS * RET_DV),
        out_shape=jax.ShapeDtypeStruct((S, RET_HEADS * RET_DV), BF16),
        scratch_shapes=[
            pltpu.VMEM((RET_HEADS, RET_DK, RET_DV), F32),
            pltpu.VMEM((RET_HEADS, C, C), F32),
            pltpu.VMEM((RET_HEADS, C, RET_DK), F32),
            pltpu.VMEM((RET_HEADS, C, RET_DK), F32),
        ],
        compiler_params=_params(("arbitrary",), V7X_VMEM_LIMIT_BYTES),
        name="retention",
    )(qr, kr, vr)


def _post_kernel(x_ref, oa_ref, zn_ref, ob_ref, zr_ref, ma_ref, mb_ref,
                 wa_ref, wb_ref, wo_ref, g_ref, out_ref):
    f32 = lambda r: r[...].astype(F32)
    ya = _dot((oa_ref[...] * f32(zn_ref)).astype(BF16), wa_ref[...])
    yb = _dot((f32(ob_ref) * f32(zr_ref)).astype(BF16), wb_ref[...])
    merged = f32(ma_ref) * ya + f32(mb_ref) * yb
    y = _dot(merged.astype(BF16), wo_ref[...])
    ms = jnp.mean(y * y, axis=-1, keepdims=True)
    out_ref[...] = x_ref[...] + y * lax.rsqrt(ms + RMS_EPS) * g_ref[...]


def _post(x2, oa, zn, ob, zr, ma, mb, wa, wb, wo, g_post):
    S = x2.shape[0]
    tm = POST_ROW_TILE
    row = lambda a: pl.BlockSpec((tm, a.shape[1]), lambda i: (i, 0))
    whole = lambda a: pl.BlockSpec(a.shape, lambda i: (0, 0))
    rows = (x2, oa, zn, ob, zr, ma, mb)
    consts = (wa, wb, wo, g_post)
    return pl.pallas_call(
        _post_kernel,
        grid=(S // tm,),
        in_specs=[row(a) for a in rows] + [whole(a) for a in consts],
        out_specs=row(x2),
        out_shape=jax.ShapeDtypeStruct(x2.shape, x2.dtype),
        compiler_params=_params(("arbitrary",), V7X_VMEM_LIMIT_BYTES),
        name="post",
    )(*rows, *consts)


def _rotary_frequencies():
    nsa_inv = 1.0 / (ROPE_THETA ** (jnp.arange(0, ROPE_DIM, 2, dtype=F32) / ROPE_DIM))
    head = jnp.concatenate([nsa_inv, nsa_inv, jnp.zeros((NSA_DH - ROPE_DIM,), F32)])
    ret_inv = 1.0 / (RET_ROPE_BASE ** jnp.linspace(0.0, 1.0, RET_DK // 2, dtype=F32))
    rows = jnp.stack([jnp.tile(head, LANE // NSA_DH), jnp.concatenate([ret_inv, ret_inv])])
    return jnp.pad(rows, ((0, 6), (0, 0)))


def kernel(x, norm_pre, w_in, b_nsa_gate, cmp_pe_k, cmp_w1_k, cmp_w2_k, cmp_pe_v, cmp_w1_v,
           cmp_w2_v, w_nsa_o, w_ret_o, w_out, norm_post):
    B, S, _ = x.shape
    assert B == 1 and S % (2 * TK) == 0 and S >= WIN_KEYS
    assert S % (SEL_BLOCK * 8 * N_SIZE_CLASSES) == 0 and (S // TQ) % N_SIZE_CLASSES == 0
    x2 = x.reshape(S, D_MODEL)

    w_bf = w_in.astype(BF16)
    gate_cols = lambda a: jnp.pad(
        a.reshape(a.shape[:-1] + (NSA_GROUPS, NSA_HPG * 3)),
        [(0, 0)] * (a.ndim - 1) + [(0, 0), (0, GATE_ROWS - NSA_HPG * 3)]).reshape(a.shape[:-1] + (-1,))
    segs, off = [], 0
    for sz, pad in zip(PROJ_SIZES, _PAD_SIZES):
        seg = w_bf[:, off:off + sz]
        if sz == NSA_HEADS * 3:
            seg = gate_cols(seg)
        segs.append(jnp.pad(seg, ((0, 0), (0, pad - seg.shape[1]))))
        off += sz
    w_hi = jnp.concatenate(segs, axis=1)
    bias = gate_cols(b_nsa_gate)
    bias = jnp.pad(bias, (0, LANE - bias.shape[0])).reshape(1, LANE)

    (qT, kc, vc, ks, vsT, kw, vwT, gT, zn, qr, kr, vr, zr, ma, mb) = _inproj(
        x2, norm_pre.reshape(1, D_MODEL), w_hi, bias, _rotary_frequencies())

    kc4 = _compress(kc, cmp_pe_k, cmp_w1_k, cmp_w2_k, keys=True)
    vcT = _compress(vc, cmp_pe_v, cmp_w1_v, cmp_w2_v, keys=False)

    oa = _nsa(qT, gT, kc4, vcT, ks, vsT, kw, vwT)
    ob = _retention(qr, kr, vr)

    out = _post(x2, oa, zn, ob, zr, ma, mb, w_nsa_o.astype(BF16), w_ret_o.astype(BF16),
                w_out.astype(BF16), norm_post.reshape(1, D_MODEL))
    return out.reshape(B, S, D_MODEL)
```

```python
import functools
import math

import jax
import jax.numpy as jnp
from jax import lax
from jax.experimental import pallas as pl
from jax.experimental.pallas import tpu as pltpu

F32 = jnp.float32
BF16 = jnp.bfloat16

D_MODEL = 1024
NSA_HEADS = 8
NSA_GROUPS = 2
NSA_HPG = NSA_HEADS // NSA_GROUPS
NSA_DH = 64
CMP_LEN = 32
CMP_STRIDE = 16
CMP_HIDDEN = 256
SEL_BLOCK = 64
SEL_TOPN = 16
WINDOW = 512
ROPE_THETA = 500000.0
ROPE_DIM = NSA_DH // 4
NSA_SCALE = NSA_DH ** -0.5
RET_HEADS = 4
RET_DK = 128
RET_DV = 256
RET_ROPE_BASE = 10000.0
RET_SCALE = RET_DK ** -0.5
RMS_EPS = 1e-6
GN_EPS = 1e-6
PROJ_SIZES = (512, 128, 128, 128, 128, 128, 128, 24, 512, 512, 512, 1024, 1024, 1024, 1024)

LANE = 128
V7X_VMEM_LIMIT_BYTES = 56 * 1024 * 1024

ROW_TILE = 256
POST_ROW_TILE = 512
TQ = 128
TK = 512
BLK_PER_TILE = TK // SEL_BLOCK
WIN_KEYS = WINDOW + TQ
RET_CHUNK = 256
V_ROWS = 80
NEG = -1e30
LOG2E = math.log2(math.e)
K_LANES = LANE
GATE_ROWS = 16
SC_PAD = 8
N_SIZE_CLASSES = 8

_PAD_SIZES = tuple(128 if s == 24 else s for s in PROJ_SIZES)
_OFF = [0]
for _s in _PAD_SIZES:
    _OFF.append(_OFF[-1] + _s)
PROJ_PAD = _OFF[-1]


def _dot(a, b):
    return jnp.dot(a, b, preferred_element_type=F32)


def _split(a):
    hi = a.astype(BF16)
    lo = (a - hi.astype(F32)).astype(BF16)
    return hi, lo


def _dot3(a_hi, a_lo, b_hi, b_lo):
    return _dot(a_hi, b_hi) + _dot(a_lo, b_hi) + _dot(a_hi, b_lo)


def _params(sem, vmem_bytes):
    return pltpu.CompilerParams(dimension_semantics=sem, vmem_limit_bytes=vmem_bytes)


def _inproj_kernel(x_ref, g_ref, whi_ref, bias_ref, freq_ref,
                   qT_ref, kc_ref, vc_ref, ks_ref, vsT_ref, kw_ref, vwT_ref, gT_ref,
                   zn_ref, qr_ref, kr_ref, vr_ref, zr_ref, ma_ref, mb_ref):
    tm = x_ref.shape[0]
    x = x_ref[...]
    ms = jnp.mean(x * x, axis=-1, keepdims=True)
    h = x * lax.rsqrt(ms + RMS_EPS) * g_ref[...]
    h_hi = h.astype(BF16)

    def mm(lo, hi):
        return _dot(h_hi, whi_ref[:, lo:hi])

    lane = lax.broadcasted_iota(jnp.int32, (tm, LANE), 1)
    row = pl.program_id(0) * tm + lax.broadcasted_iota(jnp.int32, (tm, LANE), 0)
    first_group = lane < NSA_DH
    pos = row.astype(F32)

    half = ROPE_DIM // 2
    dim = jnp.bitwise_and(lane, NSA_DH - 1)
    ang = pos * freq_ref[0:1, :]
    c_all, s_all = jnp.cos(ang), jnp.sin(ang)
    c_sw, s_sw = pltpu.roll(c_all, LANE // 2, 1), pltpu.roll(s_all, LANE // 2, 1)
    cr = jnp.where(first_group, c_all, c_sw)
    sr = jnp.where(first_group, -s_all, s_sw)
    lo_half, hi_half = dim < half, (dim >= half) & (dim < ROPE_DIM)
    c_lo, s_lo = jnp.where(first_group, c_sw, c_all), jnp.where(first_group, s_sw, s_all)
    c_hi, s_hi = pltpu.roll(c_lo, half, 1), pltpu.roll(s_lo, half, 1)
    cn = jnp.where(lo_half, c_lo, jnp.where(hi_half, c_hi, 1.0))
    san = jnp.where(hi_half, s_hi, 0.0)
    sbn = jnp.where(lo_half, -s_lo, 0.0)

    def rot_nsa(p):
        return p * cn + pltpu.roll(p, half, 1) * san + pltpu.roll(p, LANE - half, 1) * sbn

    def rot_ret(p):
        return p * cr + pltpu.roll(p, LANE // 2, 1) * sr

    def key_rows(p, fill):
        swapped = pltpu.roll(p, NSA_DH, 1)
        return [jnp.where(first_group, pg, fill).astype(BF16) for pg in (p, swapped)]

    ones_row = jnp.where(lax.broadcasted_iota(jnp.int32, (V_ROWS - NSA_DH, tm), 0) == 0, 1.0, 0.0)

    def value_rows(p):
        pT = p.T
        return [jnp.concatenate([pT[g * NSA_DH:(g + 1) * NSA_DH, :], ones_row], axis=0).astype(BF16)
                for g in range(NSA_GROUPS)]

    p = mm(0, _OFF[2])
    for c in range(4):
        qT_ref[c * LANE:(c + 1) * LANE, :] = (rot_nsa(p[:, c * LANE:(c + 1) * LANE]) * NSA_SCALE).T
    kc_ref[...] = rot_nsa(p[:, 4 * LANE:5 * LANE])

    p = mm(_OFF[2], _OFF[8])
    vc_ref[...] = p[:, 0 * LANE:1 * LANE]
    blk_in_tile = jnp.right_shift(jnp.bitwise_and(row, TK - 1), SEL_BLOCK.bit_length() - 1)
    onehot = jnp.where(lane - NSA_DH == blk_in_tile, 1.0, 0.0)
    ks = key_rows(rot_nsa(p[:, 1 * LANE:2 * LANE]), onehot)
    vs = value_rows(p[:, 2 * LANE:3 * LANE])
    kw = key_rows(rot_nsa(p[:, 3 * LANE:4 * LANE]), 0.0)
    vw = value_rows(p[:, 4 * LANE:5 * LANE])
    gT = jax.nn.sigmoid(p[:, 5 * LANE:6 * LANE] + bias_ref[...]).T
    for g in range(NSA_GROUPS):
        ks_ref[g], vsT_ref[g], kw_ref[g], vwT_ref[g] = ks[g], vs[g], kw[g], vw[g]
        gT_ref[g] = gT[g * GATE_ROWS:(g + 1) * GATE_ROWS, :]

    z = mm(_OFF[8], _OFF[9])
    zn_ref[...] = (z * jax.nn.sigmoid(z)).astype(BF16)

    p = mm(_OFF[9], _OFF[11])
    for c in range(4):
        qr_ref[:, c * LANE:(c + 1) * LANE] = rot_ret(p[:, c * LANE:(c + 1) * LANE]).astype(BF16)
    for c in range(4):
        kr_ref[:, c * LANE:(c + 1) * LANE] = (
            rot_ret(p[:, (4 + c) * LANE:(5 + c) * LANE]) * RET_SCALE).astype(BF16)

    vr_ref[...] = mm(_OFF[11], _OFF[12]).astype(BF16)
    z = mm(_OFF[12], _OFF[13])
    zr_ref[...] = (z * jax.nn.sigmoid(z)).astype(BF16)
    ma_ref[...] = jax.nn.sigmoid(mm(_OFF[13], _OFF[14])).astype(BF16)
    mb_ref[...] = jax.nn.sigmoid(mm(_OFF[14], _OFF[15])).astype(BF16)


def _inproj(x2, norm_pre, w_hi, bias, freqs):
    S = x2.shape[0]
    tm = ROW_TILE
    G = NSA_GROUPS
    row = lambda w: pl.BlockSpec((tm, w), lambda i: (i, 0))
    whole = lambda a: pl.BlockSpec(a.shape, lambda i: (0,) * a.ndim)
    col = lambda r: pl.BlockSpec((r, tm), lambda i: (0, i))
    g_row = lambda w: pl.BlockSpec((G, tm, w), lambda i: (0, i, 0))
    g_col = lambda r: pl.BlockSpec((G, r, tm), lambda i: (0, 0, i))
    sds = jax.ShapeDtypeStruct
    outs = [
        (col(NSA_HEADS * NSA_DH), sds((NSA_HEADS * NSA_DH, S), F32)),
        (row(LANE), sds((S, LANE), F32)),
        (row(LANE), sds((S, LANE), F32)),
        (g_row(K_LANES), sds((G, S, K_LANES), BF16)),
        (g_col(V_ROWS), sds((G, V_ROWS, S), BF16)),
        (g_row(K_LANES), sds((G, S, K_LANES), BF16)),
        (g_col(V_ROWS), sds((G, V_ROWS, S), BF16)),
        (g_col(GATE_ROWS), sds((G, GATE_ROWS, S), F32)),
        (row(512), sds((S, 512), BF16)),
        (row(512), sds((S, 512), BF16)), (row(512), sds((S, 512), BF16)),
        (row(1024), sds((S, 1024), BF16)), (row(1024), sds((S, 1024), BF16)),
        (row(1024), sds((S, 1024), BF16)), (row(1024), sds((S, 1024), BF16)),
    ]
    return pl.pallas_call(
        _inproj_kernel,
        grid=(S // tm,),
        in_specs=[row(D_MODEL), whole(norm_pre), whole(w_hi), whole(bias), whole(freqs)],
        out_specs=[o[0] for o in outs],
        out_shape=[o[1] for o in outs],
        compiler_params=_params(("arbitrary",), V7X_VMEM_LIMIT_BYTES),
        name="inproj",
    )(x2, norm_pre, w_hi, bias, freqs)


def _compress_kernel(x_ref, pe_ref, w1x_ref, w1_ref, w2_ref, o_ref, *, keys):
    NC = x_ref.shape[0]
    prod = lambda a, b: _dot3(*_split(a), *_split(b))
    both = prod(x_ref[...], w1x_ref[...])
    nxt = pltpu.roll(both[:, CMP_HIDDEN:], NC - 1, 0)
    pe_term = prod(pe_ref[...], w1_ref[...])[0:1, :]
    hid = both[:, :CMP_HIDDEN] + nxt + pe_term
    act = hid * (0.5 * (1.0 + jnp.tanh(math.sqrt(2.0 / math.pi) * (hid + 0.044715 * (hid * hid * hid)))))
    out = prod(act, w2_ref[...])
    if keys:
        hi, lo = _split(out)
        o_ref[...] = jnp.concatenate([hi, hi, lo, lo], axis=1)
    else:
        ones_row = jnp.where(lax.broadcasted_iota(jnp.int32, (V_ROWS - NSA_DH, NC), 0) == 0, 1.0, 0.0)
        o_ref[...] = jnp.concatenate([out.T, ones_row], axis=0).astype(BF16)


def _compress(raw, pe, w1, w2, keys):
    S = raw.shape[0]
    NC = S // CMP_STRIDE
    G = NSA_GROUPS
    half = CMP_LEN // 2
    x = raw.reshape(NC, CMP_STRIDE * G * NSA_DH)
    w1r = w1.reshape(2, half, 1, NSA_DH, CMP_HIDDEN)
    sel = (jnp.arange(G)[:, None] == jnp.arange(G)[None, :]).astype(F32)
    w1x = w1r[None] * sel[:, None, None, :, None, None]
    w1x = w1x.reshape(G, 2, half * G * NSA_DH, CMP_HIDDEN).transpose(0, 2, 1, 3)
    w1x = w1x.reshape(G, half * G * NSA_DH, 2 * CMP_HIDDEN)
    pe8 = jnp.pad(pe.reshape(1, -1), ((0, 7), (0, 0)))
    if keys:
        out_spec = pl.BlockSpec((None, NC, 4 * NSA_DH), lambda g: (g, 0, 0))
        out_shape = jax.ShapeDtypeStruct((G, NC, 4 * NSA_DH), BF16)
    else:
        out_spec = pl.BlockSpec((None, V_ROWS, NC), lambda g: (g, 0, 0))
        out_shape = jax.ShapeDtypeStruct((G, V_ROWS, NC), BF16)
    whole = lambda a: pl.BlockSpec(a.shape, lambda g: (0,) * a.ndim)
    return pl.pallas_call(
        functools.partial(_compress_kernel, keys=keys),
        grid=(G,),
        in_specs=[whole(x), whole(pe8),
                  pl.BlockSpec((None,) + w1x.shape[1:], lambda g: (g, 0, 0)),
                  whole(w1), whole(w2)],
        out_specs=out_spec,
        out_shape=out_shape,
        compiler_params=_params(("arbitrary",), V7X_VMEM_LIMIT_BYTES),
        name="compress_k" if keys else "compress_v",
    )(x, pe8, w1x, w1, w2)


def _nsa_kernel(qT_ref, gT_ref, kc4_ref, vcT_ref, ks_ref, vsT_ref, kw_ref, vwT_ref, o_ref,
                sb_ref, sa_ref, sbuf_ref, sc_ref, oc_ref, ow_ref, m_ref, acc_ref):
    i = pl.program_id(1)
    NC = kc4_ref.shape[0]
    NB = sb_ref.shape[0]
    n_q = pl.num_programs(1)
    heads = range(NSA_HPG)
    lanes = lambda a, h: a[:, h * TQ:(h + 1) * TQ]
    colmax = lambda a: jnp.max(a, axis=0, keepdims=True)
    colsum = lambda a: jnp.sum(a, axis=0, keepdims=True)

    qT = qT_ref[...] * LOG2E
    Q = jnp.concatenate([qT[h * NSA_DH:(h + 1) * NSA_DH, :] for h in heads], axis=1)
    q_hi, q_lo = _split(Q)
    q4 = jnp.concatenate([q_hi, q_lo, q_hi, q_lo], axis=0)
    t = i * TQ + lax.broadcasted_iota(jnp.int32, (1, TQ), 1)
    cur = jnp.right_shift(t, SEL_BLOCK.bit_length() - 1)

    def window_branch():
        w0 = pl.multiple_of(jnp.maximum(i * TQ - WINDOW, 0), TQ)
        q_pad = jnp.concatenate([q_hi, jnp.zeros((K_LANES - NSA_DH, NSA_HPG * TQ), BF16)], axis=0)
        s = _dot(kw_ref[pl.ds(w0, WIN_KEYS), :], q_pad)
        key = w0 + lax.broadcasted_iota(jnp.int32, (WIN_KEYS, TQ), 0)
        mask_w = (key <= t) & (key > t - WINDOW)
        p_heads = []
        for h in heads:
            sm = jnp.where(mask_w, lanes(s, h), NEG)
            p_heads.append(jnp.exp2(sm - colmax(sm)).astype(BF16))
        accw = _dot(vwT_ref[:, pl.ds(w0, WIN_KEYS)], jnp.concatenate(p_heads, axis=1))
        owT = accw[0:NSA_DH, :] * (1.0 / accw[NSA_DH:NSA_DH + 1, :])
        ow_ref[...] = owT

    def compressed_and_select(ncp, nbp):
        s = _dot(kc4_ref[0:ncp, :], q4)
        c_end = lax.broadcasted_iota(jnp.int32, (ncp, TQ), 0) * CMP_STRIDE + (CMP_LEN - 1)
        mask_c = c_end <= t
        p_heads, inv = [], []
        for h in heads:
            sm = jnp.where(mask_c, lanes(s, h), -jnp.inf)
            m = colmax(sm)
            m = jnp.where(m == -jnp.inf, 0.0, m)
            p = jnp.exp2(sm - m)
            inv.append(1.0 / jnp.maximum(colsum(p), jnp.finfo(F32).tiny))
            sc_ref[h, 0:SC_PAD, :] = jnp.zeros((SC_PAD, TQ), F32)
            sc_ref[h, SC_PAD:SC_PAD + ncp, :] = p
            p_heads.append(p.astype(BF16))
        oc_ref[...] = (_dot(vcT_ref[:, 0:ncp], jnp.concatenate(p_heads, axis=1))
                       * jnp.concatenate(inv, axis=1))
        per_sel = SEL_BLOCK // CMP_STRIDE
        imp = jnp.zeros((nbp, TQ), F32)
        for h in heads:
            tot = sc_ref[h, pl.ds(SC_PAD - 1, nbp, stride=per_sel), :]
            for r in range(per_sel):
                tot = tot + sc_ref[h, pl.ds(SC_PAD + r, nbp, stride=per_sel), :]
            imp = imp + tot * inv[h]
        blk = lax.broadcasted_iota(jnp.int32, (nbp, TQ), 0)
        blk_f = blk.astype(F32)
        valid = blk <= cur
        forced = (blk == 0) | (blk == cur) | (blk == cur - 1)
        free = valid & jnp.logical_not(forced)
        score = jnp.where(free, imp, -jnp.inf)
        for _ in range(min(SEL_TOPN, NB) - 3):
            mx = colmax(score)
            idx = jnp.min(jnp.where(score == mx, blk_f, float(NB)), axis=0, keepdims=True)
            score = jnp.where(blk_f == idx, -jnp.inf, score)
        picked = valid & (forced | (score == -jnp.inf))
        sb_ref[0:nbp, :] = jnp.where(picked, 0.0, NEG)
        if nbp < NB:
            sb_ref[nbp:NB, :] = jnp.full((NB - nbp, TQ), NEG, F32)
        window_branch()

    size_class = lax.div(i * N_SIZE_CLASSES, n_q)
    for k in range(N_SIZE_CLASSES):
        pl.when(size_class == k)(functools.partial(
            compressed_and_select, NC * (k + 1) // N_SIZE_CLASSES, NB * (k + 1) // N_SIZE_CLASSES))

    pad_rows = jnp.zeros((K_LANES - NSA_DH - 2 * BLK_PER_TILE, NSA_HPG * TQ), BF16)

    def scores(j):
        k0 = pl.multiple_of(j * TK, TK)
        sbt = sb_ref[pl.ds(pl.multiple_of(j * BLK_PER_TILE, BLK_PER_TILE), BLK_PER_TILE), :]
        rows = jnp.concatenate([sbt, jnp.zeros_like(sbt)], axis=0)
        rows = jnp.concatenate([rows] * NSA_HPG, axis=1).astype(BF16)
        w = jnp.concatenate([q_hi, rows, pad_rows], axis=0)
        return _dot(ks_ref[pl.ds(k0, TK), :], w)

    def attend(j, s, carry, causal):
        m, acc = carry
        k0 = pl.multiple_of(j * TK, TK)
        if causal:
            keep = k0 + lax.broadcasted_iota(jnp.int32, (TK, TQ), 0) <= t
            s = jnp.concatenate([jnp.where(keep, lanes(s, h), NEG) for h in heads], axis=1)
        m_new = jnp.maximum(m, colmax(s))
        alpha = jnp.exp2(m - m_new)
        p = jnp.exp2(s - m_new).astype(BF16)
        acc = acc * alpha + _dot(vsT_ref[:, pl.ds(k0, TK)], p)
        return m_new, acc

    jd = lax.div(i, TK // TQ)
    n_pairs = lax.div(jd, 2)
    sa_ref[...] = scores(0)

    def pair(pp, carry):
        sbuf_ref[...] = scores(2 * pp + 1)
        carry = attend(2 * pp, sa_ref[...], carry, False)
        sa_ref[...] = scores(2 * pp + 2)
        return attend(2 * pp + 1, sbuf_ref[...], carry, False)

    carry = (jnp.full((1, NSA_HPG * TQ), NEG, F32), jnp.zeros((V_ROWS, NSA_HPG * TQ), F32))
    quad = lambda qq, c: pair(2 * qq + 1, pair(2 * qq, c))
    n_octs, n_quads = lax.div(n_pairs, 4), lax.div(n_pairs, 2)
    carry = lax.fori_loop(0, n_octs, lambda oo, c: quad(2 * oo + 1, quad(2 * oo, c)), carry)
    carry = lax.fori_loop(2 * n_octs, n_quads, quad, carry)
    carry = lax.fori_loop(2 * n_quads, n_pairs, pair, carry)
    sbuf_ref[...] = scores(2 * n_pairs + 1)

    m_ref[...], acc_ref[...] = attend(2 * n_pairs, sa_ref[...], carry, True)

    @pl.when(jd != 2 * n_pairs)
    def _():
        m_ref[...], acc_ref[...] = attend(
            2 * n_pairs + 1, sbuf_ref[...], (m_ref[...], acc_ref[...]), True)

    acc = acc_ref[...]
    osT = acc[0:NSA_DH, :] * (1.0 / acc[NSA_DH:NSA_DH + 1, :])

    gT = gT_ref[...]
    ocT, owT = oc_ref[...], ow_ref[...]
    outs = []
    for h in heads:
        g_c, g_s, g_w = (gT[3 * h + b:3 * h + b + 1, :] for b in range(3))
        outs.append(g_c * lanes(ocT, h)[0:NSA_DH, :] + g_s * lanes(osT, h) + g_w * lanes(owT, h))
    o_ref[...] = jnp.concatenate(outs, axis=0).T.astype(BF16)


def _nsa(qT, gT, kc4, vcT, ks, vsT, kw, vwT):
    G, S = ks.shape[0], ks.shape[1]
    NC, NB = kc4.shape[1], S // SEL_BLOCK
    per_g = lambda a: pl.BlockSpec((None,) + a.shape[1:], lambda g, i: (g,) + (0,) * (a.ndim - 1))
    wide = NSA_HPG * TQ
    return pl.pallas_call(
        _nsa_kernel,
        grid=(G, S // TQ),
        in_specs=[
            pl.BlockSpec((NSA_HPG * NSA_DH, TQ), lambda g, i: (g, i)),
            pl.BlockSpec((None, GATE_ROWS, TQ), lambda g, i: (g, 0, i)),
            per_g(kc4), per_g(vcT), per_g(ks), per_g(vsT), per_g(kw), per_g(vwT),
        ],
        out_specs=pl.BlockSpec((TQ, NSA_HPG * NSA_DH), lambda g, i: (i, g)),
        out_shape=jax.ShapeDtypeStruct((S, NSA_HEADS * NSA_DH), BF16),
        scratch_shapes=[pltpu.VMEM((NB, TQ), F32),
                        pltpu.VMEM((TK, wide), F32),
                        pltpu.VMEM((TK, wide), F32),
                        pltpu.VMEM((NSA_HPG, SC_PAD + NC, TQ), F32),
                        pltpu.VMEM((V_ROWS, wide), F32),
                        pltpu.VMEM((NSA_DH, wide), F32),
                        pltpu.VMEM((1, wide), F32),
                        pltpu.VMEM((V_ROWS, wide), F32)],
        compiler_params=_params(("arbitrary", "arbitrary"), V7X_VMEM_LIMIT_BYTES),
        name="nsa",
    )(qT, gT, kc4, vcT, ks, vsT, kw, vwT)


def _ret_kernel(q_ref, k_ref, v_ref, o_ref, r_ref, dm_ref, qd_ref, kd_ref):
    C = RET_CHUNK
    log_g = [math.log(1.0 - 2.0 ** (-5.0 - h)) for h in range(RET_HEADS)]

    @pl.when(pl.program_id(0) == 0)
    def _():
        r_ref[...] = jnp.zeros(r_ref.shape, F32)
        diff = (lax.broadcasted_iota(jnp.int32, (C, C), 0)
                - lax.broadcasted_iota(jnp.int32, (C, C), 1)).astype(F32)
        n = lax.broadcasted_iota(jnp.int32, (C, RET_DK), 0).astype(F32)
        for h in range(RET_HEADS):
            dm_ref[h] = jnp.where(diff >= 0.0, jnp.exp(jnp.maximum(diff, 0.0) * log_g[h]), 0.0)
            qd_ref[h] = jnp.exp((n + 1.0) * log_g[h])
            kd_ref[h] = jnp.exp((C - 1.0 - n) * log_g[h])

    for h in range(RET_HEADS):
        q = q_ref[:, h * RET_DK:(h + 1) * RET_DK]
        k = k_ref[:, h * RET_DK:(h + 1) * RET_DK]
        v = v_ref[:, h * RET_DV:(h + 1) * RET_DV]
        att = lax.dot_general(q, k, (((1,), (1,)), ((), ())), preferred_element_type=F32)
        o = _dot((att * dm_ref[h]).astype(BF16), v)
        r = r_ref[h]
        qd = (q.astype(F32) * qd_ref[h]).astype(BF16)
        o = o + _dot(qd, r.astype(BF16))
        kd = (k.astype(F32) * kd_ref[h]).astype(BF16)
        r_ref[h] = math.exp(C * log_g[h]) * r + lax.dot_general(
            kd, v, (((0,), (0,)), ((), ())), preferred_element_type=F32)
        mu = jnp.mean(o, axis=-1, keepdims=True)
        d = o - mu
        var = jnp.mean(d * d, axis=-1, keepdims=True)
        o_ref[:, h * RET_DV:(h + 1) * RET_DV] = (d * lax.rsqrt(var + GN_EPS)).astype(BF16)


def _retention(qr, kr, vr):
    S = qr.shape[0]
    C = RET_CHUNK
    row = lambda w: pl.BlockSpec((C, w), lambda n: (n, 0))
    return pl.pallas_call(
        _ret_kernel,
        grid=(S // C,),
        in_specs=[row(RET_HEADS * RET_DK), row(RET_HEADS * RET_DK), row(RET_HEADS * RET_DV)],
        out_specs=row(RET_HEADS * RET_DV),
        out_shape=jax.ShapeDtypeStruct((S, RET_HEADS * RET_DV), BF16),
        scratch_shapes=[
            pltpu.VMEM((RET_HEADS, RET_DK, RET_DV), F32),
            pltpu.VMEM((RET_HEADS, C, C), F32),
            pltpu.VMEM((RET_HEADS, C, RET_DK), F32),
            pltpu.VMEM((RET_HEADS, C, RET_DK), F32),
        ],
        compiler_params=_params(("arbitrary",), V7X_VMEM_LIMIT_BYTES),
        name="retention",
    )(qr, kr, vr)


def _post_kernel(x_ref, oa_ref, zn_ref, ob_ref, zr_ref, ma_ref, mb_ref,
                 wa_ref, wb_ref, wo_ref, g_ref, out_ref):
    f32 = lambda r: r[...].astype(F32)
    ya = _dot((f32(oa_ref) * f32(zn_ref)).astype(BF16), wa_ref[...])
    yb = _dot((f32(ob_ref) * f32(zr_ref)).astype(BF16), wb_ref[...])
    merged = f32(ma_ref) * ya + f32(mb_ref) * yb
    y = _dot(merged.astype(BF16), wo_ref[...])
    ms = jnp.mean(y * y, axis=-1, keepdims=True)
    out_ref[...] = x_ref[...] + y * lax.rsqrt(ms + RMS_EPS) * g_ref[...]


def _post(x2, oa, zn, ob, zr, ma, mb, wa, wb, wo, g_post):
    S = x2.shape[0]
    tm = POST_ROW_TILE
    row = lambda a: pl.BlockSpec((tm, a.shape[1]), lambda i: (i, 0))
    whole = lambda a: pl.BlockSpec(a.shape, lambda i: (0, 0))
    rows = (x2, oa, zn, ob, zr, ma, mb)
    consts = (wa, wb, wo, g_post)
    return pl.pallas_call(
        _post_kernel,
        grid=(S // tm,),
        in_specs=[row(a) for a in rows] + [whole(a) for a in consts],
        out_specs=row(x2),
        out_shape=jax.ShapeDtypeStruct(x2.shape, x2.dtype),
        compiler_params=_params(("arbitrary",), V7X_VMEM_LIMIT_BYTES),
        name="post",
    )(*rows, *consts)


def _rotary_frequencies():
    nsa_inv = 1.0 / (ROPE_THETA ** (jnp.arange(0, ROPE_DIM, 2, dtype=F32) / ROPE_DIM))
    ret_inv = 1.0 / (RET_ROPE_BASE ** jnp.linspace(0.0, 1.0, RET_DK // 2, dtype=F32))
    row = jnp.concatenate([ret_inv, nsa_inv, jnp.zeros((LANE - RET_DK // 2 - ROPE_DIM // 2,), F32)])
    return jnp.pad(row[None, :], ((0, 7), (0, 0)))


def kernel(x, norm_pre, w_in, b_nsa_gate, cmp_pe_k, cmp_w1_k, cmp_w2_k, cmp_pe_v, cmp_w1_v,
           cmp_w2_v, w_nsa_o, w_ret_o, w_out, norm_post):
    B, S, _ = x.shape
    assert B == 1 and S % (2 * TK) == 0 and S >= WIN_KEYS
    assert S % (SEL_BLOCK * 8 * N_SIZE_CLASSES) == 0 and (S // TQ) % N_SIZE_CLASSES == 0
    x2 = x.reshape(S, D_MODEL)

    w_bf = w_in.astype(BF16)
    gate_cols = lambda a: jnp.pad(
        a.reshape(a.shape[:-1] + (NSA_GROUPS, NSA_HPG * 3)),
        [(0, 0)] * (a.ndim - 1) + [(0, 0), (0, GATE_ROWS - NSA_HPG * 3)]).reshape(a.shape[:-1] + (-1,))
    segs, off = [], 0
    for sz, pad in zip(PROJ_SIZES, _PAD_SIZES):
        seg = w_bf[:, off:off + sz]
        if sz == NSA_HEADS * 3:
            seg = gate_cols(seg)
        segs.append(jnp.pad(seg, ((0, 0), (0, pad - seg.shape[1]))))
        off += sz
    w_hi = jnp.concatenate(segs, axis=1)
    bias = gate_cols(b_nsa_gate)
    bias = jnp.pad(bias, (0, LANE - bias.shape[0])).reshape(1, LANE)

    (qT, kc, vc, ks, vsT, kw, vwT, gT, zn, qr, kr, vr, zr, ma, mb) = _inproj(
        x2, norm_pre.reshape(1, D_MODEL), w_hi, bias, _rotary_frequencies())

    kc4 = _compress(kc, cmp_pe_k, cmp_w1_k, cmp_w2_k, keys=True)
    vcT = _compress(vc, cmp_pe_v, cmp_w1_v, cmp_w2_v, keys=False)

    oa = _nsa(qT, gT, kc4, vcT, ks, vsT, kw, vwT)
    ob = _retention(qr, kr, vr)

    out = _post(x2, oa, zn, ob, zr, ma, mb, w_nsa_o.astype(BF16), w_ret_o.astype(BF16),
                w_out.astype(BF16), norm_post.reshape(1, D_MODEL))
    return out.reshape(B, S, D_MODEL)
```

```python
import functools
import math

import jax
import jax.numpy as jnp
from jax import lax
from jax.experimental import pallas as pl
from jax.experimental.pallas import tpu as pltpu

F32 = jnp.float32
BF16 = jnp.bfloat16

D_MODEL = 1024
NSA_HEADS = 8
NSA_GROUPS = 2
NSA_HPG = NSA_HEADS // NSA_GROUPS
NSA_DH = 64
CMP_LEN = 32
CMP_STRIDE = 16
CMP_HIDDEN = 256
SEL_BLOCK = 64
SEL_TOPN = 16
WINDOW = 512
ROPE_THETA = 500000.0
ROPE_DIM = NSA_DH // 4
NSA_SCALE = NSA_DH ** -0.5
RET_HEADS = 4
RET_DK = 128
RET_DV = 256
RET_ROPE_BASE = 10000.0
RET_SCALE = RET_DK ** -0.5
RMS_EPS = 1e-6
GN_EPS = 1e-6
PROJ_SIZES = (512, 128, 128, 128, 128, 128, 128, 24, 512, 512, 512, 1024, 1024, 1024, 1024)

LANE = 128
V7X_VMEM_LIMIT_BYTES = 56 * 1024 * 1024
VMEM_TEMP_FLOOR_BYTES = 8 * 1024 * 1024

ROW_TILE = 256
POST_ROW_TILE = 512
TQ = 128
TK = 512
BLK_PER_TILE = TK // SEL_BLOCK
WIN_KEYS = WINDOW + TQ
RET_CHUNK = 256
V_ROWS = 80
NEG = -1e30
LOG2E = math.log2(math.e)
K_LANES = LANE
GATE_ROWS = 16
SC_PAD = 8
N_SIZE_CLASSES = 8

_PAD_SIZES = tuple(128 if s == 24 else s for s in PROJ_SIZES)
_OFF = [0]
for _s in _PAD_SIZES:
    _OFF.append(_OFF[-1] + _s)


def _dot(a, b):
    return jnp.dot(a, b, preferred_element_type=F32)


def _split(a):
    hi = a.astype(BF16)
    lo = (a - hi.astype(F32)).astype(BF16)
    return hi, lo


def _dot3(a_hi, a_lo, b_hi, b_lo):
    return _dot(a_hi, b_hi) + _dot(a_lo, b_hi) + _dot(a_hi, b_lo)


def _nbytes(shape, dtype):
    return math.prod(d for d in shape if d is not None) * jnp.dtype(dtype).itemsize


def _params(sem, blocks, scratch=()):
    est = 2 * sum(_nbytes(*b) for b in blocks) + sum(_nbytes(*b) for b in scratch)
    limit = min(V7X_VMEM_LIMIT_BYTES, est + est // 4 + VMEM_TEMP_FLOOR_BYTES)
    return pltpu.CompilerParams(dimension_semantics=sem, vmem_limit_bytes=limit)


def _inproj_kernel(x_ref, g_ref, whi_ref, bias_ref, freq_ref,
                   qT_ref, kc_ref, vc_ref, ks_ref, vsT_ref, kw_ref, vwT_ref, gT_ref,
                   zn_ref, qr_ref, kr_ref, vr_ref, zr_ref, ma_ref, mb_ref):
    tm = x_ref.shape[0]
    x = x_ref[...]
    ms = jnp.mean(x * x, axis=-1, keepdims=True)
    h = x * lax.rsqrt(ms + RMS_EPS) * g_ref[...]
    h_hi = h.astype(BF16)

    def mm(lo, hi):
        return _dot(h_hi, whi_ref[:, lo:hi])

    lane = lax.broadcasted_iota(jnp.int32, (tm, LANE), 1)
    row = pl.program_id(0) * tm + lax.broadcasted_iota(jnp.int32, (tm, LANE), 0)
    first_group = lane < NSA_DH
    pos = row.astype(F32)

    half = ROPE_DIM // 2
    dim = jnp.bitwise_and(lane, NSA_DH - 1)
    ang = pos * freq_ref[0:1, :]
    c_all, s_all = jnp.cos(ang), jnp.sin(ang)
    c_sw, s_sw = pltpu.roll(c_all, LANE // 2, 1), pltpu.roll(s_all, LANE // 2, 1)
    cr = jnp.where(first_group, c_all, c_sw)
    sr = jnp.where(first_group, -s_all, s_sw)
    lo_half, hi_half = dim < half, (dim >= half) & (dim < ROPE_DIM)
    c_lo, s_lo = jnp.where(first_group, c_sw, c_all), jnp.where(first_group, s_sw, s_all)
    c_hi, s_hi = pltpu.roll(c_lo, half, 1), pltpu.roll(s_lo, half, 1)
    cn = jnp.where(lo_half, c_lo, jnp.where(hi_half, c_hi, 1.0))
    san = jnp.where(hi_half, s_hi, 0.0)
    sbn = jnp.where(lo_half, -s_lo, 0.0)

    def rot_nsa(p):
        return p * cn + pltpu.roll(p, half, 1) * san + pltpu.roll(p, LANE - half, 1) * sbn

    def rot_ret(p):
        return p * cr + pltpu.roll(p, LANE // 2, 1) * sr

    def key_rows(p, fill):
        swapped = pltpu.roll(p, NSA_DH, 1)
        return [jnp.where(first_group, pg, fill).astype(BF16) for pg in (p, swapped)]

    ones_row = jnp.where(lax.broadcasted_iota(jnp.int32, (V_ROWS - NSA_DH, tm), 0) == 0, 1.0, 0.0)

    def value_rows(p):
        pT = p.T
        return [jnp.concatenate([pT[g * NSA_DH:(g + 1) * NSA_DH, :], ones_row], axis=0).astype(BF16)
                for g in range(NSA_GROUPS)]

    p = mm(0, _OFF[2])
    for c in range(4):
        qT_ref[c * LANE:(c + 1) * LANE, :] = (rot_nsa(p[:, c * LANE:(c + 1) * LANE]) * NSA_SCALE).T
    kc_ref[...] = rot_nsa(p[:, 4 * LANE:5 * LANE])

    p = mm(_OFF[2], _OFF[8])
    vc_ref[...] = p[:, 0 * LANE:1 * LANE]
    blk_in_tile = jnp.right_shift(jnp.bitwise_and(row, TK - 1), SEL_BLOCK.bit_length() - 1)
    onehot = jnp.where(lane - NSA_DH == blk_in_tile, 1.0, 0.0)
    ks = key_rows(rot_nsa(p[:, 1 * LANE:2 * LANE]), onehot)
    vs = value_rows(p[:, 2 * LANE:3 * LANE])
    kw = key_rows(rot_nsa(p[:, 3 * LANE:4 * LANE]), 0.0)
    vw = value_rows(p[:, 4 * LANE:5 * LANE])
    gT = jax.nn.sigmoid(p[:, 5 * LANE:6 * LANE] + bias_ref[...]).T
    for g in range(NSA_GROUPS):
        ks_ref[g], vsT_ref[g], kw_ref[g], vwT_ref[g] = ks[g], vs[g], kw[g], vw[g]
        gT_ref[g] = gT[g * GATE_ROWS:(g + 1) * GATE_ROWS, :]

    z = mm(_OFF[8], _OFF[9])
    zn_ref[...] = (z * jax.nn.sigmoid(z)).astype(BF16)

    p = mm(_OFF[9], _OFF[11])
    for c in range(4):
        qr_ref[:, c * LANE:(c + 1) * LANE] = rot_ret(p[:, c * LANE:(c + 1) * LANE]).astype(BF16)
    for c in range(4):
        kr_ref[:, c * LANE:(c + 1) * LANE] = (
            rot_ret(p[:, (4 + c) * LANE:(5 + c) * LANE]) * RET_SCALE).astype(BF16)

    vr_ref[...] = mm(_OFF[11], _OFF[12]).astype(BF16)
    z = mm(_OFF[12], _OFF[13])
    zr_ref[...] = (z * jax.nn.sigmoid(z)).astype(BF16)
    ma_ref[...] = jax.nn.sigmoid(mm(_OFF[13], _OFF[14])).astype(BF16)
    mb_ref[...] = jax.nn.sigmoid(mm(_OFF[14], _OFF[15])).astype(BF16)


def _inproj(x2, norm_pre, w_hi, bias, freqs):
    S = x2.shape[0]
    tm = ROW_TILE
    G = NSA_GROUPS
    row = lambda w: pl.BlockSpec((tm, w), lambda i: (i, 0))
    whole = lambda a: pl.BlockSpec(a.shape, lambda i: (0,) * a.ndim)
    col = lambda r: pl.BlockSpec((r, tm), lambda i: (0, i))
    g_row = lambda w: pl.BlockSpec((G, tm, w), lambda i: (0, i, 0))
    g_col = lambda r: pl.BlockSpec((G, r, tm), lambda i: (0, 0, i))
    sds = jax.ShapeDtypeStruct
    outs = [
        (col(NSA_HEADS * NSA_DH), sds((NSA_HEADS * NSA_DH, S), F32)),
        (row(LANE), sds((S, LANE), F32)),
        (row(LANE), sds((S, LANE), F32)),
        (g_row(K_LANES), sds((G, S, K_LANES), BF16)),
        (g_col(V_ROWS), sds((G, V_ROWS, S), BF16)),
        (g_row(K_LANES), sds((G, S, K_LANES), BF16)),
        (g_col(V_ROWS), sds((G, V_ROWS, S), BF16)),
        (g_col(GATE_ROWS), sds((G, GATE_ROWS, S), F32)),
        (row(512), sds((S, 512), BF16)),
        (row(512), sds((S, 512), BF16)), (row(512), sds((S, 512), BF16)),
        (row(1024), sds((S, 1024), BF16)), (row(1024), sds((S, 1024), BF16)),
        (row(1024), sds((S, 1024), BF16)), (row(1024), sds((S, 1024), BF16)),
    ]
    return pl.pallas_call(
        _inproj_kernel,
        grid=(S // tm,),
        in_specs=[row(D_MODEL), whole(norm_pre), whole(w_hi), whole(bias), whole(freqs)],
        out_specs=[o[0] for o in outs],
        out_shape=[o[1] for o in outs],
        compiler_params=_params(
            ("arbitrary",),
            [((tm, D_MODEL), F32), (w_hi.shape, BF16)] + [(o[0].block_shape, o[1].dtype) for o in outs]),
        name="inproj",
    )(x2, norm_pre, w_hi, bias, freqs)


def _compress_kernel(x_ref, pe_ref, w1_ref, w2_ref, o_ref, *, keys):
    NC = x_ref.shape[0]
    g = pl.program_id(0)
    first_half = lax.broadcasted_iota(jnp.int32, (NC, LANE), 1) < NSA_DH
    tokens = []
    for u in range(CMP_STRIDE // 2):
        a, b = x_ref[:, (2 * u) * LANE:(2 * u + 1) * LANE], x_ref[:, (2 * u + 1) * LANE:(2 * u + 2) * LANE]
        ar, br = pltpu.roll(a, NSA_DH, 1), pltpu.roll(b, NSA_DH, 1)
        tokens.append(jnp.where(first_half, jnp.where(g == 0, a, ar), jnp.where(g == 0, br, b)))
    xg = jnp.concatenate(tokens, axis=1)
    prod = lambda a, b: _dot3(*_split(a), *_split(b))
    both = prod(xg, w1_ref[...])
    nxt = pltpu.roll(both[:, CMP_HIDDEN:], NC - 1, 0)
    pe_term = prod(pe_ref[...], w1_ref[...])
    hid = both[:, :CMP_HIDDEN] + nxt + pe_term[0:1, :CMP_HIDDEN] + pe_term[1:2, CMP_HIDDEN:]
    act = hid * (0.5 * (1.0 + jnp.tanh(math.sqrt(2.0 / math.pi) * (hid + 0.044715 * (hid * hid * hid)))))
    out = prod(act, w2_ref[...])
    if keys:
        hi, lo = _split(out)
        o_ref[...] = jnp.concatenate([hi, hi, lo, lo], axis=1)
    else:
        ones_row = jnp.where(lax.broadcasted_iota(jnp.int32, (V_ROWS - NSA_DH, NC), 0) == 0, 1.0, 0.0)
        o_ref[...] = jnp.concatenate([out.T, ones_row], axis=0).astype(BF16)


def _compress(raw, pe, w1, w2, keys):
    S = raw.shape[0]
    NC = S // CMP_STRIDE
    G = NSA_GROUPS
    half = CMP_STRIDE * NSA_DH
    x = raw.reshape(NC, CMP_STRIDE * G * NSA_DH)
    w1h = w1.reshape(2, half, CMP_HIDDEN).transpose(1, 0, 2).reshape(half, 2 * CMP_HIDDEN)
    pe8 = jnp.pad(pe.reshape(2, half), ((0, 6), (0, 0)))
    if keys:
        out_spec = pl.BlockSpec((None, NC, 4 * NSA_DH), lambda g: (g, 0, 0))
        out_shape = jax.ShapeDtypeStruct((G, NC, 4 * NSA_DH), BF16)
    else:
        out_spec = pl.BlockSpec((None, V_ROWS, NC), lambda g: (g, 0, 0))
        out_shape = jax.ShapeDtypeStruct((G, V_ROWS, NC), BF16)
    whole = lambda a: pl.BlockSpec(a.shape, lambda g: (0,) * a.ndim)
    return pl.pallas_call(
        functools.partial(_compress_kernel, keys=keys),
        grid=(G,),
        in_specs=[whole(x), whole(pe8), whole(w1h), whole(w2)],
        out_specs=out_spec,
        out_shape=out_shape,
        compiler_params=_params(
            ("arbitrary",),
            [(x.shape, F32), (w1h.shape, F32), (w2.shape, F32), (out_shape.shape[1:], BF16)],
            [((NC, 2 * CMP_HIDDEN), F32), ((NC, half), F32)]),
        name="compress_k" if keys else "compress_v",
    )(x, pe8, w1h, w2)


def _nsa_kernel(qT_ref, gT_ref, kc4_ref, vcT_ref, ks_ref, vsT_ref, kw_ref, vwT_ref, o_ref,
                sb_ref, sa_ref, sbuf_ref, sc_ref, oc_ref, ow_ref, m_ref, acc_ref):
    i = pl.program_id(1)
    NC = kc4_ref.shape[0]
    NB = sb_ref.shape[0]
    n_q = pl.num_programs(1)
    heads = range(NSA_HPG)
    lanes = lambda a, h: a[:, h * TQ:(h + 1) * TQ]
    colmax = lambda a: jnp.max(a, axis=0, keepdims=True)
    colsum = lambda a: jnp.sum(a, axis=0, keepdims=True)

    qT = qT_ref[...] * LOG2E
    Q = jnp.concatenate([qT[h * NSA_DH:(h + 1) * NSA_DH, :] for h in heads], axis=1)
    q_hi, q_lo = _split(Q)
    q4 = jnp.concatenate([q_hi, q_lo, q_hi, q_lo], axis=0)
    t = i * TQ + lax.broadcasted_iota(jnp.int32, (1, TQ), 1)
    cur = jnp.right_shift(t, SEL_BLOCK.bit_length() - 1)

    def window_branch():
        w0 = pl.multiple_of(jnp.maximum(i * TQ - WINDOW, 0), TQ)
        q_pad = jnp.concatenate([q_hi, jnp.zeros((K_LANES - NSA_DH, NSA_HPG * TQ), BF16)], axis=0)
        s = _dot(kw_ref[pl.ds(w0, WIN_KEYS), :], q_pad)
        key = w0 + lax.broadcasted_iota(jnp.int32, (WIN_KEYS, TQ), 0)
        mask_w = (key <= t) & (key > t - WINDOW)
        p_heads = []
        for h in heads:
            sm = jnp.where(mask_w, lanes(s, h), NEG)
            p_heads.append(jnp.exp2(sm - colmax(sm)).astype(BF16))
        accw = _dot(vwT_ref[:, pl.ds(w0, WIN_KEYS)], jnp.concatenate(p_heads, axis=1))
        owT = accw[0:NSA_DH, :] * (1.0 / accw[NSA_DH:NSA_DH + 1, :])
        ow_ref[...] = owT

    def compressed_and_select(ncp, nbp):
        s = _dot(kc4_ref[0:ncp, :], q4)
        c_end = lax.broadcasted_iota(jnp.int32, (ncp, TQ), 0) * CMP_STRIDE + (CMP_LEN - 1)
        mask_c = c_end <= t
        p_heads, inv = [], []
        for h in heads:
            sm = jnp.where(mask_c, lanes(s, h), -jnp.inf)
            m = colmax(sm)
            m = jnp.where(m == -jnp.inf, 0.0, m)
            p = jnp.exp2(sm - m)
            inv.append(1.0 / jnp.maximum(colsum(p), jnp.finfo(F32).tiny))
            sc_ref[h, 0:SC_PAD, :] = jnp.zeros((SC_PAD, TQ), F32)
            sc_ref[h, SC_PAD:SC_PAD + ncp, :] = p
            p_heads.append(p.astype(BF16))
        oc_ref[...] = (_dot(vcT_ref[:, 0:ncp], jnp.concatenate(p_heads, axis=1))
                       * jnp.concatenate(inv, axis=1))
        per_sel = SEL_BLOCK // CMP_STRIDE
        imp = jnp.zeros((nbp, TQ), F32)
        for h in heads:
            tot = sc_ref[h, pl.ds(SC_PAD - 1, nbp, stride=per_sel), :]
            for r in range(per_sel):
                tot = tot + sc_ref[h, pl.ds(SC_PAD + r, nbp, stride=per_sel), :]
            imp = imp + tot * inv[h]
        blk = lax.broadcasted_iota(jnp.int32, (nbp, TQ), 0)
        blk_f = blk.astype(F32)
        valid = blk <= cur
        forced = (blk == 0) | (blk == cur) | (blk == cur - 1)
        free = valid & jnp.logical_not(forced)
        score = jnp.where(free, imp, -jnp.inf)
        for _ in range(min(SEL_TOPN, NB) - 3):
            mx = colmax(score)
            idx = jnp.min(jnp.where(score == mx, blk_f, float(NB)), axis=0, keepdims=True)
            score = jnp.where(blk_f == idx, -jnp.inf, score)
        picked = valid & (forced | (score == -jnp.inf))
        sb_ref[0:nbp, :] = jnp.where(picked, 0.0, NEG)
        if nbp < NB:
            sb_ref[nbp:NB, :] = jnp.full((NB - nbp, TQ), NEG, F32)
        window_branch()

    size_class = lax.div(i * N_SIZE_CLASSES, n_q)
    for k in range(N_SIZE_CLASSES):
        pl.when(size_class == k)(functools.partial(
            compressed_and_select, NC * (k + 1) // N_SIZE_CLASSES, NB * (k + 1) // N_SIZE_CLASSES))

    pad_rows = jnp.zeros((K_LANES - NSA_DH - 2 * BLK_PER_TILE, NSA_HPG * TQ), BF16)

    def scores(j):
        k0 = pl.multiple_of(j * TK, TK)
        sbt = sb_ref[pl.ds(pl.multiple_of(j * BLK_PER_TILE, BLK_PER_TILE), BLK_PER_TILE), :]
        rows = jnp.concatenate([sbt, jnp.zeros_like(sbt)], axis=0)
        rows = jnp.concatenate([rows] * NSA_HPG, axis=1).astype(BF16)
        w = jnp.concatenate([q_hi, rows, pad_rows], axis=0)
        return _dot(ks_ref[pl.ds(k0, TK), :], w)

    def attend(j, s, carry, causal):
        m, acc = carry
        k0 = pl.multiple_of(j * TK, TK)
        if causal:
            keep = k0 + lax.broadcasted_iota(jnp.int32, (TK, TQ), 0) <= t
            s = jnp.concatenate([jnp.where(keep, lanes(s, h), NEG) for h in heads], axis=1)
        m_new = jnp.maximum(m, colmax(s))
        alpha = jnp.exp2(m - m_new)
        p = jnp.exp2(s - m_new).astype(BF16)
        acc = acc * alpha + _dot(vsT_ref[:, pl.ds(k0, TK)], p)
        return m_new, acc

    jd = lax.div(i, TK // TQ)
    n_pairs = lax.div(jd, 2)
    sa_ref[...] = scores(0)

    def pair(pp, carry):
        sbuf_ref[...] = scores(2 * pp + 1)
        carry = attend(2 * pp, sa_ref[...], carry, False)
        sa_ref[...] = scores(2 * pp + 2)
        return attend(2 * pp + 1, sbuf_ref[...], carry, False)

    carry = (jnp.full((1, NSA_HPG * TQ), NEG, F32), jnp.zeros((V_ROWS, NSA_HPG * TQ), F32))
    quad = lambda qq, c: pair(2 * qq + 1, pair(2 * qq, c))
    n_octs, n_quads = lax.div(n_pairs, 4), lax.div(n_pairs, 2)
    carry = lax.fori_loop(0, n_octs, lambda oo, c: quad(2 * oo + 1, quad(2 * oo, c)), carry)
    carry = lax.fori_loop(2 * n_octs, n_quads, quad, carry)
    carry = lax.fori_loop(2 * n_quads, n_pairs, pair, carry)
    sbuf_ref[...] = scores(2 * n_pairs + 1)

    m_ref[...], acc_ref[...] = attend(2 * n_pairs, sa_ref[...], carry, True)

    @pl.when(jd != 2 * n_pairs)
    def _():
        m_ref[...], acc_ref[...] = attend(
            2 * n_pairs + 1, sbuf_ref[...], (m_ref[...], acc_ref[...]), True)

    acc = acc_ref[...]
    osT = acc[0:NSA_DH, :] * (1.0 / acc[NSA_DH:NSA_DH + 1, :])

    gT = gT_ref[...]
    ocT, owT = oc_ref[...], ow_ref[...]
    outs = []
    for h in heads:
        g_c, g_s, g_w = (gT[3 * h + b:3 * h + b + 1, :] for b in range(3))
        outs.append(g_c * lanes(ocT, h)[0:NSA_DH, :] + g_s * lanes(osT, h) + g_w * lanes(owT, h))
    o_ref[...] = jnp.concatenate(outs, axis=0).T.astype(BF16)


def _nsa(qT, gT, kc4, vcT, ks, vsT, kw, vwT):
    G, S = ks.shape[0], ks.shape[1]
    NC, NB = kc4.shape[1], S // SEL_BLOCK
    per_g = lambda a: pl.BlockSpec((None,) + a.shape[1:], lambda g, i: (g,) + (0,) * (a.ndim - 1))
    wide = NSA_HPG * TQ
    scratch = [(NB, TQ),
               (TK, wide), (TK, wide),
               (NSA_HPG, SC_PAD + NC, TQ),
               (V_ROWS, wide), (NSA_DH, wide),
               (1, wide), (V_ROWS, wide)]
    return pl.pallas_call(
        _nsa_kernel,
        grid=(G, S // TQ),
        in_specs=[
            pl.BlockSpec((NSA_HPG * NSA_DH, TQ), lambda g, i: (g, i)),
            pl.BlockSpec((None, GATE_ROWS, TQ), lambda g, i: (g, 0, i)),
            per_g(kc4), per_g(vcT), per_g(ks), per_g(vsT), per_g(kw), per_g(vwT),
        ],
        out_specs=pl.BlockSpec((TQ, NSA_HPG * NSA_DH), lambda g, i: (i, g)),
        out_shape=jax.ShapeDtypeStruct((S, NSA_HEADS * NSA_DH), BF16),
        scratch_shapes=[pltpu.VMEM(shape, F32) for shape in scratch],
        compiler_params=_params(
            ("arbitrary", "arbitrary"),
            [(a.shape[1:], a.dtype) for a in (kc4, vcT, ks, vsT, kw, vwT)]
            + [((NSA_HPG * NSA_DH, TQ), F32), ((TQ, NSA_HPG * NSA_DH), BF16)],
            [(shape, F32) for shape in scratch]),
        name="nsa",
    )(qT, gT, kc4, vcT, ks, vsT, kw, vwT)


def _ret_kernel(q_ref, k_ref, v_ref, o_ref, r_ref, dm_ref, qd_ref, kd_ref):
    C = RET_CHUNK
    log_g = [math.log(1.0 - 2.0 ** (-5.0 - h)) for h in range(RET_HEADS)]

    @pl.when(pl.program_id(0) == 0)
    def _():
        r_ref[...] = jnp.zeros(r_ref.shape, F32)
        diff = (lax.broadcasted_iota(jnp.int32, (C, C), 0)
                - lax.broadcasted_iota(jnp.int32, (C, C), 1)).astype(F32)
        n = lax.broadcasted_iota(jnp.int32, (C, RET_DK), 0).astype(F32)
        for h in range(RET_HEADS):
            dm_ref[h] = jnp.where(diff >= 0.0, jnp.exp(jnp.maximum(diff, 0.0) * log_g[h]), 0.0)
            qd_ref[h] = jnp.exp((n + 1.0) * log_g[h])
            kd_ref[h] = jnp.exp((C - 1.0 - n) * log_g[h])

    for h in range(RET_HEADS):
        q = q_ref[:, h * RET_DK:(h + 1) * RET_DK]
        k = k_ref[:, h * RET_DK:(h + 1) * RET_DK]
        v = v_ref[:, h * RET_DV:(h + 1) * RET_DV]
        att = lax.dot_general(q, k, (((1,), (1,)), ((), ())), preferred_element_type=F32)
        o = _dot((att * dm_ref[h]).astype(BF16), v)
        r = r_ref[h]
        qd = (q.astype(F32) * qd_ref[h]).astype(BF16)
        o = o + _dot(qd, r.astype(BF16))
        kd = (k.astype(F32) * kd_ref[h]).astype(BF16)
        r_ref[h] = math.exp(C * log_g[h]) * r + lax.dot_general(
            kd, v, (((0,), (0,)), ((), ())), preferred_element_type=F32)
        mu = jnp.mean(o, axis=-1, keepdims=True)
        d = o - mu
        var = jnp.mean(d * d, axis=-1, keepdims=True)
        o_ref[:, h * RET_DV:(h + 1) * RET_DV] = (d * lax.rsqrt(var + GN_EPS)).astype(BF16)


def _retention(qr, kr, vr):
    S = qr.shape[0]
    C = RET_CHUNK
    row = lambda w: pl.BlockSpec((C, w), lambda n: (n, 0))
    scratch = [(RET_HEADS, RET_DK, RET_DV),
               (RET_HEADS, C, C),
               (RET_HEADS, C, RET_DK), (RET_HEADS, C, RET_DK)]
    return pl.pallas_call(
        _ret_kernel,
        grid=(S // C,),
        in_specs=[row(RET_HEADS * RET_DK), row(RET_HEADS * RET_DK), row(RET_HEADS * RET_DV)],
        out_specs=row(RET_HEADS * RET_DV),
        out_shape=jax.ShapeDtypeStruct((S, RET_HEADS * RET_DV), BF16),
        scratch_shapes=[pltpu.VMEM(shape, F32) for shape in scratch],
        compiler_params=_params(
            ("arbitrary",),
            [((C, w), BF16) for w in (RET_HEADS * RET_DK, RET_HEADS * RET_DK, 2 * RET_HEADS * RET_DV)],
            [(shape, F32) for shape in scratch]),
        name="retention",
    )(qr, kr, vr)


def _post_kernel(x_ref, oa_ref, zn_ref, ob_ref, zr_ref, ma_ref, mb_ref,
                 wa_ref, wb_ref, wo_ref, g_ref, out_ref):
    f32 = lambda r: r[...].astype(F32)
    ya = _dot((f32(oa_ref) * f32(zn_ref)).astype(BF16), wa_ref[...])
    yb = _dot((f32(ob_ref) * f32(zr_ref)).astype(BF16), wb_ref[...])
    merged = f32(ma_ref) * ya + f32(mb_ref) * yb
    y = _dot(merged.astype(BF16), wo_ref[...])
    ms = jnp.mean(y * y, axis=-1, keepdims=True)
    out_ref[...] = x_ref[...] + y * lax.rsqrt(ms + RMS_EPS) * g_ref[...]


def _post(x2, oa, zn, ob, zr, ma, mb, wa, wb, wo, g_post):
    S = x2.shape[0]
    tm = POST_ROW_TILE
    row = lambda a: pl.BlockSpec((tm, a.shape[1]), lambda i: (i, 0))
    whole = lambda a: pl.BlockSpec(a.shape, lambda i: (0, 0))
    rows = (x2, oa, zn, ob, zr, ma, mb)
    consts = (wa, wb, wo, g_post)
    return pl.pallas_call(
        _post_kernel,
        grid=(S // tm,),
        in_specs=[row(a) for a in rows] + [whole(a) for a in consts],
        out_specs=row(x2),
        out_shape=jax.ShapeDtypeStruct(x2.shape, x2.dtype),
        compiler_params=_params(
            ("arbitrary",),
            [((tm, a.shape[1]), a.dtype) for a in rows + (x2,)] + [(a.shape, a.dtype) for a in consts]),
        name="post",
    )(*rows, *consts)


def _rotary_frequencies():
    nsa_inv = 1.0 / (ROPE_THETA ** (jnp.arange(0, ROPE_DIM, 2, dtype=F32) / ROPE_DIM))
    ret_inv = 1.0 / (RET_ROPE_BASE ** jnp.linspace(0.0, 1.0, RET_DK // 2, dtype=F32))
    row = jnp.concatenate([ret_inv, nsa_inv, jnp.zeros((LANE - RET_DK // 2 - ROPE_DIM // 2,), F32)])
    return jnp.pad(row[None, :], ((0, 7), (0, 0)))


def kernel(x, norm_pre, w_in, b_nsa_gate, cmp_pe_k, cmp_w1_k, cmp_w2_k, cmp_pe_v, cmp_w1_v,
           cmp_w2_v, w_nsa_o, w_ret_o, w_out, norm_post):
    B, S, _ = x.shape
    assert B == 1 and S % (2 * TK) == 0 and S >= WIN_KEYS
    assert S % (SEL_BLOCK * 8 * N_SIZE_CLASSES) == 0 and (S // TQ) % N_SIZE_CLASSES == 0
    x2 = x.reshape(S, D_MODEL)

    gate_cols = lambda a: jnp.pad(
        a.reshape(a.shape[:-1] + (NSA_GROUPS, NSA_HPG * 3)),
        [(0, 0)] * (a.ndim - 1) + [(0, 0), (0, GATE_ROWS - NSA_HPG * 3)]).reshape(a.shape[:-1] + (-1,))
    g0 = sum(PROJ_SIZES[:7])
    g1 = g0 + NSA_HEADS * 3
    gate_seg = gate_cols(w_in[:, g0:g1])
    gate_seg = jnp.pad(gate_seg, ((0, 0), (0, LANE - gate_seg.shape[1])))
    w_hi = jnp.concatenate([w_in[:, :g0], gate_seg, w_in[:, g1:]], axis=1).astype(BF16)
    bias = gate_cols(b_nsa_gate)
    bias = jnp.pad(bias, (0, LANE - bias.shape[0])).reshape(1, LANE)

    (qT, kc, vc, ks, vsT, kw, vwT, gT, zn, qr, kr, vr, zr, ma, mb) = _inproj(
        x2, norm_pre.reshape(1, D_MODEL), w_hi, bias, _rotary_frequencies())

    kc4 = _compress(kc, cmp_pe_k, cmp_w1_k, cmp_w2_k, keys=True)
    vcT = _compress(vc, cmp_pe_v, cmp_w1_v, cmp_w2_v, keys=False)

    oa = _nsa(qT, gT, kc4, vcT, ks, vsT, kw, vwT)
    ob = _retention(qr, kr, vr)

    out = _post(x2, oa, zn, ob, zr, ma, mb, w_nsa_o.astype(BF16), w_ret_o.astype(BF16),
                w_out.astype(BF16), norm_post.reshape(1, D_MODEL))
    return out.reshape(B, S, D_MODEL)
```

```python
import functools
import math

import jax
import jax.numpy as jnp
from jax import lax
from jax.experimental import pallas as pl
from jax.experimental.pallas import tpu as pltpu

F32 = jnp.float32
BF16 = jnp.bfloat16

D_MODEL = 1024
NSA_HEADS = 8
NSA_GROUPS = 2
NSA_HPG = NSA_HEADS // NSA_GROUPS
NSA_DH = 64
CMP_LEN = 32
CMP_STRIDE = 16
CMP_HIDDEN = 256
SEL_BLOCK = 64
SEL_TOPN = 16
WINDOW = 512
ROPE_THETA = 500000.0
ROPE_DIM = NSA_DH // 4
NSA_SCALE = NSA_DH ** -0.5
RET_HEADS = 4
RET_DK = 128
RET_DV = 256
RET_ROPE_BASE = 10000.0
RET_SCALE = RET_DK ** -0.5
RMS_EPS = 1e-6
GN_EPS = 1e-6
PROJ_SIZES = (512, 128, 128, 128, 128, 128, 128, 24, 512, 512, 512, 1024, 1024, 1024, 1024)

LANE = 128
V7X_VMEM_LIMIT_BYTES = 56 * 1024 * 1024
VMEM_TEMP_FLOOR_BYTES = 8 * 1024 * 1024

ROW_TILE = 256
POST_ROW_TILE = 512
TQ = 128
TK = 512
BLK_PER_TILE = TK // SEL_BLOCK
WIN_KEYS = WINDOW + TQ
RET_CHUNK = 256
V_ROWS = 80
NEG = -1e30
LOG2E = math.log2(math.e)
K_LANES = LANE
GATE_ROWS = 16
SC_PAD = 8
N_SIZE_CLASSES = 8

_PAD_SIZES = tuple(128 if s == 24 else s for s in PROJ_SIZES)
_OFF = [0]
for _s in _PAD_SIZES:
    _OFF.append(_OFF[-1] + _s)


def _dot(a, b):
    return jnp.dot(a, b, preferred_element_type=F32)


def _split(a):
    hi = a.astype(BF16)
    lo = (a - hi.astype(F32)).astype(BF16)
    return hi, lo


def _dot3(a_hi, a_lo, b_hi, b_lo):
    return _dot(a_hi, b_hi) + _dot(a_lo, b_hi) + _dot(a_hi, b_lo)


def _nbytes(shape, dtype):
    return math.prod(d for d in shape if d is not None) * jnp.dtype(dtype).itemsize


def _params(sem, blocks, scratch=()):
    est = 2 * sum(_nbytes(*b) for b in blocks) + sum(_nbytes(*b) for b in scratch)
    limit = min(V7X_VMEM_LIMIT_BYTES, est + est // 4 + VMEM_TEMP_FLOOR_BYTES)
    return pltpu.CompilerParams(dimension_semantics=sem, vmem_limit_bytes=limit)


def _inproj_kernel(x_ref, g_ref, whi_ref, bias_ref, freq_ref,
                   qT_ref, kc_ref, vc_ref, ks_ref, vsT_ref, kw_ref, vwT_ref, gT_ref,
                   zn_ref, qr_ref, kr_ref, vr_ref, zr_ref, ma_ref, mb_ref):
    tm = x_ref.shape[0]
    x = x_ref[...]
    ms = jnp.mean(x * x, axis=-1, keepdims=True)
    h = x * lax.rsqrt(ms + RMS_EPS) * g_ref[...]
    h_hi = h.astype(BF16)

    def mm(lo, hi):
        return _dot(h_hi, whi_ref[:, lo:hi])

    lane = lax.broadcasted_iota(jnp.int32, (tm, LANE), 1)
    row = pl.program_id(0) * tm + lax.broadcasted_iota(jnp.int32, (tm, LANE), 0)
    first_group = lane < NSA_DH
    pos = row.astype(F32)

    half = ROPE_DIM // 2
    dim = jnp.bitwise_and(lane, NSA_DH - 1)
    ang = pos * freq_ref[0:1, :]
    c_all, s_all = jnp.cos(ang), jnp.sin(ang)
    c_sw, s_sw = pltpu.roll(c_all, LANE // 2, 1), pltpu.roll(s_all, LANE // 2, 1)
    cr = jnp.where(first_group, c_all, c_sw)
    sr = jnp.where(first_group, -s_all, s_sw)
    lo_half, hi_half = dim < half, (dim >= half) & (dim < ROPE_DIM)
    c_lo, s_lo = jnp.where(first_group, c_sw, c_all), jnp.where(first_group, s_sw, s_all)
    c_hi, s_hi = pltpu.roll(c_lo, half, 1), pltpu.roll(s_lo, half, 1)
    cn = jnp.where(lo_half, c_lo, jnp.where(hi_half, c_hi, 1.0))
    san = jnp.where(hi_half, s_hi, 0.0)
    sbn = jnp.where(lo_half, -s_lo, 0.0)

    def rot_nsa(p):
        return p * cn + pltpu.roll(p, half, 1) * san + pltpu.roll(p, LANE - half, 1) * sbn

    def rot_ret(p):
        return p * cr + pltpu.roll(p, LANE // 2, 1) * sr

    def key_rows(p, fill):
        swapped = pltpu.roll(p, NSA_DH, 1)
        return [jnp.where(first_group, pg, fill).astype(BF16) for pg in (p, swapped)]

    ones_row = jnp.where(lax.broadcasted_iota(jnp.int32, (V_ROWS - NSA_DH, tm), 0) == 0, 1.0, 0.0)

    def value_rows(p):
        pT = p.T
        return [jnp.concatenate([pT[g * NSA_DH:(g + 1) * NSA_DH, :], ones_row], axis=0).astype(BF16)
                for g in range(NSA_GROUPS)]

    p = mm(0, _OFF[2])
    for c in range(4):
        qT_ref[c * LANE:(c + 1) * LANE, :] = (rot_nsa(p[:, c * LANE:(c + 1) * LANE]) * NSA_SCALE).T
    kc_ref[...] = rot_nsa(p[:, 4 * LANE:5 * LANE])

    p = mm(_OFF[2], _OFF[8])
    vc_ref[...] = p[:, 0 * LANE:1 * LANE]
    blk_in_tile = jnp.right_shift(jnp.bitwise_and(row, TK - 1), SEL_BLOCK.bit_length() - 1)
    onehot = jnp.where(lane - NSA_DH == blk_in_tile, 1.0, 0.0)
    ks = key_rows(rot_nsa(p[:, 1 * LANE:2 * LANE]), onehot)
    vs = value_rows(p[:, 2 * LANE:3 * LANE])
    kw = key_rows(rot_nsa(p[:, 3 * LANE:4 * LANE]), 0.0)
    vw = value_rows(p[:, 4 * LANE:5 * LANE])
    gT = jax.nn.sigmoid(p[:, 5 * LANE:6 * LANE] + bias_ref[...]).T
    for g in range(NSA_GROUPS):
        ks_ref[g], vsT_ref[g], kw_ref[g], vwT_ref[g] = ks[g], vs[g], kw[g], vw[g]
        gT_ref[g] = gT[g * GATE_ROWS:(g + 1) * GATE_ROWS, :]

    z = mm(_OFF[8], _OFF[9])
    zn_ref[...] = (z * jax.nn.sigmoid(z)).astype(BF16)

    p = mm(_OFF[9], _OFF[11])
    for c in range(4):
        qr_ref[:, c * LANE:(c + 1) * LANE] = rot_ret(p[:, c * LANE:(c + 1) * LANE]).astype(BF16)
    for c in range(4):
        kr_ref[:, c * LANE:(c + 1) * LANE] = (
            rot_ret(p[:, (4 + c) * LANE:(5 + c) * LANE]) * RET_SCALE).astype(BF16)

    vr_ref[...] = mm(_OFF[11], _OFF[12]).astype(BF16)
    z = mm(_OFF[12], _OFF[13])
    zr_ref[...] = (z * jax.nn.sigmoid(z)).astype(BF16)
    ma_ref[...] = jax.nn.sigmoid(mm(_OFF[13], _OFF[14])).astype(BF16)
    mb_ref[...] = jax.nn.sigmoid(mm(_OFF[14], _OFF[15])).astype(BF16)


def _inproj(x2, norm_pre, w_hi, bias, freqs):
    S = x2.shape[0]
    tm = ROW_TILE
    G = NSA_GROUPS
    row = lambda w: pl.BlockSpec((tm, w), lambda i: (i, 0))
    whole = lambda a: pl.BlockSpec(a.shape, lambda i: (0,) * a.ndim)
    col = lambda r: pl.BlockSpec((r, tm), lambda i: (0, i))
    g_row = lambda w: pl.BlockSpec((G, tm, w), lambda i: (0, i, 0))
    g_col = lambda r: pl.BlockSpec((G, r, tm), lambda i: (0, 0, i))
    sds = jax.ShapeDtypeStruct
    outs = [
        (col(NSA_HEADS * NSA_DH), sds((NSA_HEADS * NSA_DH, S), F32)),
        (row(LANE), sds((S, LANE), F32)),
        (row(LANE), sds((S, LANE), F32)),
        (g_row(K_LANES), sds((G, S, K_LANES), BF16)),
        (g_col(V_ROWS), sds((G, V_ROWS, S), BF16)),
        (g_row(K_LANES), sds((G, S, K_LANES), BF16)),
        (g_col(V_ROWS), sds((G, V_ROWS, S), BF16)),
        (g_col(GATE_ROWS), sds((G, GATE_ROWS, S), F32)),
        (row(512), sds((S, 512), BF16)),
        (row(512), sds((S, 512), BF16)), (row(512), sds((S, 512), BF16)),
        (row(1024), sds((S, 1024), BF16)), (row(1024), sds((S, 1024), BF16)),
        (row(1024), sds((S, 1024), BF16)), (row(1024), sds((S, 1024), BF16)),
    ]
    return pl.pallas_call(
        _inproj_kernel,
        grid=(S // tm,),
        in_specs=[row(D_MODEL), whole(norm_pre), whole(w_hi), whole(bias), whole(freqs)],
        out_specs=[o[0] for o in outs],
        out_shape=[o[1] for o in outs],
        compiler_params=_params(
            ("arbitrary",),
            [((tm, D_MODEL), F32), (w_hi.shape, BF16)] + [(o[0].block_shape, o[1].dtype) for o in outs]),
        name="inproj",
    )(x2, norm_pre, w_hi, bias, freqs)


def _compress_kernel(x_ref, pe_ref, w1_ref, w2_ref, o_ref, *, keys):
    NC = x_ref.shape[0]
    g = pl.program_id(0)
    first_half = lax.broadcasted_iota(jnp.int32, (NC, LANE), 1) < NSA_DH
    tokens = []
    for u in range(CMP_STRIDE // 2):
        a, b = x_ref[:, (2 * u) * LANE:(2 * u + 1) * LANE], x_ref[:, (2 * u + 1) * LANE:(2 * u + 2) * LANE]
        ar, br = pltpu.roll(a, NSA_DH, 1), pltpu.roll(b, NSA_DH, 1)
        tokens.append(jnp.where(first_half, jnp.where(g == 0, a, ar), jnp.where(g == 0, br, b)))
    xg = jnp.concatenate(tokens, axis=1)
    prod = lambda a, b: _dot3(*_split(a), *_split(b))
    half = xg.shape[1]
    first = prod(xg, w1_ref[0:half, :])
    second = prod(xg, w1_ref[half:2 * half, :])
    nxt = pltpu.roll(second, NC - 1, 0)
    pe_term = prod(pe_ref[...], w1_ref[...])[0:1, :]
    hid = first + nxt + pe_term
    act = hid * (0.5 * (1.0 + jnp.tanh(math.sqrt(2.0 / math.pi) * (hid + 0.044715 * (hid * hid * hid)))))
    out = prod(act, w2_ref[...])
    if keys:
        hi, lo = _split(out)
        o_ref[...] = jnp.concatenate([hi, hi, lo, lo], axis=1)
    else:
        ones_row = jnp.where(lax.broadcasted_iota(jnp.int32, (V_ROWS - NSA_DH, NC), 0) == 0, 1.0, 0.0)
        o_ref[...] = jnp.concatenate([out.T, ones_row], axis=0).astype(BF16)


def _compress(raw, pe, w1, w2, keys):
    S = raw.shape[0]
    NC = S // CMP_STRIDE
    G = NSA_GROUPS
    half = CMP_STRIDE * NSA_DH
    x = raw.reshape(NC, CMP_STRIDE * G * NSA_DH)
    pe8 = jnp.pad(pe.reshape(1, 2 * half), ((0, 7), (0, 0)))
    if keys:
        out_spec = pl.BlockSpec((None, NC, 4 * NSA_DH), lambda g: (g, 0, 0))
        out_shape = jax.ShapeDtypeStruct((G, NC, 4 * NSA_DH), BF16)
    else:
        out_spec = pl.BlockSpec((None, V_ROWS, NC), lambda g: (g, 0, 0))
        out_shape = jax.ShapeDtypeStruct((G, V_ROWS, NC), BF16)
    whole = lambda a: pl.BlockSpec(a.shape, lambda g: (0,) * a.ndim)
    return pl.pallas_call(
        functools.partial(_compress_kernel, keys=keys),
        grid=(G,),
        in_specs=[whole(x), whole(pe8), whole(w1), whole(w2)],
        out_specs=out_spec,
        out_shape=out_shape,
        compiler_params=_params(
            ("arbitrary",),
            [(x.shape, F32), (w1.shape, F32), (w2.shape, F32), (out_shape.shape[1:], BF16)],
            [((NC, 2 * CMP_HIDDEN), F32), ((NC, half), F32)]),
        name="compress_k" if keys else "compress_v",
    )(x, pe8, w1, w2)


def _nsa_kernel(qT_ref, gT_ref, kc4_ref, vcT_ref, ks_ref, vsT_ref, kw_ref, vwT_ref, o_ref,
                sb_ref, sa_ref, sbuf_ref, sc_ref, oc_ref, ow_ref, m_ref, acc_ref):
    i = pl.program_id(1)
    NC = kc4_ref.shape[0]
    NB = sb_ref.shape[0]
    n_q = pl.num_programs(1)
    heads = range(NSA_HPG)
    lanes = lambda a, h: a[:, h * TQ:(h + 1) * TQ]
    colmax = lambda a: jnp.max(a, axis=0, keepdims=True)
    colsum = lambda a: jnp.sum(a, axis=0, keepdims=True)

    qT = qT_ref[...] * LOG2E
    Q = jnp.concatenate([qT[h * NSA_DH:(h + 1) * NSA_DH, :] for h in heads], axis=1)
    q_hi, q_lo = _split(Q)
    q4 = jnp.concatenate([q_hi, q_lo, q_hi, q_lo], axis=0)
    t = i * TQ + lax.broadcasted_iota(jnp.int32, (1, TQ), 1)
    cur = jnp.right_shift(t, SEL_BLOCK.bit_length() - 1)

    def window_branch():
        w0 = pl.multiple_of(jnp.maximum(i * TQ - WINDOW, 0), TQ)
        q_pad = jnp.concatenate([q_hi, jnp.zeros((K_LANES - NSA_DH, NSA_HPG * TQ), BF16)], axis=0)
        s = _dot(kw_ref[pl.ds(w0, WIN_KEYS), :], q_pad)
        key = w0 + lax.broadcasted_iota(jnp.int32, (WIN_KEYS, TQ), 0)
        mask_w = (key <= t) & (key > t - WINDOW)
        p_heads = []
        for h in heads:
            sm = jnp.where(mask_w, lanes(s, h), NEG)
            p_heads.append(jnp.exp2(sm - colmax(sm)).astype(BF16))
        accw = _dot(vwT_ref[:, pl.ds(w0, WIN_KEYS)], jnp.concatenate(p_heads, axis=1))
        owT = accw[0:NSA_DH, :] * (1.0 / accw[NSA_DH:NSA_DH + 1, :])
        ow_ref[...] = owT

    def compressed_and_select(ncp, nbp):
        s = _dot(kc4_ref[0:ncp, :], q4)
        c_end = lax.broadcasted_iota(jnp.int32, (ncp, TQ), 0) * CMP_STRIDE + (CMP_LEN - 1)
        mask_c = c_end <= t
        p_heads, inv = [], []
        for h in heads:
            sm = jnp.where(mask_c, lanes(s, h), -jnp.inf)
            m = colmax(sm)
            m = jnp.where(m == -jnp.inf, 0.0, m)
            p = jnp.exp2(sm - m)
            inv.append(1.0 / jnp.maximum(colsum(p), jnp.finfo(F32).tiny))
            sc_ref[h, 0:SC_PAD, :] = jnp.zeros((SC_PAD, TQ), F32)
            sc_ref[h, SC_PAD:SC_PAD + ncp, :] = p
            p_heads.append(p.astype(BF16))
        oc_ref[...] = (_dot(vcT_ref[:, 0:ncp], jnp.concatenate(p_heads, axis=1))
                       * jnp.concatenate(inv, axis=1))
        per_sel = SEL_BLOCK // CMP_STRIDE
        imp = jnp.zeros((nbp, TQ), F32)
        for h in heads:
            tot = sc_ref[h, pl.ds(SC_PAD - 1, nbp, stride=per_sel), :]
            for r in range(per_sel):
                tot = tot + sc_ref[h, pl.ds(SC_PAD + r, nbp, stride=per_sel), :]
            imp = imp + tot * inv[h]
        blk = lax.broadcasted_iota(jnp.int32, (nbp, TQ), 0)
        blk_f = blk.astype(F32)
        valid = blk <= cur
        forced = (blk == 0) | (blk == cur) | (blk == cur - 1)
        free = valid & jnp.logical_not(forced)
        score = jnp.where(free, imp, -jnp.inf)
        for _ in range(min(SEL_TOPN, NB) - 3):
            mx = colmax(score)
            idx = jnp.min(jnp.where(score == mx, blk_f, float(NB)), axis=0, keepdims=True)
            score = jnp.where(blk_f == idx, -jnp.inf, score)
        picked = valid & (forced | (score == -jnp.inf))
        sb_ref[0:nbp, :] = jnp.where(picked, 0.0, NEG)
        if nbp < NB:
            sb_ref[nbp:NB, :] = jnp.full((NB - nbp, TQ), NEG, F32)
        window_branch()

    size_class = lax.div(i * N_SIZE_CLASSES, n_q)
    for k in range(N_SIZE_CLASSES):
        pl.when(size_class == k)(functools.partial(
            compressed_and_select, NC * (k + 1) // N_SIZE_CLASSES, NB * (k + 1) // N_SIZE_CLASSES))

    pad_rows = jnp.zeros((K_LANES - NSA_DH - 2 * BLK_PER_TILE, NSA_HPG * TQ), BF16)

    def scores(j):
        k0 = pl.multiple_of(j * TK, TK)
        sbt = sb_ref[pl.ds(pl.multiple_of(j * BLK_PER_TILE, BLK_PER_TILE), BLK_PER_TILE), :]
        rows = jnp.concatenate([sbt, jnp.zeros_like(sbt)], axis=0)
        rows = jnp.concatenate([rows] * NSA_HPG, axis=1).astype(BF16)
        w = jnp.concatenate([q_hi, rows, pad_rows], axis=0)
        return _dot(ks_ref[pl.ds(k0, TK), :], w)

    def attend(j, s, carry, causal):
        m, acc = carry
        k0 = pl.multiple_of(j * TK, TK)
        if causal:
            keep = k0 + lax.broadcasted_iota(jnp.int32, (TK, TQ), 0) <= t
            s = jnp.concatenate([jnp.where(keep, lanes(s, h), NEG) for h in heads], axis=1)
        m_new = jnp.maximum(m, colmax(s))
        alpha = jnp.exp2(m - m_new)
        p = jnp.exp2(s - m_new).astype(BF16)
        acc = acc * alpha + _dot(vsT_ref[:, pl.ds(k0, TK)], p)
        return m_new, acc

    jd = lax.div(i, TK // TQ)
    n_pairs = lax.div(jd, 2)
    sa_ref[...] = scores(0)

    def pair(pp, carry):
        sbuf_ref[...] = scores(2 * pp + 1)
        carry = attend(2 * pp, sa_ref[...], carry, False)
        sa_ref[...] = scores(2 * pp + 2)
        return attend(2 * pp + 1, sbuf_ref[...], carry, False)

    carry = (jnp.full((1, NSA_HPG * TQ), NEG, F32), jnp.zeros((V_ROWS, NSA_HPG * TQ), F32))
    quad = lambda qq, c: pair(2 * qq + 1, pair(2 * qq, c))
    octo = lambda oo, c: quad(2 * oo + 1, quad(2 * oo, c))
    n_hexs, n_octs, n_quads = lax.div(n_pairs, 8), lax.div(n_pairs, 4), lax.div(n_pairs, 2)
    carry = lax.fori_loop(0, n_hexs, lambda hh, c: octo(2 * hh + 1, octo(2 * hh, c)), carry)
    carry = lax.fori_loop(2 * n_hexs, n_octs, octo, carry)
    carry = lax.fori_loop(2 * n_octs, n_quads, quad, carry)
    carry = lax.fori_loop(2 * n_quads, n_pairs, pair, carry)
    sbuf_ref[...] = scores(2 * n_pairs + 1)

    m_ref[...], acc_ref[...] = attend(2 * n_pairs, sa_ref[...], carry, True)

    @pl.when(jd != 2 * n_pairs)
    def _():
        m_ref[...], acc_ref[...] = attend(
            2 * n_pairs + 1, sbuf_ref[...], (m_ref[...], acc_ref[...]), True)

    acc = acc_ref[...]
    osT = acc[0:NSA_DH, :] * (1.0 / acc[NSA_DH:NSA_DH + 1, :])

    gT = gT_ref[...]
    ocT, owT = oc_ref[...], ow_ref[...]
    outs = []
    for h in heads:
        g_c, g_s, g_w = (gT[3 * h + b:3 * h + b + 1, :] for b in range(3))
        outs.append(g_c * lanes(ocT, h)[0:NSA_DH, :] + g_s * lanes(osT, h) + g_w * lanes(owT, h))
    o_ref[...] = jnp.concatenate(outs, axis=0).T.astype(BF16)


def _nsa(qT, gT, kc4, vcT, ks, vsT, kw, vwT):
    G, S = ks.shape[0], ks.shape[1]
    NC, NB = kc4.shape[1], S // SEL_BLOCK
    per_g = lambda a: pl.BlockSpec((None,) + a.shape[1:], lambda g, i: (g,) + (0,) * (a.ndim - 1))
    wide = NSA_HPG * TQ
    scratch = [(NB, TQ),
               (TK, wide), (TK, wide),
               (NSA_HPG, SC_PAD + NC, TQ),
               (V_ROWS, wide), (NSA_DH, wide),
               (1, wide), (V_ROWS, wide)]
    return pl.pallas_call(
        _nsa_kernel,
        grid=(G, S // TQ),
        in_specs=[
            pl.BlockSpec((NSA_HPG * NSA_DH, TQ), lambda g, i: (g, i)),
            pl.BlockSpec((None, GATE_ROWS, TQ), lambda g, i: (g, 0, i)),
            per_g(kc4), per_g(vcT), per_g(ks), per_g(vsT), per_g(kw), per_g(vwT),
        ],
        out_specs=pl.BlockSpec((TQ, NSA_HPG * NSA_DH), lambda g, i: (i, g)),
        out_shape=jax.ShapeDtypeStruct((S, NSA_HEADS * NSA_DH), BF16),
        scratch_shapes=[pltpu.VMEM(shape, F32) for shape in scratch],
        compiler_params=_params(
            ("arbitrary", "arbitrary"),
            [(a.shape[1:], a.dtype) for a in (kc4, vcT, ks, vsT, kw, vwT)]
            + [((NSA_HPG * NSA_DH, TQ), F32), ((TQ, NSA_HPG * NSA_DH), BF16)],
            [(shape, F32) for shape in scratch]),
        name="nsa",
    )(qT, gT, kc4, vcT, ks, vsT, kw, vwT)


def _ret_kernel(q_ref, k_ref, v_ref, o_ref, r_ref, dm_ref, qd_ref, kd_ref):
    C = RET_CHUNK
    log_g = [math.log(1.0 - 2.0 ** (-5.0 - h)) for h in range(RET_HEADS)]

    @pl.when(pl.program_id(0) == 0)
    def _():
        r_ref[...] = jnp.zeros(r_ref.shape, F32)
        diff = (lax.broadcasted_iota(jnp.int32, (C, C), 0)
                - lax.broadcasted_iota(jnp.int32, (C, C), 1)).astype(F32)
        n = lax.broadcasted_iota(jnp.int32, (C, RET_DK), 0).astype(F32)
        for h in range(RET_HEADS):
            dm_ref[h] = jnp.where(diff >= 0.0, jnp.exp(jnp.maximum(diff, 0.0) * log_g[h]), 0.0)
            qd_ref[h] = jnp.exp((n + 1.0) * log_g[h])
            kd_ref[h] = jnp.exp((C - 1.0 - n) * log_g[h])

    for h in range(RET_HEADS):
        q = q_ref[:, h * RET_DK:(h + 1) * RET_DK]
        k = k_ref[:, h * RET_DK:(h + 1) * RET_DK]
        v = v_ref[:, h * RET_DV:(h + 1) * RET_DV]
        att = lax.dot_general(q, k, (((1,), (1,)), ((), ())), preferred_element_type=F32)
        o = _dot((att * dm_ref[h]).astype(BF16), v)
        r = r_ref[h]
        qd = (q.astype(F32) * qd_ref[h]).astype(BF16)
        o = o + _dot(qd, r.astype(BF16))
        kd = (k.astype(F32) * kd_ref[h]).astype(BF16)
        r_ref[h] = math.exp(C * log_g[h]) * r + lax.dot_general(
            kd, v, (((0,), (0,)), ((), ())), preferred_element_type=F32)
        mu = jnp.mean(o, axis=-1, keepdims=True)
        d = o - mu
        var = jnp.mean(d * d, axis=-1, keepdims=True)
        o_ref[:, h * RET_DV:(h + 1) * RET_DV] = (d * lax.rsqrt(var + GN_EPS)).astype(BF16)


def _retention(qr, kr, vr):
    S = qr.shape[0]
    C = RET_CHUNK
    row = lambda w: pl.BlockSpec((C, w), lambda n: (n, 0))
    scratch = [(RET_HEADS, RET_DK, RET_DV),
               (RET_HEADS, C, C),
               (RET_HEADS, C, RET_DK), (RET_HEADS, C, RET_DK)]
    return pl.pallas_call(
        _ret_kernel,
        grid=(S // C,),
        in_specs=[row(RET_HEADS * RET_DK), row(RET_HEADS * RET_DK), row(RET_HEADS * RET_DV)],
        out_specs=row(RET_HEADS * RET_DV),
        out_shape=jax.ShapeDtypeStruct((S, RET_HEADS * RET_DV), BF16),
        scratch_shapes=[pltpu.VMEM(shape, F32) for shape in scratch],
        compiler_params=_params(
            ("arbitrary",),
            [((C, w), BF16) for w in (RET_HEADS * RET_DK, RET_HEADS * RET_DK, 2 * RET_HEADS * RET_DV)],
            [(shape, F32) for shape in scratch]),
        name="retention",
    )(qr, kr, vr)


def _post_kernel(x_ref, oa_ref, zn_ref, ob_ref, zr_ref, ma_ref, mb_ref,
                 wa_ref, wb_ref, wo_ref, g_ref, out_ref):
    f32 = lambda r: r[...].astype(F32)
    ya = _dot((f32(oa_ref) * f32(zn_ref)).astype(BF16), wa_ref[...])
    yb = _dot((f32(ob_ref) * f32(zr_ref)).astype(BF16), wb_ref[...])
    merged = f32(ma_ref) * ya + f32(mb_ref) * yb
    y = _dot(merged.astype(BF16), wo_ref[...])
    ms = jnp.mean(y * y, axis=-1, keepdims=True)
    out_ref[...] = x_ref[...] + y * lax.rsqrt(ms + RMS_EPS) * g_ref[...]


def _post(x2, oa, zn, ob, zr, ma, mb, wa, wb, wo, g_post):
    S = x2.shape[0]
    tm = POST_ROW_TILE
    row = lambda a: pl.BlockSpec((tm, a.shape[1]), lambda i: (i, 0))
    whole = lambda a: pl.BlockSpec(a.shape, lambda i: (0, 0))
    rows = (x2, oa, zn, ob, zr, ma, mb)
    consts = (wa, wb, wo, g_post)
    return pl.pallas_call(
        _post_kernel,
        grid=(S // tm,),
        in_specs=[row(a) for a in rows] + [whole(a) for a in consts],
        out_specs=row(x2),
        out_shape=jax.ShapeDtypeStruct(x2.shape, x2.dtype),
        compiler_params=_params(
            ("arbitrary",),
            [((tm, a.shape[1]), a.dtype) for a in rows + (x2,)] + [(a.shape, a.dtype) for a in consts]),
        name="post",
    )(*rows, *consts)


def _rotary_frequencies():
    nsa_inv = 1.0 / (ROPE_THETA ** (jnp.arange(0, ROPE_DIM, 2, dtype=F32) / ROPE_DIM))
    ret_inv = 1.0 / (RET_ROPE_BASE ** jnp.linspace(0.0, 1.0, RET_DK // 2, dtype=F32))
    row = jnp.concatenate([ret_inv, nsa_inv, jnp.zeros((LANE - RET_DK // 2 - ROPE_DIM // 2,), F32)])
    return jnp.pad(row[None, :], ((0, 7), (0, 0)))


def kernel(x, norm_pre, w_in, b_nsa_gate, cmp_pe_k, cmp_w1_k, cmp_w2_k, cmp_pe_v, cmp_w1_v,
           cmp_w2_v, w_nsa_o, w_ret_o, w_out, norm_post):
    B, S, _ = x.shape
    assert B == 1 and S % (2 * TK) == 0 and S >= WIN_KEYS
    assert S % (SEL_BLOCK * 8 * N_SIZE_CLASSES) == 0 and (S // TQ) % N_SIZE_CLASSES == 0
    x2 = x.reshape(S, D_MODEL)

    gate_cols = lambda a: jnp.pad(
        a.reshape(a.shape[:-1] + (NSA_GROUPS, NSA_HPG * 3)),
        [(0, 0)] * (a.ndim - 1) + [(0, 0), (0, GATE_ROWS - NSA_HPG * 3)]).reshape(a.shape[:-1] + (-1,))
    g0 = sum(PROJ_SIZES[:7])
    g1 = g0 + NSA_HEADS * 3
    gate_seg = gate_cols(w_in[:, g0:g1])
    gate_seg = jnp.pad(gate_seg, ((0, 0), (0, LANE - gate_seg.shape[1])))
    w_hi = jnp.concatenate([w_in[:, :g0], gate_seg, w_in[:, g1:]], axis=1).astype(BF16)
    bias = gate_cols(b_nsa_gate)
    bias = jnp.pad(bias, (0, LANE - bias.shape[0])).reshape(1, LANE)

    (qT, kc, vc, ks, vsT, kw, vwT, gT, zn, qr, kr, vr, zr, ma, mb) = _inproj(
        x2, norm_pre.reshape(1, D_MODEL), w_hi, bias, _rotary_frequencies())

    kc4 = _compress(kc, cmp_pe_k, cmp_w1_k, cmp_w2_k, keys=True)
    vcT = _compress(vc, cmp_pe_v, cmp_w1_v, cmp_w2_v, keys=False)

    oa = _nsa(qT, gT, kc4, vcT, ks, vsT, kw, vwT)
    ob = _retention(qr, kr, vr)

    out = _post(x2, oa, zn, ob, zr, ma, mb, w_nsa_o.astype(BF16), w_ret_o.astype(BF16),
                w_out.astype(BF16), norm_post.reshape(1, D_MODEL))
    return out.reshape(B, S, D_MODEL)
```

```python
import functools
import math

import jax
import jax.numpy as jnp
from jax import lax
from jax.experimental import pallas as pl
from jax.experimental.pallas import tpu as pltpu

F32 = jnp.float32
BF16 = jnp.bfloat16

D_MODEL = 1024
NSA_HEADS = 8
NSA_GROUPS = 2
NSA_HPG = NSA_HEADS // NSA_GROUPS
NSA_DH = 64
CMP_LEN = 32
CMP_STRIDE = 16
CMP_HIDDEN = 256
SEL_BLOCK = 64
SEL_TOPN = 16
WINDOW = 512
ROPE_THETA = 500000.0
ROPE_DIM = NSA_DH // 4
NSA_SCALE = NSA_DH ** -0.5
RET_HEADS = 4
RET_DK = 128
RET_DV = 256
RET_ROPE_BASE = 10000.0
RET_SCALE = RET_DK ** -0.5
RMS_EPS = 1e-6
GN_EPS = 1e-6
PROJ_SIZES = (512, 128, 128, 128, 128, 128, 128, 24, 512, 512, 512, 1024, 1024, 1024, 1024)

LANE = 128
V7X_VMEM_LIMIT_BYTES = 56 * 1024 * 1024
VMEM_TEMP_FLOOR_BYTES = 8 * 1024 * 1024

ROW_TILE = 256
POST_ROW_TILE = 512
TQ = 128
TK = 512
BLK_PER_TILE = TK // SEL_BLOCK
WIN_KEYS = WINDOW + TQ
RET_CHUNK = 256
V_ROWS = 80
NEG = -1e30
LOG2E = math.log2(math.e)
K_LANES = LANE
GATE_ROWS = 16
SC_PAD = 8
N_SIZE_CLASSES = 8

_PAD_SIZES = tuple(128 if s == 24 else s for s in PROJ_SIZES)
_OFF = [0]
for _s in _PAD_SIZES:
    _OFF.append(_OFF[-1] + _s)


def _dot(a, b):
    return jnp.dot(a, b, preferred_element_type=F32)


def _split(a):
    hi = a.astype(BF16)
    lo = (a - hi.astype(F32)).astype(BF16)
    return hi, lo


def _dot3(a_hi, a_lo, b_hi, b_lo):
    return _dot(a_hi, b_hi) + _dot(a_lo, b_hi) + _dot(a_hi, b_lo)


def _nbytes(shape, dtype):
    return math.prod(d for d in shape if d is not None) * jnp.dtype(dtype).itemsize


def _params(sem, blocks, scratch=()):
    est = 2 * sum(_nbytes(*b) for b in blocks) + sum(_nbytes(*b) for b in scratch)
    limit = min(V7X_VMEM_LIMIT_BYTES, est + est // 4 + VMEM_TEMP_FLOOR_BYTES)
    return pltpu.CompilerParams(dimension_semantics=sem, vmem_limit_bytes=limit)


def _inproj_kernel(x_ref, g_ref, whi_ref, bias_ref, freq_ref,
                   qT_ref, kc_ref, vc_ref, ks_ref, vsT_ref, kw_ref, vwT_ref, gT_ref,
                   zn_ref, qr_ref, kr_ref, vr_ref, zr_ref, ma_ref, mb_ref):
    tm = x_ref.shape[0]
    x = x_ref[...]
    ms = jnp.mean(x * x, axis=-1, keepdims=True)
    h = x * lax.rsqrt(ms + RMS_EPS) * g_ref[...]
    h_hi = h.astype(BF16)

    def mm(lo, hi):
        return _dot(h_hi, whi_ref[:, lo:hi])

    lane = lax.broadcasted_iota(jnp.int32, (tm, LANE), 1)
    row = pl.program_id(0) * tm + lax.broadcasted_iota(jnp.int32, (tm, LANE), 0)
    first_group = lane < NSA_DH
    pos = row.astype(F32)

    half = ROPE_DIM // 2
    dim = jnp.bitwise_and(lane, NSA_DH - 1)
    ang = pos * freq_ref[0:1, :]
    c_all, s_all = jnp.cos(ang), jnp.sin(ang)
    c_sw, s_sw = pltpu.roll(c_all, LANE // 2, 1), pltpu.roll(s_all, LANE // 2, 1)
    cr = jnp.where(first_group, c_all, c_sw)
    sr = jnp.where(first_group, -s_all, s_sw)
    lo_half, hi_half = dim < half, (dim >= half) & (dim < ROPE_DIM)
    c_lo, s_lo = jnp.where(first_group, c_sw, c_all), jnp.where(first_group, s_sw, s_all)
    c_hi, s_hi = pltpu.roll(c_lo, half, 1), pltpu.roll(s_lo, half, 1)
    cn = jnp.where(lo_half, c_lo, jnp.where(hi_half, c_hi, 1.0))
    san = jnp.where(hi_half, s_hi, 0.0)
    sbn = jnp.where(lo_half, -s_lo, 0.0)

    def rot_nsa(p):
        return p * cn + pltpu.roll(p, half, 1) * san + pltpu.roll(p, LANE - half, 1) * sbn

    def rot_ret(p):
        return p * cr + pltpu.roll(p, LANE // 2, 1) * sr

    def key_rows(p, fill):
        swapped = pltpu.roll(p, NSA_DH, 1)
        return [jnp.where(first_group, pg, fill).astype(BF16) for pg in (p, swapped)]

    ones_row = jnp.where(lax.broadcasted_iota(jnp.int32, (V_ROWS - NSA_DH, tm), 0) == 0, 1.0, 0.0)

    def value_rows(p):
        pT = p.T
        return [jnp.concatenate([pT[g * NSA_DH:(g + 1) * NSA_DH, :], ones_row], axis=0).astype(BF16)
                for g in range(NSA_GROUPS)]

    p = mm(0, _OFF[2])
    for c in range(4):
        qT_ref[c * LANE:(c + 1) * LANE, :] = (rot_nsa(p[:, c * LANE:(c + 1) * LANE]) * NSA_SCALE).T
    kc_ref[...] = rot_nsa(p[:, 4 * LANE:5 * LANE])

    p = mm(_OFF[2], _OFF[8])
    vc_ref[...] = p[:, 0 * LANE:1 * LANE]
    blk_in_tile = jnp.right_shift(jnp.bitwise_and(row, TK - 1), SEL_BLOCK.bit_length() - 1)
    onehot = jnp.where(lane - NSA_DH == blk_in_tile, 1.0, 0.0)
    ks = key_rows(rot_nsa(p[:, 1 * LANE:2 * LANE]), onehot)
    vs = value_rows(p[:, 2 * LANE:3 * LANE])
    kw = key_rows(rot_nsa(p[:, 3 * LANE:4 * LANE]), 0.0)
    vw = value_rows(p[:, 4 * LANE:5 * LANE])
    gT = jax.nn.sigmoid(p[:, 5 * LANE:6 * LANE] + bias_ref[...]).T
    for g in range(NSA_GROUPS):
        ks_ref[g], vsT_ref[g], kw_ref[g], vwT_ref[g] = ks[g], vs[g], kw[g], vw[g]
        gT_ref[g] = gT[g * GATE_ROWS:(g + 1) * GATE_ROWS, :]

    z = mm(_OFF[8], _OFF[9])
    zn_ref[...] = (z * jax.nn.sigmoid(z)).astype(BF16)

    p = mm(_OFF[9], _OFF[11])
    for c in range(4):
        qr_ref[:, c * LANE:(c + 1) * LANE] = rot_ret(p[:, c * LANE:(c + 1) * LANE]).astype(BF16)
    for c in range(4):
        kr_ref[:, c * LANE:(c + 1) * LANE] = (
            rot_ret(p[:, (4 + c) * LANE:(5 + c) * LANE]) * RET_SCALE).astype(BF16)

    vr_ref[...] = mm(_OFF[11], _OFF[12]).astype(BF16)
    z = mm(_OFF[12], _OFF[13])
    zr_ref[...] = (z * jax.nn.sigmoid(z)).astype(BF16)
    ma_ref[...] = jax.nn.sigmoid(mm(_OFF[13], _OFF[14])).astype(BF16)
    mb_ref[...] = jax.nn.sigmoid(mm(_OFF[14], _OFF[15])).astype(BF16)


def _inproj(x2, norm_pre, w_hi, bias, freqs):
    S = x2.shape[0]
    tm = ROW_TILE
    G = NSA_GROUPS
    row = lambda w: pl.BlockSpec((tm, w), lambda i: (i, 0))
    whole = lambda a: pl.BlockSpec(a.shape, lambda i: (0,) * a.ndim)
    col = lambda r: pl.BlockSpec((r, tm), lambda i: (0, i))
    g_row = lambda w: pl.BlockSpec((G, tm, w), lambda i: (0, i, 0))
    g_col = lambda r: pl.BlockSpec((G, r, tm), lambda i: (0, 0, i))
    sds = jax.ShapeDtypeStruct
    outs = [
        (col(NSA_HEADS * NSA_DH), sds((NSA_HEADS * NSA_DH, S), F32)),
        (row(LANE), sds((S, LANE), F32)),
        (row(LANE), sds((S, LANE), F32)),
        (g_row(K_LANES), sds((G, S, K_LANES), BF16)),
        (g_col(V_ROWS), sds((G, V_ROWS, S), BF16)),
        (g_row(K_LANES), sds((G, S, K_LANES), BF16)),
        (g_col(V_ROWS), sds((G, V_ROWS, S), BF16)),
        (g_col(GATE_ROWS), sds((G, GATE_ROWS, S), F32)),
        (row(512), sds((S, 512), BF16)),
        (row(512), sds((S, 512), BF16)), (row(512), sds((S, 512), BF16)),
        (row(1024), sds((S, 1024), BF16)), (row(1024), sds((S, 1024), BF16)),
        (row(1024), sds((S, 1024), BF16)), (row(1024), sds((S, 1024), BF16)),
    ]
    return pl.pallas_call(
        _inproj_kernel,
        grid=(S // tm,),
        in_specs=[row(D_MODEL), whole(norm_pre), whole(w_hi), whole(bias), whole(freqs)],
        out_specs=[o[0] for o in outs],
        out_shape=[o[1] for o in outs],
        compiler_params=_params(
            ("arbitrary",),
            [((tm, D_MODEL), F32), (w_hi.shape, BF16)] + [(o[0].block_shape, o[1].dtype) for o in outs]),
        name="inproj",
    )(x2, norm_pre, w_hi, bias, freqs)


def _compress_kernel(x_ref, pe_ref, w1_ref, w2_ref, o_ref, *, keys):
    NC = x_ref.shape[0] // CMP_STRIDE
    g = pl.program_id(0)
    first_half = lax.broadcasted_iota(jnp.int32, (NC, LANE), 1) < NSA_DH
    tokens = []
    for u in range(CMP_STRIDE // 2):
        a = x_ref[pl.ds(2 * u, NC, stride=CMP_STRIDE), :]
        b = x_ref[pl.ds(2 * u + 1, NC, stride=CMP_STRIDE), :]
        ar, br = pltpu.roll(a, NSA_DH, 1), pltpu.roll(b, NSA_DH, 1)
        tokens.append(jnp.where(first_half, jnp.where(g == 0, a, ar), jnp.where(g == 0, br, b)))
    xg = jnp.concatenate(tokens, axis=1)
    prod = lambda a, b: _dot3(*_split(a), *_split(b))
    half = xg.shape[1]
    first = prod(xg, w1_ref[0:half, :])
    second = prod(xg, w1_ref[half:2 * half, :])
    nxt = pltpu.roll(second, NC - 1, 0)
    pe_term = prod(pe_ref[...], w1_ref[...])[0:1, :]
    hid = first + nxt + pe_term
    act = hid * (0.5 * (1.0 + jnp.tanh(math.sqrt(2.0 / math.pi) * (hid + 0.044715 * (hid * hid * hid)))))
    out = prod(act, w2_ref[...])
    if keys:
        hi, lo = _split(out)
        o_ref[...] = jnp.concatenate([hi, hi, lo, lo], axis=1)
    else:
        ones_row = jnp.where(lax.broadcasted_iota(jnp.int32, (V_ROWS - NSA_DH, NC), 0) == 0, 1.0, 0.0)
        o_ref[...] = jnp.concatenate([out.T, ones_row], axis=0).astype(BF16)


def _compress(raw, pe, w1, w2, keys):
    S = raw.shape[0]
    NC = S // CMP_STRIDE
    G = NSA_GROUPS
    half = CMP_STRIDE * NSA_DH
    pe8 = jnp.pad(pe.reshape(1, 2 * half), ((0, 7), (0, 0)))
    if keys:
        out_spec = pl.BlockSpec((None, NC, 4 * NSA_DH), lambda g: (g, 0, 0))
        out_shape = jax.ShapeDtypeStruct((G, NC, 4 * NSA_DH), BF16)
    else:
        out_spec = pl.BlockSpec((None, V_ROWS, NC), lambda g: (g, 0, 0))
        out_shape = jax.ShapeDtypeStruct((G, V_ROWS, NC), BF16)
    whole = lambda a: pl.BlockSpec(a.shape, lambda g: (0,) * a.ndim)
    return pl.pallas_call(
        functools.partial(_compress_kernel, keys=keys),
        grid=(G,),
        in_specs=[whole(raw), whole(pe8), whole(w1), whole(w2)],
        out_specs=out_spec,
        out_shape=out_shape,
        compiler_params=_params(
            ("arbitrary",),
            [(raw.shape, F32), (w1.shape, F32), (w2.shape, F32), (out_shape.shape[1:], BF16)],
            [((NC, 2 * CMP_HIDDEN), F32), ((NC, half), F32)]),
        name="compress_k" if keys else "compress_v",
    )(raw, pe8, w1, w2)


def _nsa_kernel(qT_ref, gT_ref, kc4_ref, vcT_ref, ks_ref, vsT_ref, kw_ref, vwT_ref, o_ref,
                sb_ref, sa_ref, sbuf_ref, sc_ref, oc_ref, ow_ref, m_ref, acc_ref):
    i = pl.program_id(1)
    NC = kc4_ref.shape[0]
    NB = sb_ref.shape[0]
    n_q = pl.num_programs(1)
    heads = range(NSA_HPG)
    lanes = lambda a, h: a[:, h * TQ:(h + 1) * TQ]
    colmax = lambda a: jnp.max(a, axis=0, keepdims=True)
    colsum = lambda a: jnp.sum(a, axis=0, keepdims=True)

    qT = qT_ref[...] * LOG2E
    Q = jnp.concatenate([qT[h * NSA_DH:(h + 1) * NSA_DH, :] for h in heads], axis=1)
    q_hi, q_lo = _split(Q)
    q4 = jnp.concatenate([q_hi, q_lo, q_hi, q_lo], axis=0)
    t = i * TQ + lax.broadcasted_iota(jnp.int32, (1, TQ), 1)
    cur = jnp.right_shift(t, SEL_BLOCK.bit_length() - 1)

    def window_branch():
        w0 = pl.multiple_of(jnp.maximum(i * TQ - WINDOW, 0), TQ)
        q_pad = jnp.concatenate([q_hi, jnp.zeros((K_LANES - NSA_DH, NSA_HPG * TQ), BF16)], axis=0)
        s = _dot(kw_ref[pl.ds(w0, WIN_KEYS), :], q_pad)
        key = w0 + lax.broadcasted_iota(jnp.int32, (WIN_KEYS, TQ), 0)
        mask_w = (key <= t) & (key > t - WINDOW)
        p_heads = []
        for h in heads:
            sm = jnp.where(mask_w, lanes(s, h), NEG)
            p_heads.append(jnp.exp2(sm - colmax(sm)).astype(BF16))
        accw = _dot(vwT_ref[:, pl.ds(w0, WIN_KEYS)], jnp.concatenate(p_heads, axis=1))
        owT = accw[0:NSA_DH, :] * (1.0 / accw[NSA_DH:NSA_DH + 1, :])
        ow_ref[...] = owT

    def compressed_and_select(ncp, nbp):
        s = _dot(kc4_ref[0:ncp, :], q4)
        c_end = lax.broadcasted_iota(jnp.int32, (ncp, TQ), 0) * CMP_STRIDE + (CMP_LEN - 1)
        mask_c = c_end <= t
        p_heads, inv = [], []
        for h in heads:
            sm = jnp.where(mask_c, lanes(s, h), -jnp.inf)
            m = colmax(sm)
            m = jnp.where(m == -jnp.inf, 0.0, m)
            p = jnp.exp2(sm - m)
            inv.append(1.0 / jnp.maximum(colsum(p), jnp.finfo(F32).tiny))
            sc_ref[h, 0:SC_PAD, :] = jnp.zeros((SC_PAD, TQ), F32)
            sc_ref[h, SC_PAD:SC_PAD + ncp, :] = p
            p_heads.append(p.astype(BF16))
        oc_ref[...] = (_dot(vcT_ref[:, 0:ncp], jnp.concatenate(p_heads, axis=1))
                       * jnp.concatenate(inv, axis=1))
        per_sel = SEL_BLOCK // CMP_STRIDE
        imp = jnp.zeros((nbp, TQ), F32)
        for h in heads:
            tot = sc_ref[h, pl.ds(SC_PAD - 1, nbp, stride=per_sel), :]
            for r in range(per_sel):
                tot = tot + sc_ref[h, pl.ds(SC_PAD + r, nbp, stride=per_sel), :]
            imp = imp + tot * inv[h]
        blk = lax.broadcasted_iota(jnp.int32, (nbp, TQ), 0)
        blk_f = blk.astype(F32)
        valid = blk <= cur
        forced = (blk == 0) | (blk == cur) | (blk == cur - 1)
        free = valid & jnp.logical_not(forced)
        score = jnp.where(free, imp, -jnp.inf)
        for _ in range(min(SEL_TOPN, NB) - 3):
            mx = colmax(score)
            idx = jnp.min(jnp.where(score == mx, blk_f, float(NB)), axis=0, keepdims=True)
            score = jnp.where(blk_f == idx, -jnp.inf, score)
        picked = valid & (forced | (score == -jnp.inf))
        sb_ref[0:nbp, :] = jnp.where(picked, 0.0, NEG)
        if nbp < NB:
            sb_ref[nbp:NB, :] = jnp.full((NB - nbp, TQ), NEG, F32)
        window_branch()

    size_class = lax.div(i * N_SIZE_CLASSES, n_q)
    for k in range(N_SIZE_CLASSES):
        pl.when(size_class == k)(functools.partial(
            compressed_and_select, NC * (k + 1) // N_SIZE_CLASSES, NB * (k + 1) // N_SIZE_CLASSES))

    pad_rows = jnp.zeros((K_LANES - NSA_DH - 2 * BLK_PER_TILE, NSA_HPG * TQ), BF16)

    def scores(j):
        k0 = pl.multiple_of(j * TK, TK)
        sbt = sb_ref[pl.ds(pl.multiple_of(j * BLK_PER_TILE, BLK_PER_TILE), BLK_PER_TILE), :]
        rows = jnp.concatenate([sbt, jnp.zeros_like(sbt)], axis=0)
        rows = jnp.concatenate([rows] * NSA_HPG, axis=1).astype(BF16)
        w = jnp.concatenate([q_hi, rows, pad_rows], axis=0)
        return _dot(ks_ref[pl.ds(k0, TK), :], w)

    def attend(j, s, carry, causal):
        m, acc = carry
        k0 = pl.multiple_of(j * TK, TK)
        if causal:
            keep = k0 + lax.broadcasted_iota(jnp.int32, (TK, TQ), 0) <= t
            s = jnp.concatenate([jnp.where(keep, lanes(s, h), NEG) for h in heads], axis=1)
        m_new = jnp.maximum(m, colmax(s))
        alpha = jnp.exp2(m - m_new)
        p = jnp.exp2(s - m_new).astype(BF16)
        acc = acc * alpha + _dot(vsT_ref[:, pl.ds(k0, TK)], p)
        return m_new, acc

    jd = lax.div(i, TK // TQ)
    n_pairs = lax.div(jd, 2)
    sa_ref[...] = scores(0)

    def pair(pp, carry):
        sbuf_ref[...] = scores(2 * pp + 1)
        carry = attend(2 * pp, sa_ref[...], carry, False)
        sa_ref[...] = scores(2 * pp + 2)
        return attend(2 * pp + 1, sbuf_ref[...], carry, False)

    carry = (jnp.full((1, NSA_HPG * TQ), NEG, F32), jnp.zeros((V_ROWS, NSA_HPG * TQ), F32))
    quad = lambda qq, c: pair(2 * qq + 1, pair(2 * qq, c))
    octo = lambda oo, c: quad(2 * oo + 1, quad(2 * oo, c))
    n_hexs, n_octs, n_quads = lax.div(n_pairs, 8), lax.div(n_pairs, 4), lax.div(n_pairs, 2)
    carry = lax.fori_loop(0, n_hexs, lambda hh, c: octo(2 * hh + 1, octo(2 * hh, c)), carry)
    carry = lax.fori_loop(2 * n_hexs, n_octs, octo, carry)
    carry = lax.fori_loop(2 * n_octs, n_quads, quad, carry)
    carry = lax.fori_loop(2 * n_quads, n_pairs, pair, carry)
    sbuf_ref[...] = scores(2 * n_pairs + 1)

    m_ref[...], acc_ref[...] = attend(2 * n_pairs, sa_ref[...], carry, True)

    @pl.when(jd != 2 * n_pairs)
    def _():
        m_ref[...], acc_ref[...] = attend(
            2 * n_pairs + 1, sbuf_ref[...], (m_ref[...], acc_ref[...]), True)

    acc = acc_ref[...]
    osT = acc[0:NSA_DH, :] * (1.0 / acc[NSA_DH:NSA_DH + 1, :])

    gT = gT_ref[...]
    ocT, owT = oc_ref[...], ow_ref[...]
    outs = []
    for h in heads:
        g_c, g_s, g_w = (gT[3 * h + b:3 * h + b + 1, :] for b in range(3))
        outs.append(g_c * lanes(ocT, h)[0:NSA_DH, :] + g_s * lanes(osT, h) + g_w * lanes(owT, h))
    o_ref[...] = jnp.concatenate(outs, axis=0).T.astype(BF16)


def _nsa(qT, gT, kc4, vcT, ks, vsT, kw, vwT):
    G, S = ks.shape[0], ks.shape[1]
    NC, NB = kc4.shape[1], S // SEL_BLOCK
    per_g = lambda a: pl.BlockSpec((None,) + a.shape[1:], lambda g, i: (g,) + (0,) * (a.ndim - 1))
    wide = NSA_HPG * TQ
    scratch = [(NB, TQ),
               (TK, wide), (TK, wide),
               (NSA_HPG, SC_PAD + NC, TQ),
               (V_ROWS, wide), (NSA_DH, wide),
               (1, wide), (V_ROWS, wide)]
    return pl.pallas_call(
        _nsa_kernel,
        grid=(G, S // TQ),
        in_specs=[
            pl.BlockSpec((NSA_HPG * NSA_DH, TQ), lambda g, i: (g, i)),
            pl.BlockSpec((None, GATE_ROWS, TQ), lambda g, i: (g, 0, i)),
            per_g(kc4), per_g(vcT), per_g(ks), per_g(vsT), per_g(kw), per_g(vwT),
        ],
        out_specs=pl.BlockSpec((TQ, NSA_HPG * NSA_DH), lambda g, i: (i, g)),
        out_shape=jax.ShapeDtypeStruct((S, NSA_HEADS * NSA_DH), BF16),
        scratch_shapes=[pltpu.VMEM(shape, F32) for shape in scratch],
        compiler_params=_params(
            ("arbitrary", "arbitrary"),
            [(a.shape[1:], a.dtype) for a in (kc4, vcT, ks, vsT, kw, vwT)]
            + [((NSA_HPG * NSA_DH, TQ), F32), ((TQ, NSA_HPG * NSA_DH), BF16)],
            [(shape, F32) for shape in scratch]),
        name="nsa",
    )(qT, gT, kc4, vcT, ks, vsT, kw, vwT)


def _ret_kernel(q_ref, k_ref, v_ref, o_ref, r_ref, dm_ref, qd_ref, kd_ref):
    C = RET_CHUNK
    log_g = [math.log(1.0 - 2.0 ** (-5.0 - h)) for h in range(RET_HEADS)]

    @pl.when(pl.program_id(0) == 0)
    def _():
        r_ref[...] = jnp.zeros(r_ref.shape, F32)
        diff = (lax.broadcasted_iota(jnp.int32, (C, C), 0)
                - lax.broadcasted_iota(jnp.int32, (C, C), 1)).astype(F32)
        n = lax.broadcasted_iota(jnp.int32, (C, RET_DK), 0).astype(F32)
        for h in range(RET_HEADS):
            dm_ref[h] = jnp.where(diff >= 0.0, jnp.exp(jnp.maximum(diff, 0.0) * log_g[h]), 0.0)
            qd_ref[h] = jnp.exp((n + 1.0) * log_g[h])
            kd_ref[h] = jnp.exp((C - 1.0 - n) * log_g[h])

    for h in range(RET_HEADS):
        q = q_ref[:, h * RET_DK:(h + 1) * RET_DK]
        k = k_ref[:, h * RET_DK:(h + 1) * RET_DK]
        v = v_ref[:, h * RET_DV:(h + 1) * RET_DV]
        att = lax.dot_general(q, k, (((1,), (1,)), ((), ())), preferred_element_type=F32)
        o = _dot((att * dm_ref[h]).astype(BF16), v)
        r = r_ref[h]
        qd = (q.astype(F32) * qd_ref[h]).astype(BF16)
        o = o + _dot(qd, r.astype(BF16))
        kd = (k.astype(F32) * kd_ref[h]).astype(BF16)
        r_ref[h] = math.exp(C * log_g[h]) * r + lax.dot_general(
            kd, v, (((0,), (0,)), ((), ())), preferred_element_type=F32)
        mu = jnp.mean(o, axis=-1, keepdims=True)
        d = o - mu
        var = jnp.mean(d * d, axis=-1, keepdims=True)
        o_ref[:, h * RET_DV:(h + 1) * RET_DV] = (d * lax.rsqrt(var + GN_EPS)).astype(BF16)


def _retention(qr, kr, vr):
    S = qr.shape[0]
    C = RET_CHUNK
    row = lambda w: pl.BlockSpec((C, w), lambda n: (n, 0))
    scratch = [(RET_HEADS, RET_DK, RET_DV),
               (RET_HEADS, C, C),
               (RET_HEADS, C, RET_DK), (RET_HEADS, C, RET_DK)]
    return pl.pallas_call(
        _ret_kernel,
        grid=(S // C,),
        in_specs=[row(RET_HEADS * RET_DK), row(RET_HEADS * RET_DK), row(RET_HEADS * RET_DV)],
        out_specs=row(RET_HEADS * RET_DV),
        out_shape=jax.ShapeDtypeStruct((S, RET_HEADS * RET_DV), BF16),
        scratch_shapes=[pltpu.VMEM(shape, F32) for shape in scratch],
        compiler_params=_params(
            ("arbitrary",),
            [((C, w), BF16) for w in (RET_HEADS * RET_DK, RET_HEADS * RET_DK, 2 * RET_HEADS * RET_DV)],
            [(shape, F32) for shape in scratch]),
        name="retention",
    )(qr, kr, vr)


def _post_kernel(x_ref, oa_ref, zn_ref, ob_ref, zr_ref, ma_ref, mb_ref,
                 wa_ref, wb_ref, wo_ref, g_ref, out_ref):
    f32 = lambda r: r[...].astype(F32)
    ya = _dot((f32(oa_ref) * f32(zn_ref)).astype(BF16), wa_ref[...])
    yb = _dot((f32(ob_ref) * f32(zr_ref)).astype(BF16), wb_ref[...])
    merged = f32(ma_ref) * ya + f32(mb_ref) * yb
    y = _dot(merged.astype(BF16), wo_ref[...])
    ms = jnp.mean(y * y, axis=-1, keepdims=True)
    out_ref[...] = x_ref[...] + y * lax.rsqrt(ms + RMS_EPS) * g_ref[...]


def _post(x2, oa, zn, ob, zr, ma, mb, wa, wb, wo, g_post):
    S = x2.shape[0]
    tm = POST_ROW_TILE
    row = lambda a: pl.BlockSpec((tm, a.shape[1]), lambda i: (i, 0))
    whole = lambda a: pl.BlockSpec(a.shape, lambda i: (0, 0))
    rows = (x2, oa, zn, ob, zr, ma, mb)
    consts = (wa, wb, wo, g_post)
    return pl.pallas_call(
        _post_kernel,
        grid=(S // tm,),
        in_specs=[row(a) for a in rows] + [whole(a) for a in consts],
        out_specs=row(x2),
        out_shape=jax.ShapeDtypeStruct(x2.shape, x2.dtype),
        compiler_params=_params(
            ("arbitrary",),
            [((tm, a.shape[1]), a.dtype) for a in rows + (x2,)] + [(a.shape, a.dtype) for a in consts]),
        name="post",
    )(*rows, *consts)


def _rotary_frequencies():
    nsa_inv = 1.0 / (ROPE_THETA ** (jnp.arange(0, ROPE_DIM, 2, dtype=F32) / ROPE_DIM))
    ret_inv = 1.0 / (RET_ROPE_BASE ** jnp.linspace(0.0, 1.0, RET_DK // 2, dtype=F32))
    row = jnp.concatenate([ret_inv, nsa_inv, jnp.zeros((LANE - RET_DK // 2 - ROPE_DIM // 2,), F32)])
    return jnp.pad(row[None, :], ((0, 7), (0, 0)))


def kernel(x, norm_pre, w_in, b_nsa_gate, cmp_pe_k, cmp_w1_k, cmp_w2_k, cmp_pe_v, cmp_w1_v,
           cmp_w2_v, w_nsa_o, w_ret_o, w_out, norm_post):
    B, S, _ = x.shape
    assert B == 1 and S % (2 * TK) == 0 and S >= WIN_KEYS
    assert S % (SEL_BLOCK * 8 * N_SIZE_CLASSES) == 0 and (S // TQ) % N_SIZE_CLASSES == 0
    x2 = x.reshape(S, D_MODEL)

    gate_cols = lambda a: jnp.pad(
        a.reshape(a.shape[:-1] + (NSA_GROUPS, NSA_HPG * 3)),
        [(0, 0)] * (a.ndim - 1) + [(0, 0), (0, GATE_ROWS - NSA_HPG * 3)]).reshape(a.shape[:-1] + (-1,))
    g0 = sum(PROJ_SIZES[:7])
    g1 = g0 + NSA_HEADS * 3
    gate_seg = gate_cols(w_in[:, g0:g1])
    gate_seg = jnp.pad(gate_seg, ((0, 0), (0, LANE - gate_seg.shape[1])))
    w_hi = jnp.concatenate([w_in[:, :g0], gate_seg, w_in[:, g1:]], axis=1).astype(BF16)
    bias = gate_cols(b_nsa_gate)
    bias = jnp.pad(bias, (0, LANE - bias.shape[0])).reshape(1, LANE)

    (qT, kc, vc, ks, vsT, kw, vwT, gT, zn, qr, kr, vr, zr, ma, mb) = _inproj(
        x2, norm_pre.reshape(1, D_MODEL), w_hi, bias, _rotary_frequencies())

    kc4 = _compress(kc, cmp_pe_k, cmp_w1_k, cmp_w2_k, keys=True)
    vcT = _compress(vc, cmp_pe_v, cmp_w1_v, cmp_w2_v, keys=False)

    oa = _nsa(qT, gT, kc4, vcT, ks, vsT, kw, vwT)
    ob = _retention(qr, kr, vr)

    out = _post(x2, oa, zn, ob, zr, ma, mb, w_nsa_o.astype(BF16), w_ret_o.astype(BF16),
                w_out.astype(BF16), norm_post.reshape(1, D_MODEL))
    return out.reshape(B, S, D_MODEL)
```

```python
import functools
import math

import jax
import jax.numpy as jnp
from jax import lax
from jax.experimental import pallas as pl
from jax.experimental.pallas import tpu as pltpu

F32 = jnp.float32
BF16 = jnp.bfloat16

D_MODEL = 1024
NSA_HEADS = 8
NSA_GROUPS = 2
NSA_HPG = NSA_HEADS // NSA_GROUPS
NSA_DH = 64
CMP_LEN = 32
CMP_STRIDE = 16
CMP_HIDDEN = 256
SEL_BLOCK = 64
SEL_TOPN = 16
WINDOW = 512
ROPE_THETA = 500000.0
ROPE_DIM = NSA_DH // 4
NSA_SCALE = NSA_DH ** -0.5
RET_HEADS = 4
RET_DK = 128
RET_DV = 256
RET_ROPE_BASE = 10000.0
RET_SCALE = RET_DK ** -0.5
RMS_EPS = 1e-6
GN_EPS = 1e-6
PROJ_SIZES = (512, 128, 128, 128, 128, 128, 128, 24, 512, 512, 512, 1024, 1024, 1024, 1024)

LANE = 128
V7X_VMEM_LIMIT_BYTES = 56 * 1024 * 1024
VMEM_TEMP_FLOOR_BYTES = 8 * 1024 * 1024

ROW_TILE = 256
POST_ROW_TILE = 512
TQ = 128
TK = 512
BLK_PER_TILE = TK // SEL_BLOCK
WIN_KEYS = WINDOW + TQ
RET_CHUNK = 256
V_ROWS = 80
NEG = -1e30
LOG2E = math.log2(math.e)
K_LANES = LANE
GATE_ROWS = 16
SC_PAD = 8
N_SIZE_CLASSES = 8

_PAD_SIZES = tuple(128 if s == 24 else s for s in PROJ_SIZES)
_OFF = [0]
for _s in _PAD_SIZES:
    _OFF.append(_OFF[-1] + _s)


def _dot(a, b):
    return jnp.dot(a, b, preferred_element_type=F32)


def _split(a):
    hi = a.astype(BF16)
    lo = (a - hi.astype(F32)).astype(BF16)
    return hi, lo


def _dot3(a_hi, a_lo, b_hi, b_lo):
    return _dot(a_hi, b_hi) + _dot(a_lo, b_hi) + _dot(a_hi, b_lo)


def _nbytes(shape, dtype):
    return math.prod(d for d in shape if d is not None) * jnp.dtype(dtype).itemsize


def _params(sem, blocks, scratch=()):
    est = 2 * sum(_nbytes(*b) for b in blocks) + sum(_nbytes(*b) for b in scratch)
    limit = min(V7X_VMEM_LIMIT_BYTES, est + est // 4 + VMEM_TEMP_FLOOR_BYTES)
    return pltpu.CompilerParams(dimension_semantics=sem, vmem_limit_bytes=limit)


def _inproj_kernel(x_ref, g_ref, whi_ref, bias_ref, freq_ref,
                   qT_ref, kc_ref, vc_ref, ks_ref, vsT_ref, kw_ref, vwT_ref, gT_ref,
                   zn_ref, qr_ref, kr_ref, vr_ref, zr_ref, ma_ref, mb_ref):
    tm = x_ref.shape[0]
    x = x_ref[...]
    ms = jnp.mean(x * x, axis=-1, keepdims=True)
    h = x * lax.rsqrt(ms + RMS_EPS) * g_ref[...]
    h_hi = h.astype(BF16)

    def mm(lo, hi):
        return _dot(h_hi, whi_ref[:, lo:hi])

    lane = lax.broadcasted_iota(jnp.int32, (tm, LANE), 1)
    row = pl.program_id(0) * tm + lax.broadcasted_iota(jnp.int32, (tm, LANE), 0)
    first_group = lane < NSA_DH
    pos = row.astype(F32)

    half = ROPE_DIM // 2
    dim = jnp.bitwise_and(lane, NSA_DH - 1)
    ang = pos * freq_ref[0:1, :]
    c_all, s_all = jnp.cos(ang), jnp.sin(ang)
    c_sw, s_sw = pltpu.roll(c_all, LANE // 2, 1), pltpu.roll(s_all, LANE // 2, 1)
    cr = jnp.where(first_group, c_all, c_sw)
    sr = jnp.where(first_group, -s_all, s_sw)
    lo_half, hi_half = dim < half, (dim >= half) & (dim < ROPE_DIM)
    c_lo, s_lo = jnp.where(first_group, c_sw, c_all), jnp.where(first_group, s_sw, s_all)
    c_hi, s_hi = pltpu.roll(c_lo, half, 1), pltpu.roll(s_lo, half, 1)
    cn = jnp.where(lo_half, c_lo, jnp.where(hi_half, c_hi, 1.0))
    san = jnp.where(hi_half, s_hi, 0.0)
    sbn = jnp.where(lo_half, -s_lo, 0.0)

    def rot_nsa(p):
        return p * cn + pltpu.roll(p, half, 1) * san + pltpu.roll(p, LANE - half, 1) * sbn

    def rot_ret(p):
        return p * cr + pltpu.roll(p, LANE // 2, 1) * sr

    def key_rows(p, fill):
        swapped = pltpu.roll(p, NSA_DH, 1)
        return [jnp.where(first_group, pg, fill).astype(BF16) for pg in (p, swapped)]

    ones_row = jnp.where(lax.broadcasted_iota(jnp.int32, (V_ROWS - NSA_DH, tm), 0) == 0, 1.0, 0.0)

    def value_rows(p):
        pT = p.T
        return [jnp.concatenate([pT[g * NSA_DH:(g + 1) * NSA_DH, :], ones_row], axis=0).astype(BF16)
                for g in range(NSA_GROUPS)]

    p = mm(0, _OFF[2])
    for c in range(4):
        qT = (rot_nsa(p[:, c * LANE:(c + 1) * LANE]) * NSA_SCALE).T
        for j in range(tm // TQ):
            qT_ref[j, c * LANE:(c + 1) * LANE, :] = qT[:, j * TQ:(j + 1) * TQ]
    kc_ref[...] = rot_nsa(p[:, 4 * LANE:5 * LANE])

    p = mm(_OFF[2], _OFF[8])
    vc_ref[...] = p[:, 0 * LANE:1 * LANE]
    blk_in_tile = jnp.right_shift(jnp.bitwise_and(row, TK - 1), SEL_BLOCK.bit_length() - 1)
    onehot = jnp.where(lane - NSA_DH == blk_in_tile, 1.0, 0.0)
    ks = key_rows(rot_nsa(p[:, 1 * LANE:2 * LANE]), onehot)
    vs = value_rows(p[:, 2 * LANE:3 * LANE])
    kw = key_rows(rot_nsa(p[:, 3 * LANE:4 * LANE]), 0.0)
    vw = value_rows(p[:, 4 * LANE:5 * LANE])
    gT = jax.nn.sigmoid(p[:, 5 * LANE:6 * LANE] + bias_ref[...]).T
    for g in range(NSA_GROUPS):
        ks_ref[g], vsT_ref[g], kw_ref[g], vwT_ref[g] = ks[g], vs[g], kw[g], vw[g]
        gT_ref[g] = gT[g * GATE_ROWS:(g + 1) * GATE_ROWS, :]

    z = mm(_OFF[8], _OFF[9])
    zn_ref[...] = (z * jax.nn.sigmoid(z)).astype(BF16)

    p = mm(_OFF[9], _OFF[11])
    for c in range(4):
        qr_ref[:, c * LANE:(c + 1) * LANE] = rot_ret(p[:, c * LANE:(c + 1) * LANE]).astype(BF16)
    for c in range(4):
        kr_ref[:, c * LANE:(c + 1) * LANE] = (
            rot_ret(p[:, (4 + c) * LANE:(5 + c) * LANE]) * RET_SCALE).astype(BF16)

    vr_ref[...] = mm(_OFF[11], _OFF[12]).astype(BF16)
    z = mm(_OFF[12], _OFF[13])
    zr_ref[...] = (z * jax.nn.sigmoid(z)).astype(BF16)
    ma_ref[...] = jax.nn.sigmoid(mm(_OFF[13], _OFF[14])).astype(BF16)
    mb_ref[...] = jax.nn.sigmoid(mm(_OFF[14], _OFF[15])).astype(BF16)


def _inproj(x2, norm_pre, w_hi, bias, freqs):
    S = x2.shape[0]
    tm = ROW_TILE
    G = NSA_GROUPS
    row = lambda w: pl.BlockSpec((tm, w), lambda i: (i, 0))
    whole = lambda a: pl.BlockSpec(a.shape, lambda i: (0,) * a.ndim)
    g_row = lambda w: pl.BlockSpec((G, tm, w), lambda i: (0, i, 0))
    g_col = lambda r: pl.BlockSpec((G, r, tm), lambda i: (0, 0, i))
    sds = jax.ShapeDtypeStruct
    outs = [
        (pl.BlockSpec((tm // TQ, NSA_HEADS * NSA_DH, TQ), lambda i: (i, 0, 0)),
         sds((S // TQ, NSA_HEADS * NSA_DH, TQ), F32)),
        (row(LANE), sds((S, LANE), F32)),
        (row(LANE), sds((S, LANE), F32)),
        (g_row(K_LANES), sds((G, S, K_LANES), BF16)),
        (g_col(V_ROWS), sds((G, V_ROWS, S), BF16)),
        (g_row(K_LANES), sds((G, S, K_LANES), BF16)),
        (g_col(V_ROWS), sds((G, V_ROWS, S), BF16)),
        (g_col(GATE_ROWS), sds((G, GATE_ROWS, S), F32)),
        (row(512), sds((S, 512), BF16)),
        (row(512), sds((S, 512), BF16)), (row(512), sds((S, 512), BF16)),
        (row(1024), sds((S, 1024), BF16)), (row(1024), sds((S, 1024), BF16)),
        (row(1024), sds((S, 1024), BF16)), (row(1024), sds((S, 1024), BF16)),
    ]
    return pl.pallas_call(
        _inproj_kernel,
        grid=(S // tm,),
        in_specs=[row(D_MODEL), whole(norm_pre), whole(w_hi), whole(bias), whole(freqs)],
        out_specs=[o[0] for o in outs],
        out_shape=[o[1] for o in outs],
        compiler_params=_params(
            ("arbitrary",),
            [((tm, D_MODEL), F32), (w_hi.shape, BF16)] + [(o[0].block_shape, o[1].dtype) for o in outs]),
        name="inproj",
    )(x2, norm_pre, w_hi, bias, freqs)


def _compress_kernel(x_ref, pe_ref, w1_ref, w2_ref, o_ref, *, keys):
    NC = x_ref.shape[0] // CMP_STRIDE
    g = pl.program_id(0)
    first_half = lax.broadcasted_iota(jnp.int32, (NC, LANE), 1) < NSA_DH
    tokens = []
    for u in range(CMP_STRIDE // 2):
        a = x_ref[pl.ds(2 * u, NC, stride=CMP_STRIDE), :]
        b = x_ref[pl.ds(2 * u + 1, NC, stride=CMP_STRIDE), :]
        ar, br = pltpu.roll(a, NSA_DH, 1), pltpu.roll(b, NSA_DH, 1)
        tokens.append(jnp.where(first_half, jnp.where(g == 0, a, ar), jnp.where(g == 0, br, b)))
    xg = jnp.concatenate(tokens, axis=1)
    prod = lambda a, b: _dot3(*_split(a), *_split(b))
    half = xg.shape[1]
    first = prod(xg, w1_ref[0:half, :])
    second = prod(xg, w1_ref[half:2 * half, :])
    nxt = pltpu.roll(second, NC - 1, 0)
    pe_term = prod(pe_ref[...], w1_ref[...])[0:1, :]
    hid = first + nxt + pe_term
    act = hid * (0.5 * (1.0 + jnp.tanh(math.sqrt(2.0 / math.pi) * (hid + 0.044715 * (hid * hid * hid)))))
    out = prod(act, w2_ref[...])
    if keys:
        hi, lo = _split(out)
        o_ref[...] = jnp.concatenate([hi, hi, lo, lo], axis=1)
    else:
        ones_row = jnp.where(lax.broadcasted_iota(jnp.int32, (V_ROWS - NSA_DH, NC), 0) == 0, 1.0, 0.0)
        o_ref[...] = jnp.concatenate([out.T, ones_row], axis=0).astype(BF16)


def _compress(raw, pe, w1, w2, keys):
    S = raw.shape[0]
    NC = S // CMP_STRIDE
    G = NSA_GROUPS
    half = CMP_STRIDE * NSA_DH
    pe8 = jnp.pad(pe.reshape(1, 2 * half), ((0, 7), (0, 0)))
    if keys:
        out_spec = pl.BlockSpec((None, NC, 4 * NSA_DH), lambda g: (g, 0, 0))
        out_shape = jax.ShapeDtypeStruct((G, NC, 4 * NSA_DH), BF16)
    else:
        out_spec = pl.BlockSpec((None, V_ROWS, NC), lambda g: (g, 0, 0))
        out_shape = jax.ShapeDtypeStruct((G, V_ROWS, NC), BF16)
    whole = lambda a: pl.BlockSpec(a.shape, lambda g: (0,) * a.ndim)
    return pl.pallas_call(
        functools.partial(_compress_kernel, keys=keys),
        grid=(G,),
        in_specs=[whole(raw), whole(pe8), whole(w1), whole(w2)],
        out_specs=out_spec,
        out_shape=out_shape,
        compiler_params=_params(
            ("arbitrary",),
            [(raw.shape, F32), (w1.shape, F32), (w2.shape, F32), (out_shape.shape[1:], BF16)],
            [((NC, 2 * CMP_HIDDEN), F32), ((NC, half), F32)]),
        name="compress_k" if keys else "compress_v",
    )(raw, pe8, w1, w2)


def _nsa_kernel(qT_ref, gT_ref, kc4_ref, vcT_ref, ks_ref, vsT_ref, kw_ref, vwT_ref, o_ref,
                sb_ref, sa_ref, sbuf_ref, sc_ref, oc_ref, ow_ref, m_ref, acc_ref):
    i = pl.program_id(1)
    NC = kc4_ref.shape[0]
    NB = sb_ref.shape[0]
    n_q = pl.num_programs(1)
    heads = range(NSA_HPG)
    lanes = lambda a, h: a[:, h * TQ:(h + 1) * TQ]
    colmax = lambda a: jnp.max(a, axis=0, keepdims=True)
    colsum = lambda a: jnp.sum(a, axis=0, keepdims=True)

    qT = qT_ref[...] * LOG2E
    Q = jnp.concatenate([qT[h * NSA_DH:(h + 1) * NSA_DH, :] for h in heads], axis=1)
    q_hi, q_lo = _split(Q)
    q4 = jnp.concatenate([q_hi, q_lo, q_hi, q_lo], axis=0)
    t = i * TQ + lax.broadcasted_iota(jnp.int32, (1, TQ), 1)
    cur = jnp.right_shift(t, SEL_BLOCK.bit_length() - 1)

    def window_branch():
        w0 = pl.multiple_of(jnp.maximum(i * TQ - WINDOW, 0), TQ)
        q_pad = jnp.concatenate([q_hi, jnp.zeros((K_LANES - NSA_DH, NSA_HPG * TQ), BF16)], axis=0)
        s = _dot(kw_ref[pl.ds(w0, WIN_KEYS), :], q_pad)
        key = w0 + lax.broadcasted_iota(jnp.int32, (WIN_KEYS, TQ), 0)
        mask_w = (key <= t) & (key > t - WINDOW)
        p_heads = []
        for h in heads:
            sm = jnp.where(mask_w, lanes(s, h), NEG)
            p_heads.append(jnp.exp2(sm - colmax(sm)).astype(BF16))
        accw = _dot(vwT_ref[:, pl.ds(w0, WIN_KEYS)], jnp.concatenate(p_heads, axis=1))
        owT = accw[0:NSA_DH, :] * (1.0 / accw[NSA_DH:NSA_DH + 1, :])
        ow_ref[...] = owT

    def compressed_and_select(ncp, nbp):
        s = _dot(kc4_ref[0:ncp, :], q4)
        c_end = lax.broadcasted_iota(jnp.int32, (ncp, TQ), 0) * CMP_STRIDE + (CMP_LEN - 1)
        mask_c = c_end <= t
        p_heads, inv = [], []
        for h in heads:
            sm = jnp.where(mask_c, lanes(s, h), -jnp.inf)
            m = colmax(sm)
            m = jnp.where(m == -jnp.inf, 0.0, m)
            p = jnp.exp2(sm - m)
            inv.append(1.0 / jnp.maximum(colsum(p), jnp.finfo(F32).tiny))
            sc_ref[h, 0:SC_PAD, :] = jnp.zeros((SC_PAD, TQ), F32)
            sc_ref[h, SC_PAD:SC_PAD + ncp, :] = p
            p_heads.append(p.astype(BF16))
        oc_ref[...] = (_dot(vcT_ref[:, 0:ncp], jnp.concatenate(p_heads, axis=1))
                       * jnp.concatenate(inv, axis=1))
        per_sel = SEL_BLOCK // CMP_STRIDE
        imp = jnp.zeros((nbp, TQ), F32)
        for h in heads:
            tot = sc_ref[h, pl.ds(SC_PAD - 1, nbp, stride=per_sel), :]
            for r in range(per_sel):
                tot = tot + sc_ref[h, pl.ds(SC_PAD + r, nbp, stride=per_sel), :]
            imp = imp + tot * inv[h]
        blk = lax.broadcasted_iota(jnp.int32, (nbp, TQ), 0)
        blk_f = blk.astype(F32)
        valid = blk <= cur
        forced = (blk == 0) | (blk == cur) | (blk == cur - 1)
        free = valid & jnp.logical_not(forced)
        score = jnp.where(free, imp, -jnp.inf)
        for _ in range(min(SEL_TOPN, NB) - 3):
            mx = colmax(score)
            idx = jnp.min(jnp.where(score == mx, blk_f, float(NB)), axis=0, keepdims=True)
            score = jnp.where(blk_f == idx, -jnp.inf, score)
        picked = valid & (forced | (score == -jnp.inf))
        sb_ref[0:nbp, :] = jnp.where(picked, 0.0, NEG)
        if nbp < NB:
            sb_ref[nbp:NB, :] = jnp.full((NB - nbp, TQ), NEG, F32)
        window_branch()

    size_class = lax.div(i * N_SIZE_CLASSES, n_q)
    for k in range(N_SIZE_CLASSES):
        pl.when(size_class == k)(functools.partial(
            compressed_and_select, NC * (k + 1) // N_SIZE_CLASSES, NB * (k + 1) // N_SIZE_CLASSES))

    pad_rows = jnp.zeros((K_LANES - NSA_DH - 2 * BLK_PER_TILE, NSA_HPG * TQ), BF16)

    def scores(j):
        k0 = pl.multiple_of(j * TK, TK)
        sbt = sb_ref[pl.ds(pl.multiple_of(j * BLK_PER_TILE, BLK_PER_TILE), BLK_PER_TILE), :]
        rows = jnp.concatenate([sbt, jnp.zeros_like(sbt)], axis=0)
        rows = jnp.concatenate([rows] * NSA_HPG, axis=1).astype(BF16)
        w = jnp.concatenate([q_hi, rows, pad_rows], axis=0)
        return _dot(ks_ref[pl.ds(k0, TK), :], w)

    def attend(j, s, carry, causal):
        m, acc = carry
        k0 = pl.multiple_of(j * TK, TK)
        if causal:
            keep = k0 + lax.broadcasted_iota(jnp.int32, (TK, TQ), 0) <= t
            s = jnp.concatenate([jnp.where(keep, lanes(s, h), NEG) for h in heads], axis=1)
        m_new = jnp.maximum(m, colmax(s))
        alpha = jnp.exp2(m - m_new)
        p = jnp.exp2(s - m_new).astype(BF16)
        acc = acc * alpha + _dot(vsT_ref[:, pl.ds(k0, TK)], p)
        return m_new, acc

    jd = lax.div(i, TK // TQ)
    n_pairs = lax.div(jd, 2)
    sa_ref[...] = scores(0)

    def pair(pp, carry):
        sbuf_ref[...] = scores(2 * pp + 1)
        carry = attend(2 * pp, sa_ref[...], carry, False)
        sa_ref[...] = scores(2 * pp + 2)
        return attend(2 * pp + 1, sbuf_ref[...], carry, False)

    carry = (jnp.full((1, NSA_HPG * TQ), NEG, F32), jnp.zeros((V_ROWS, NSA_HPG * TQ), F32))
    quad = lambda qq, c: pair(2 * qq + 1, pair(2 * qq, c))
    octo = lambda oo, c: quad(2 * oo + 1, quad(2 * oo, c))
    n_hexs, n_octs, n_quads = lax.div(n_pairs, 8), lax.div(n_pairs, 4), lax.div(n_pairs, 2)
    carry = lax.fori_loop(0, n_hexs, lambda hh, c: octo(2 * hh + 1, octo(2 * hh, c)), carry)
    carry = lax.fori_loop(2 * n_hexs, n_octs, octo, carry)
    carry = lax.fori_loop(2 * n_octs, n_quads, quad, carry)
    carry = lax.fori_loop(2 * n_quads, n_pairs, pair, carry)
    sbuf_ref[...] = scores(2 * n_pairs + 1)

    m_ref[...], acc_ref[...] = attend(2 * n_pairs, sa_ref[...], carry, True)

    @pl.when(jd != 2 * n_pairs)
    def _():
        m_ref[...], acc_ref[...] = attend(
            2 * n_pairs + 1, sbuf_ref[...], (m_ref[...], acc_ref[...]), True)

    acc = acc_ref[...]
    osT = acc[0:NSA_DH, :] * (1.0 / acc[NSA_DH:NSA_DH + 1, :])

    gT = gT_ref[...]
    ocT, owT = oc_ref[...], ow_ref[...]
    outs = []
    for h in heads:
        g_c, g_s, g_w = (gT[3 * h + b:3 * h + b + 1, :] for b in range(3))
        outs.append(g_c * lanes(ocT, h)[0:NSA_DH, :] + g_s * lanes(osT, h) + g_w * lanes(owT, h))
    o_ref[...] = jnp.concatenate(outs, axis=0).T.astype(BF16)


def _nsa(qT, gT, kc4, vcT, ks, vsT, kw, vwT):
    G, S = ks.shape[0], ks.shape[1]
    NC, NB = kc4.shape[1], S // SEL_BLOCK
    per_g = lambda a: pl.BlockSpec((None,) + a.shape[1:], lambda g, i: (g,) + (0,) * (a.ndim - 1))
    wide = NSA_HPG * TQ
    scratch = [(NB, TQ),
               (TK, wide), (TK, wide),
               (NSA_HPG, SC_PAD + NC, TQ),
               (V_ROWS, wide), (NSA_DH, wide),
               (1, wide), (V_ROWS, wide)]
    return pl.pallas_call(
        _nsa_kernel,
        grid=(G, S // TQ),
        in_specs=[
            pl.BlockSpec((None, NSA_HPG * NSA_DH, TQ), lambda g, i: (i, g, 0)),
            pl.BlockSpec((None, GATE_ROWS, TQ), lambda g, i: (g, 0, i)),
            per_g(kc4), per_g(vcT), per_g(ks), per_g(vsT), per_g(kw), per_g(vwT),
        ],
        out_specs=pl.BlockSpec((None, TQ, NSA_HPG * NSA_DH), lambda g, i: (g, i, 0)),
        out_shape=jax.ShapeDtypeStruct((G, S, NSA_HPG * NSA_DH), BF16),
        scratch_shapes=[pltpu.VMEM(shape, F32) for shape in scratch],
        compiler_params=_params(
            ("arbitrary", "arbitrary"),
            [(a.shape[1:], a.dtype) for a in (kc4, vcT, ks, vsT, kw, vwT)]
            + [((NSA_HPG * NSA_DH, TQ), F32), ((TQ, NSA_HPG * NSA_DH), BF16)],
            [(shape, F32) for shape in scratch]),
        name="nsa",
    )(qT, gT, kc4, vcT, ks, vsT, kw, vwT)


def _ret_kernel(q_ref, k_ref, v_ref, o_ref, r_ref, dm_ref, qd_ref, kd_ref):
    C = RET_CHUNK
    log_g = [math.log(1.0 - 2.0 ** (-5.0 - h)) for h in range(RET_HEADS)]

    @pl.when(pl.program_id(0) == 0)
    def _():
        r_ref[...] = jnp.zeros(r_ref.shape, F32)
        diff = (lax.broadcasted_iota(jnp.int32, (C, C), 0)
                - lax.broadcasted_iota(jnp.int32, (C, C), 1)).astype(F32)
        n = lax.broadcasted_iota(jnp.int32, (C, RET_DK), 0).astype(F32)
        for h in range(RET_HEADS):
            dm_ref[h] = jnp.where(diff >= 0.0, jnp.exp(jnp.maximum(diff, 0.0) * log_g[h]), 0.0)
            qd_ref[h] = jnp.exp((n + 1.0) * log_g[h])
            kd_ref[h] = jnp.exp((C - 1.0 - n) * log_g[h])

    for h in range(RET_HEADS):
        q = q_ref[:, h * RET_DK:(h + 1) * RET_DK]
        k = k_ref[:, h * RET_DK:(h + 1) * RET_DK]
        v = v_ref[:, h * RET_DV:(h + 1) * RET_DV]
        att = lax.dot_general(q, k, (((1,), (1,)), ((), ())), preferred_element_type=F32)
        o = _dot((att * dm_ref[h]).astype(BF16), v)
        r = r_ref[h]
        qd = (q.astype(F32) * qd_ref[h]).astype(BF16)
        o = o + _dot(qd, r.astype(BF16))
        kd = (k.astype(F32) * kd_ref[h]).astype(BF16)
        r_ref[h] = math.exp(C * log_g[h]) * r + lax.dot_general(
            kd, v, (((0,), (0,)), ((), ())), preferred_element_type=F32)
        mu = jnp.mean(o, axis=-1, keepdims=True)
        d = o - mu
        var = jnp.mean(d * d, axis=-1, keepdims=True)
        o_ref[:, h * RET_DV:(h + 1) * RET_DV] = (d * lax.rsqrt(var + GN_EPS)).astype(BF16)


def _retention(qr, kr, vr):
    S = qr.shape[0]
    C = RET_CHUNK
    row = lambda w: pl.BlockSpec((C, w), lambda n: (n, 0))
    scratch = [(RET_HEADS, RET_DK, RET_DV),
               (RET_HEADS, C, C),
               (RET_HEADS, C, RET_DK), (RET_HEADS, C, RET_DK)]
    return pl.pallas_call(
        _ret_kernel,
        grid=(S // C,),
        in_specs=[row(RET_HEADS * RET_DK), row(RET_HEADS * RET_DK), row(RET_HEADS * RET_DV)],
        out_specs=row(RET_HEADS * RET_DV),
        out_shape=jax.ShapeDtypeStruct((S, RET_HEADS * RET_DV), BF16),
        scratch_shapes=[pltpu.VMEM(shape, F32) for shape in scratch],
        compiler_params=_params(
            ("arbitrary",),
            [((C, w), BF16) for w in (RET_HEADS * RET_DK, RET_HEADS * RET_DK, 2 * RET_HEADS * RET_DV)],
            [(shape, F32) for shape in scratch]),
        name="retention",
    )(qr, kr, vr)


def _post_kernel(x_ref, oa_ref, zn_ref, ob_ref, zr_ref, ma_ref, mb_ref,
                 wa_ref, wb_ref, wo_ref, g_ref, out_ref):
    f32 = lambda r: r[...].astype(F32)
    gw = oa_ref.shape[2]
    ya = sum(_dot((oa_ref[g].astype(F32) * zn_ref[:, g * gw:(g + 1) * gw].astype(F32)).astype(BF16),
                  wa_ref[g * gw:(g + 1) * gw, :]) for g in range(oa_ref.shape[0]))
    yb = _dot((f32(ob_ref) * f32(zr_ref)).astype(BF16), wb_ref[...])
    merged = f32(ma_ref) * ya + f32(mb_ref) * yb
    y = _dot(merged.astype(BF16), wo_ref[...])
    ms = jnp.mean(y * y, axis=-1, keepdims=True)
    out_ref[...] = x_ref[...] + y * lax.rsqrt(ms + RMS_EPS) * g_ref[...]


def _post(x2, oa, zn, ob, zr, ma, mb, wa, wb, wo, g_post):
    S = x2.shape[0]
    tm = POST_ROW_TILE
    row = lambda a: (pl.BlockSpec((tm, a.shape[1]), lambda i: (i, 0)) if a.ndim == 2 else
                     pl.BlockSpec((a.shape[0], tm, a.shape[2]), lambda i: (0, i, 0)))
    whole = lambda a: pl.BlockSpec(a.shape, lambda i: (0, 0))
    rows = (x2, oa, zn, ob, zr, ma, mb)
    consts = (wa, wb, wo, g_post)
    return pl.pallas_call(
        _post_kernel,
        grid=(S // tm,),
        in_specs=[row(a) for a in rows] + [whole(a) for a in consts],
        out_specs=row(x2),
        out_shape=jax.ShapeDtypeStruct(x2.shape, x2.dtype),
        compiler_params=_params(
            ("arbitrary",),
            [((tm, a.size // S), a.dtype) for a in rows + (x2,)] + [(a.shape, a.dtype) for a in consts]),
        name="post",
    )(*rows, *consts)


def _rotary_frequencies():
    nsa_inv = 1.0 / (ROPE_THETA ** (jnp.arange(0, ROPE_DIM, 2, dtype=F32) / ROPE_DIM))
    ret_inv = 1.0 / (RET_ROPE_BASE ** jnp.linspace(0.0, 1.0, RET_DK // 2, dtype=F32))
    row = jnp.concatenate([ret_inv, nsa_inv, jnp.zeros((LANE - RET_DK // 2 - ROPE_DIM // 2,), F32)])
    return jnp.pad(row[None, :], ((0, 7), (0, 0)))


def kernel(x, norm_pre, w_in, b_nsa_gate, cmp_pe_k, cmp_w1_k, cmp_w2_k, cmp_pe_v, cmp_w1_v,
           cmp_w2_v, w_nsa_o, w_ret_o, w_out, norm_post):
    B, S, _ = x.shape
    assert B == 1 and S % (2 * TK) == 0 and S >= WIN_KEYS
    assert S % (SEL_BLOCK * 8 * N_SIZE_CLASSES) == 0 and (S // TQ) % N_SIZE_CLASSES == 0
    x2 = x.reshape(S, D_MODEL)

    gate_cols = lambda a: jnp.pad(
        a.reshape(a.shape[:-1] + (NSA_GROUPS, NSA_HPG * 3)),
        [(0, 0)] * (a.ndim - 1) + [(0, 0), (0, GATE_ROWS - NSA_HPG * 3)]).reshape(a.shape[:-1] + (-1,))
    g0 = sum(PROJ_SIZES[:7])
    g1 = g0 + NSA_HEADS * 3
    gate_seg = gate_cols(w_in[:, g0:g1])
    gate_seg = jnp.pad(gate_seg, ((0, 0), (0, LANE - gate_seg.shape[1])))
    w_hi = jnp.concatenate([w_in[:, :g0], gate_seg, w_in[:, g1:]], axis=1).astype(BF16)
    bias = gate_cols(b_nsa_gate)
    bias = jnp.pad(bias, (0, LANE - bias.shape[0])).reshape(1, LANE)

    (qT, kc, vc, ks, vsT, kw, vwT, gT, zn, qr, kr, vr, zr, ma, mb) = _inproj(
        x2, norm_pre.reshape(1, D_MODEL), w_hi, bias, _rotary_frequencies())

    kc4 = _compress(kc, cmp_pe_k, cmp_w1_k, cmp_w2_k, keys=True)
    vcT = _compress(vc, cmp_pe_v, cmp_w1_v, cmp_w2_v, keys=False)

    oa = _nsa(qT, gT, kc4, vcT, ks, vsT, kw, vwT)
    ob = _retention(qr, kr, vr)

    out = _post(x2, oa, zn, ob, zr, ma, mb, w_nsa_o.astype(BF16), w_ret_o.astype(BF16),
                w_out.astype(BF16), norm_post.reshape(1, D_MODEL))
    return out.reshape(B, S, D_MODEL)
```

```python
import functools
import math

import jax
import jax.numpy as jnp
from jax import lax
from jax.experimental import pallas as pl
from jax.experimental.pallas import tpu as pltpu

F32 = jnp.float32
BF16 = jnp.bfloat16

D_MODEL = 1024
NSA_HEADS = 8
NSA_GROUPS = 2
NSA_HPG = NSA_HEADS // NSA_GROUPS
NSA_DH = 64
CMP_LEN = 32
CMP_STRIDE = 16
CMP_HIDDEN = 256
SEL_BLOCK = 64
SEL_TOPN = 16
WINDOW = 512
ROPE_THETA = 500000.0
ROPE_DIM = NSA_DH // 4
NSA_SCALE = NSA_DH ** -0.5
RET_HEADS = 4
RET_DK = 128
RET_DV = 256
RET_ROPE_BASE = 10000.0
RET_SCALE = RET_DK ** -0.5
RMS_EPS = 1e-6
GN_EPS = 1e-6
PROJ_SIZES = (512, 128, 128, 128, 128, 128, 128, 24, 512, 512, 512, 1024, 1024, 1024, 1024)

LANE = 128
V7X_VMEM_LIMIT_BYTES = 56 * 1024 * 1024
VMEM_TEMP_FLOOR_BYTES = 8 * 1024 * 1024

ROW_TILE = 256
POST_ROW_TILE = 512
TQ = 128
TK = 512
BLK_PER_TILE = TK // SEL_BLOCK
WIN_KEYS = WINDOW + TQ
RET_CHUNK = 256
V_ROWS = 80
NEG = -1e30
LOG2E = math.log2(math.e)
K_LANES = LANE
GATE_ROWS = 16
SC_PAD = 8
N_SIZE_CLASSES = 8

_PAD_SIZES = tuple(128 if s == 24 else s for s in PROJ_SIZES)
_OFF = [0]
for _s in _PAD_SIZES:
    _OFF.append(_OFF[-1] + _s)


def _dot(a, b):
    return jnp.dot(a, b, preferred_element_type=F32)


def _split(a):
    hi = a.astype(BF16)
    lo = (a - hi.astype(F32)).astype(BF16)
    return hi, lo


def _dot3(a_hi, a_lo, b_hi, b_lo):
    return _dot(a_hi, b_hi) + _dot(a_lo, b_hi) + _dot(a_hi, b_lo)


def _nbytes(shape, dtype):
    return math.prod(d for d in shape if d is not None) * jnp.dtype(dtype).itemsize


def _params(sem, blocks, scratch=()):
    est = 2 * sum(_nbytes(*b) for b in blocks) + sum(_nbytes(*b) for b in scratch)
    limit = min(V7X_VMEM_LIMIT_BYTES, est + est // 4 + VMEM_TEMP_FLOOR_BYTES)
    return pltpu.CompilerParams(dimension_semantics=sem, vmem_limit_bytes=limit)


def _inproj_kernel(x_ref, g_ref, whi_ref, bias_ref, freq_ref,
                   qT_ref, kc_ref, vc_ref, ks_ref, vsT_ref, kw_ref, vwT_ref, gT_ref,
                   zn_ref, qr_ref, kr_ref, vr_ref, zr_ref, ma_ref, mb_ref):
    tm = x_ref.shape[0]
    x = x_ref[...]
    ms = jnp.mean(x * x, axis=-1, keepdims=True)
    h = x * lax.rsqrt(ms + RMS_EPS) * g_ref[...]
    h_hi = h.astype(BF16)

    def mm(lo, hi):
        return _dot(h_hi, whi_ref[:, lo:hi])

    lane = lax.broadcasted_iota(jnp.int32, (tm, LANE), 1)
    row = pl.program_id(0) * tm + lax.broadcasted_iota(jnp.int32, (tm, LANE), 0)
    first_group = lane < NSA_DH
    pos = row.astype(F32)

    half = ROPE_DIM // 2
    dim = jnp.bitwise_and(lane, NSA_DH - 1)
    ang = pos * freq_ref[0:1, :]
    c_all, s_all = jnp.cos(ang), jnp.sin(ang)
    c_sw, s_sw = pltpu.roll(c_all, LANE // 2, 1), pltpu.roll(s_all, LANE // 2, 1)
    cr = jnp.where(first_group, c_all, c_sw)
    sr = jnp.where(first_group, -s_all, s_sw)
    lo_half, hi_half = dim < half, (dim >= half) & (dim < ROPE_DIM)
    c_lo, s_lo = jnp.where(first_group, c_sw, c_all), jnp.where(first_group, s_sw, s_all)
    c_hi, s_hi = pltpu.roll(c_lo, half, 1), pltpu.roll(s_lo, half, 1)
    cn = jnp.where(lo_half, c_lo, jnp.where(hi_half, c_hi, 1.0))
    san = jnp.where(hi_half, s_hi, 0.0)
    sbn = jnp.where(lo_half, -s_lo, 0.0)

    def rot_nsa(p):
        return p * cn + pltpu.roll(p, half, 1) * san + pltpu.roll(p, LANE - half, 1) * sbn

    def rot_ret(p):
        return p * cr + pltpu.roll(p, LANE // 2, 1) * sr

    def key_rows(p, fill):
        swapped = pltpu.roll(p, NSA_DH, 1)
        return [jnp.where(first_group, pg, fill).astype(BF16) for pg in (p, swapped)]

    ones_row = jnp.where(lax.broadcasted_iota(jnp.int32, (V_ROWS - NSA_DH, tm), 0) == 0, 1.0, 0.0)

    def value_rows(p):
        pT = p.T
        return [jnp.concatenate([pT[g * NSA_DH:(g + 1) * NSA_DH, :], ones_row], axis=0).astype(BF16)
                for g in range(NSA_GROUPS)]

    p = mm(0, _OFF[2])
    for c in range(4):
        qT = (rot_nsa(p[:, c * LANE:(c + 1) * LANE]) * NSA_SCALE).T
        for j in range(tm // TQ):
            qT_ref[j, c * LANE:(c + 1) * LANE, :] = qT[:, j * TQ:(j + 1) * TQ]
    kc_ref[...] = rot_nsa(p[:, 4 * LANE:5 * LANE])

    p = mm(_OFF[2], _OFF[8])
    vc_ref[...] = p[:, 0 * LANE:1 * LANE]
    blk_in_tile = jnp.right_shift(jnp.bitwise_and(row, TK - 1), SEL_BLOCK.bit_length() - 1)
    onehot = jnp.where(lane - NSA_DH == blk_in_tile, 1.0, 0.0)
    ks = key_rows(rot_nsa(p[:, 1 * LANE:2 * LANE]), onehot)
    vs = value_rows(p[:, 2 * LANE:3 * LANE])
    kw = key_rows(rot_nsa(p[:, 3 * LANE:4 * LANE]), 0.0)
    vw = value_rows(p[:, 4 * LANE:5 * LANE])
    gT = jax.nn.sigmoid(p[:, 5 * LANE:6 * LANE] + bias_ref[...]).T
    for g in range(NSA_GROUPS):
        ks_ref[g], vsT_ref[g], kw_ref[g], vwT_ref[g] = ks[g], vs[g], kw[g], vw[g]
        gT_ref[g] = gT[g * GATE_ROWS:(g + 1) * GATE_ROWS, :]

    z = mm(_OFF[8], _OFF[9])
    zn_ref[...] = (z * jax.nn.sigmoid(z)).astype(BF16)

    p = mm(_OFF[9], _OFF[11])
    for c in range(4):
        qr_ref[:, c * LANE:(c + 1) * LANE] = rot_ret(p[:, c * LANE:(c + 1) * LANE]).astype(BF16)
    for c in range(4):
        kr_ref[:, c * LANE:(c + 1) * LANE] = (
            rot_ret(p[:, (4 + c) * LANE:(5 + c) * LANE]) * RET_SCALE).astype(BF16)

    vr_ref[...] = mm(_OFF[11], _OFF[12]).astype(BF16)
    z = mm(_OFF[12], _OFF[13])
    zr_ref[...] = (z * jax.nn.sigmoid(z)).astype(BF16)
    ma_ref[...] = jax.nn.sigmoid(mm(_OFF[13], _OFF[14])).astype(BF16)
    mb_ref[...] = jax.nn.sigmoid(mm(_OFF[14], _OFF[15])).astype(BF16)


def _inproj(x2, norm_pre, w_hi, bias, freqs):
    S = x2.shape[0]
    tm = ROW_TILE
    G = NSA_GROUPS
    row = lambda w: pl.BlockSpec((tm, w), lambda i: (i, 0))
    whole = lambda a: pl.BlockSpec(a.shape, lambda i: (0,) * a.ndim)
    g_row = lambda w: pl.BlockSpec((G, tm, w), lambda i: (0, i, 0))
    g_col = lambda r: pl.BlockSpec((G, r, tm), lambda i: (0, 0, i))
    sds = jax.ShapeDtypeStruct
    outs = [
        (pl.BlockSpec((tm // TQ, NSA_HEADS * NSA_DH, TQ), lambda i: (i, 0, 0)),
         sds((S // TQ, NSA_HEADS * NSA_DH, TQ), F32)),
        (row(LANE), sds((S, LANE), F32)),
        (row(LANE), sds((S, LANE), F32)),
        (g_row(K_LANES), sds((G, S, K_LANES), BF16)),
        (g_col(V_ROWS), sds((G, V_ROWS, S), BF16)),
        (g_row(K_LANES), sds((G, S, K_LANES), BF16)),
        (g_col(V_ROWS), sds((G, V_ROWS, S), BF16)),
        (g_col(GATE_ROWS), sds((G, GATE_ROWS, S), F32)),
        (row(512), sds((S, 512), BF16)),
        (row(512), sds((S, 512), BF16)), (row(512), sds((S, 512), BF16)),
        (row(1024), sds((S, 1024), BF16)), (row(1024), sds((S, 1024), BF16)),
        (row(1024), sds((S, 1024), BF16)), (row(1024), sds((S, 1024), BF16)),
    ]
    return pl.pallas_call(
        _inproj_kernel,
        grid=(S // tm,),
        in_specs=[row(D_MODEL), whole(norm_pre), whole(w_hi), whole(bias), whole(freqs)],
        out_specs=[o[0] for o in outs],
        out_shape=[o[1] for o in outs],
        compiler_params=_params(
            ("arbitrary",),
            [((tm, D_MODEL), F32), (w_hi.shape, BF16)] + [(o[0].block_shape, o[1].dtype) for o in outs]),
        name="inproj",
    )(x2, norm_pre, w_hi, bias, freqs)


def _compress_kernel(x_ref, pe_ref, w1_ref, w2_ref, o_ref, *, keys):
    NC = x_ref.shape[0] // CMP_STRIDE
    g = pl.program_id(0)
    first_half = lax.broadcasted_iota(jnp.int32, (NC, LANE), 1) < NSA_DH
    tokens = []
    for u in range(CMP_STRIDE // 2):
        a = x_ref[pl.ds(2 * u, NC, stride=CMP_STRIDE), :]
        b = x_ref[pl.ds(2 * u + 1, NC, stride=CMP_STRIDE), :]
        ar, br = pltpu.roll(a, NSA_DH, 1), pltpu.roll(b, NSA_DH, 1)
        tokens.append(jnp.where(first_half, jnp.where(g == 0, a, ar), jnp.where(g == 0, br, b)))
    xg = jnp.concatenate(tokens, axis=1)
    prod = lambda a, b: _dot3(*_split(a), *_split(b))
    half = xg.shape[1]
    first = prod(xg, w1_ref[0:half, :])
    second = prod(xg, w1_ref[half:2 * half, :])
    nxt = pltpu.roll(second, NC - 1, 0)
    pe_term = prod(pe_ref[...], w1_ref[...])[0:1, :]
    hid = first + nxt + pe_term
    act = hid * (0.5 * (1.0 + jnp.tanh(math.sqrt(2.0 / math.pi) * (hid + 0.044715 * (hid * hid * hid)))))
    out = prod(act, w2_ref[...])
    if keys:
        hi, lo = _split(out)
        o_ref[...] = jnp.concatenate([hi, hi, lo, lo], axis=1)
    else:
        ones_row = jnp.where(lax.broadcasted_iota(jnp.int32, (V_ROWS - NSA_DH, NC), 0) == 0, 1.0, 0.0)
        o_ref[...] = jnp.concatenate([out.T, ones_row], axis=0).astype(BF16)


def _compress(raw, pe, w1, w2, keys):
    S = raw.shape[0]
    NC = S // CMP_STRIDE
    G = NSA_GROUPS
    half = CMP_STRIDE * NSA_DH
    pe8 = jnp.pad(pe.reshape(1, 2 * half), ((0, 7), (0, 0)))
    if keys:
        out_spec = pl.BlockSpec((None, NC, 4 * NSA_DH), lambda g: (g, 0, 0))
        out_shape = jax.ShapeDtypeStruct((G, NC, 4 * NSA_DH), BF16)
    else:
        out_spec = pl.BlockSpec((None, V_ROWS, NC), lambda g: (g, 0, 0))
        out_shape = jax.ShapeDtypeStruct((G, V_ROWS, NC), BF16)
    whole = lambda a: pl.BlockSpec(a.shape, lambda g: (0,) * a.ndim)
    return pl.pallas_call(
        functools.partial(_compress_kernel, keys=keys),
        grid=(G,),
        in_specs=[whole(raw), whole(pe8), whole(w1), whole(w2)],
        out_specs=out_spec,
        out_shape=out_shape,
        compiler_params=_params(
            ("arbitrary",),
            [(raw.shape, F32), (w1.shape, F32), (w2.shape, F32), (out_shape.shape[1:], BF16)],
            [((NC, 2 * CMP_HIDDEN), F32), ((NC, half), F32)]),
        name="compress_k" if keys else "compress_v",
    )(raw, pe8, w1, w2)


def _nsa_kernel(qT_ref, gT_ref, kc4_ref, vcT_ref, ks_ref, vsT_ref, kw_ref, vwT_ref, o_ref,
                sb_ref, sa_ref, sbuf_ref, sc_ref, oc_ref, ow_ref, m_ref, acc_ref):
    i = pl.program_id(1)
    NC = kc4_ref.shape[0]
    NB = sb_ref.shape[0]
    n_q = pl.num_programs(1)
    heads = range(NSA_HPG)
    lanes = lambda a, h: a[:, h * TQ:(h + 1) * TQ]
    colmax = lambda a: jnp.max(a, axis=0, keepdims=True)
    colsum = lambda a: jnp.sum(a, axis=0, keepdims=True)

    qT = qT_ref[...] * LOG2E
    Q = jnp.concatenate([qT[h * NSA_DH:(h + 1) * NSA_DH, :] for h in heads], axis=1)
    q_hi, q_lo = _split(Q)
    q4 = jnp.concatenate([q_hi, q_lo, q_hi, q_lo], axis=0)
    t = i * TQ + lax.broadcasted_iota(jnp.int32, (1, TQ), 1)
    cur = jnp.right_shift(t, SEL_BLOCK.bit_length() - 1)

    def window_branch():
        w0 = pl.multiple_of(jnp.maximum(i * TQ - WINDOW, 0), TQ)
        q_pad = jnp.concatenate([q_hi, jnp.zeros((K_LANES - NSA_DH, NSA_HPG * TQ), BF16)], axis=0)
        s = _dot(kw_ref[pl.ds(w0, WIN_KEYS), :], q_pad)
        key = w0 + lax.broadcasted_iota(jnp.int32, (WIN_KEYS, TQ), 0)
        mask_w = (key <= t) & (key > t - WINDOW)
        p_heads = []
        for h in heads:
            sm = jnp.where(mask_w, lanes(s, h), NEG)
            p_heads.append(jnp.exp2(sm - colmax(sm)).astype(BF16))
        accw = _dot(vwT_ref[:, pl.ds(w0, WIN_KEYS)], jnp.concatenate(p_heads, axis=1))
        owT = accw[0:NSA_DH, :] * (1.0 / accw[NSA_DH:NSA_DH + 1, :])
        ow_ref[...] = owT

    def compressed_and_select(ncp, nbp):
        s = _dot(kc4_ref[0:ncp, :], q4)
        c_end = lax.broadcasted_iota(jnp.int32, (ncp, TQ), 0) * CMP_STRIDE + (CMP_LEN - 1)
        mask_c = c_end <= t
        p_heads, inv = [], []
        for h in heads:
            sm = jnp.where(mask_c, lanes(s, h), -jnp.inf)
            m = colmax(sm)
            m = jnp.where(m == -jnp.inf, 0.0, m)
            p = jnp.exp2(sm - m)
            inv.append(1.0 / jnp.maximum(colsum(p), jnp.finfo(F32).tiny))
            sc_ref[h, 0:SC_PAD, :] = jnp.zeros((SC_PAD, TQ), F32)
            sc_ref[h, SC_PAD:SC_PAD + ncp, :] = p
            p_heads.append(p.astype(BF16))
        oc_ref[...] = (_dot(vcT_ref[:, 0:ncp], jnp.concatenate(p_heads, axis=1))
                       * jnp.concatenate(inv, axis=1))
        per_sel = SEL_BLOCK // CMP_STRIDE
        imp = jnp.zeros((nbp, TQ), F32)
        for h in heads:
            tot = sc_ref[h, pl.ds(SC_PAD - 1, nbp, stride=per_sel), :]
            for r in range(per_sel):
                tot = tot + sc_ref[h, pl.ds(SC_PAD + r, nbp, stride=per_sel), :]
            imp = imp + tot * inv[h]
        blk = lax.broadcasted_iota(jnp.int32, (nbp, TQ), 0)
        blk_f = blk.astype(F32)
        valid = blk <= cur
        forced = (blk == 0) | (blk == cur) | (blk == cur - 1)
        free = valid & jnp.logical_not(forced)
        score = jnp.where(free, imp, -jnp.inf)
        for _ in range(min(SEL_TOPN, NB) - 3):
            mx = colmax(score)
            idx = jnp.min(jnp.where(score == mx, blk_f, float(NB)), axis=0, keepdims=True)
            score = jnp.where(blk_f == idx, -jnp.inf, score)
        picked = valid & (forced | (score == -jnp.inf))
        sb_ref[0:nbp, :] = jnp.where(picked, 0.0, NEG)
        if nbp < NB:
            sb_ref[nbp:NB, :] = jnp.full((NB - nbp, TQ), NEG, F32)
        window_branch()

    size_class = lax.div(i * N_SIZE_CLASSES, n_q)
    for k in range(N_SIZE_CLASSES):
        pl.when(size_class == k)(functools.partial(
            compressed_and_select, NC * (k + 1) // N_SIZE_CLASSES, NB * (k + 1) // N_SIZE_CLASSES))

    pad_rows = jnp.zeros((K_LANES - NSA_DH - 2 * BLK_PER_TILE, NSA_HPG * TQ), BF16)

    def scores(j):
        k0 = pl.multiple_of(j * TK, TK)
        sbt = sb_ref[pl.ds(pl.multiple_of(j * BLK_PER_TILE, BLK_PER_TILE), BLK_PER_TILE), :]
        rows = jnp.concatenate([sbt, jnp.zeros_like(sbt)], axis=0)
        rows = jnp.concatenate([rows] * NSA_HPG, axis=1).astype(BF16)
        w = jnp.concatenate([q_hi, rows, pad_rows], axis=0)
        s = _dot(ks_ref[pl.ds(k0, TK), :], w)
        return s, colmax(s)

    def attend(j, s, mt, carry, causal):
        m, acc = carry
        k0 = pl.multiple_of(j * TK, TK)
        if causal:
            keep = k0 + lax.broadcasted_iota(jnp.int32, (TK, TQ), 0) <= t
            s = jnp.concatenate([jnp.where(keep, lanes(s, h), NEG) for h in heads], axis=1)
            mt = colmax(s)
        m_new = jnp.maximum(m, mt)
        alpha = jnp.exp2(m - m_new)
        p = jnp.exp2(s - m_new).astype(BF16)
        acc = acc * alpha + _dot(vsT_ref[:, pl.ds(k0, TK)], p)
        return m_new, acc

    jd = lax.div(i, TK // TQ)
    n_pairs = lax.div(jd, 2)
    sa_ref[...], mt0 = scores(0)

    def pair(pp, carry):
        m, acc, mta = carry
        sbuf_ref[...], mtb = scores(2 * pp + 1)
        m, acc = attend(2 * pp, sa_ref[...], mta, (m, acc), False)
        sa_ref[...], mta = scores(2 * pp + 2)
        m, acc = attend(2 * pp + 1, sbuf_ref[...], mtb, (m, acc), False)
        return m, acc, mta

    carry = (jnp.full((1, NSA_HPG * TQ), NEG, F32), jnp.zeros((V_ROWS, NSA_HPG * TQ), F32), mt0)
    quad = lambda qq, c: pair(2 * qq + 1, pair(2 * qq, c))
    octo = lambda oo, c: quad(2 * oo + 1, quad(2 * oo, c))
    n_hexs, n_octs, n_quads = lax.div(n_pairs, 8), lax.div(n_pairs, 4), lax.div(n_pairs, 2)
    carry = lax.fori_loop(0, n_hexs, lambda hh, c: octo(2 * hh + 1, octo(2 * hh, c)), carry)
    carry = lax.fori_loop(2 * n_hexs, n_octs, octo, carry)
    carry = lax.fori_loop(2 * n_octs, n_quads, quad, carry)
    carry = lax.fori_loop(2 * n_quads, n_pairs, pair, carry)
    sbuf_ref[...] = scores(2 * n_pairs + 1)[0]

    m_ref[...], acc_ref[...] = attend(2 * n_pairs, sa_ref[...], None, carry[:2], True)

    @pl.when(jd != 2 * n_pairs)
    def _():
        m_ref[...], acc_ref[...] = attend(
            2 * n_pairs + 1, sbuf_ref[...], None, (m_ref[...], acc_ref[...]), True)

    acc = acc_ref[...]
    osT = acc[0:NSA_DH, :] * (1.0 / acc[NSA_DH:NSA_DH + 1, :])

    gT = gT_ref[...]
    ocT, owT = oc_ref[...], ow_ref[...]
    outs = []
    for h in heads:
        g_c, g_s, g_w = (gT[3 * h + b:3 * h + b + 1, :] for b in range(3))
        outs.append(g_c * lanes(ocT, h)[0:NSA_DH, :] + g_s * lanes(osT, h) + g_w * lanes(owT, h))
    o_ref[...] = jnp.concatenate(outs, axis=0).T.astype(BF16)


def _nsa(qT, gT, kc4, vcT, ks, vsT, kw, vwT):
    G, S = ks.shape[0], ks.shape[1]
    NC, NB = kc4.shape[1], S // SEL_BLOCK
    per_g = lambda a: pl.BlockSpec((None,) + a.shape[1:], lambda g, i: (g,) + (0,) * (a.ndim - 1))
    wide = NSA_HPG * TQ
    scratch = [(NB, TQ),
               (TK, wide), (TK, wide),
               (NSA_HPG, SC_PAD + NC, TQ),
               (V_ROWS, wide), (NSA_DH, wide),
               (1, wide), (V_ROWS, wide)]
    return pl.pallas_call(
        _nsa_kernel,
        grid=(G, S // TQ),
        in_specs=[
            pl.BlockSpec((None, NSA_HPG * NSA_DH, TQ), lambda g, i: (i, g, 0)),
            pl.BlockSpec((None, GATE_ROWS, TQ), lambda g, i: (g, 0, i)),
            per_g(kc4), per_g(vcT), per_g(ks), per_g(vsT), per_g(kw), per_g(vwT),
        ],
        out_specs=pl.BlockSpec((None, TQ, NSA_HPG * NSA_DH), lambda g, i: (g, i, 0)),
        out_shape=jax.ShapeDtypeStruct((G, S, NSA_HPG * NSA_DH), BF16),
        scratch_shapes=[pltpu.VMEM(shape, F32) for shape in scratch],
        compiler_params=_params(
            ("arbitrary", "arbitrary"),
            [(a.shape[1:], a.dtype) for a in (kc4, vcT, ks, vsT, kw, vwT)]
            + [((NSA_HPG * NSA_DH, TQ), F32), ((TQ, NSA_HPG * NSA_DH), BF16)],
            [(shape, F32) for shape in scratch]),
        name="nsa",
    )(qT, gT, kc4, vcT, ks, vsT, kw, vwT)


def _ret_kernel(q_ref, k_ref, v_ref, o_ref, r_ref, dm_ref, qd_ref, kd_ref):
    C = RET_CHUNK
    log_g = [math.log(1.0 - 2.0 ** (-5.0 - h)) for h in range(RET_HEADS)]

    @pl.when(pl.program_id(0) == 0)
    def _():
        r_ref[...] = jnp.zeros(r_ref.shape, F32)
        diff = (lax.broadcasted_iota(jnp.int32, (C, C), 0)
                - lax.broadcasted_iota(jnp.int32, (C, C), 1)).astype(F32)
        n = lax.broadcasted_iota(jnp.int32, (C, RET_DK), 0).astype(F32)
        for h in range(RET_HEADS):
            dm_ref[h] = jnp.where(diff >= 0.0, jnp.exp(jnp.maximum(diff, 0.0) * log_g[h]), 0.0)
            qd_ref[h] = jnp.exp((n + 1.0) * log_g[h])
            kd_ref[h] = jnp.exp((C - 1.0 - n) * log_g[h])

    for h in range(RET_HEADS):
        q = q_ref[:, h * RET_DK:(h + 1) * RET_DK]
        k = k_ref[:, h * RET_DK:(h + 1) * RET_DK]
        v = v_ref[:, h * RET_DV:(h + 1) * RET_DV]
        att = lax.dot_general(q, k, (((1,), (1,)), ((), ())), preferred_element_type=F32)
        o = _dot((att * dm_ref[h]).astype(BF16), v)
        r = r_ref[h]
        qd = (q.astype(F32) * qd_ref[h]).astype(BF16)
        o = o + _dot(qd, r.astype(BF16))
        kd = (k.astype(F32) * kd_ref[h]).astype(BF16)
        r_ref[h] = math.exp(C * log_g[h]) * r + lax.dot_general(
            kd, v, (((0,), (0,)), ((), ())), preferred_element_type=F32)
        mu = jnp.mean(o, axis=-1, keepdims=True)
        d = o - mu
        var = jnp.mean(d * d, axis=-1, keepdims=True)
        o_ref[:, h * RET_DV:(h + 1) * RET_DV] = (d * lax.rsqrt(var + GN_EPS)).astype(BF16)


def _retention(qr, kr, vr):
    S = qr.shape[0]
    C = RET_CHUNK
    row = lambda w: pl.BlockSpec((C, w), lambda n: (n, 0))
    scratch = [(RET_HEADS, RET_DK, RET_DV),
               (RET_HEADS, C, C),
               (RET_HEADS, C, RET_DK), (RET_HEADS, C, RET_DK)]
    return pl.pallas_call(
        _ret_kernel,
        grid=(S // C,),
        in_specs=[row(RET_HEADS * RET_DK), row(RET_HEADS * RET_DK), row(RET_HEADS * RET_DV)],
        out_specs=row(RET_HEADS * RET_DV),
        out_shape=jax.ShapeDtypeStruct((S, RET_HEADS * RET_DV), BF16),
        scratch_shapes=[pltpu.VMEM(shape, F32) for shape in scratch],
        compiler_params=_params(
            ("arbitrary",),
            [((C, w), BF16) for w in (RET_HEADS * RET_DK, RET_HEADS * RET_DK, 2 * RET_HEADS * RET_DV)],
            [(shape, F32) for shape in scratch]),
        name="retention",
    )(qr, kr, vr)


def _post_kernel(x_ref, oa_ref, zn_ref, ob_ref, zr_ref, ma_ref, mb_ref,
                 wa_ref, wb_ref, wo_ref, g_ref, out_ref):
    f32 = lambda r: r[...].astype(F32)
    gw = oa_ref.shape[2]
    ya = sum(_dot((oa_ref[g].astype(F32) * zn_ref[:, g * gw:(g + 1) * gw].astype(F32)).astype(BF16),
                  wa_ref[g * gw:(g + 1) * gw, :]) for g in range(oa_ref.shape[0]))
    yb = _dot((f32(ob_ref) * f32(zr_ref)).astype(BF16), wb_ref[...])
    merged = f32(ma_ref) * ya + f32(mb_ref) * yb
    y = _dot(merged.astype(BF16), wo_ref[...])
    ms = jnp.mean(y * y, axis=-1, keepdims=True)
    out_ref[...] = x_ref[...] + y * lax.rsqrt(ms + RMS_EPS) * g_ref[...]


def _post(x2, oa, zn, ob, zr, ma, mb, wa, wb, wo, g_post):
    S = x2.shape[0]
    tm = POST_ROW_TILE
    row = lambda a: (pl.BlockSpec((tm, a.shape[1]), lambda i: (i, 0)) if a.ndim == 2 else
                     pl.BlockSpec((a.shape[0], tm, a.shape[2]), lambda i: (0, i, 0)))
    whole = lambda a: pl.BlockSpec(a.shape, lambda i: (0, 0))
    rows = (x2, oa, zn, ob, zr, ma, mb)
    consts = (wa, wb, wo, g_post)
    return pl.pallas_call(
        _post_kernel,
        grid=(S // tm,),
        in_specs=[row(a) for a in rows] + [whole(a) for a in consts],
        out_specs=row(x2),
        out_shape=jax.ShapeDtypeStruct(x2.shape, x2.dtype),
        compiler_params=_params(
            ("arbitrary",),
            [((tm, a.size // S), a.dtype) for a in rows + (x2,)] + [(a.shape, a.dtype) for a in consts]),
        name="post",
    )(*rows, *consts)


def _rotary_frequencies():
    nsa_inv = 1.0 / (ROPE_THETA ** (jnp.arange(0, ROPE_DIM, 2, dtype=F32) / ROPE_DIM))
    ret_inv = 1.0 / (RET_ROPE_BASE ** jnp.linspace(0.0, 1.0, RET_DK // 2, dtype=F32))
    row = jnp.concatenate([ret_inv, nsa_inv, jnp.zeros((LANE - RET_DK // 2 - ROPE_DIM // 2,), F32)])
    return jnp.pad(row[None, :], ((0, 7), (0, 0)))


def kernel(x, norm_pre, w_in, b_nsa_gate, cmp_pe_k, cmp_w1_k, cmp_w2_k, cmp_pe_v, cmp_w1_v,
           cmp_w2_v, w_nsa_o, w_ret_o, w_out, norm_post):
    B, S, _ = x.shape
    assert B == 1 and S % (2 * TK) == 0 and S >= WIN_KEYS
    assert S % (SEL_BLOCK * 8 * N_SIZE_CLASSES) == 0 and (S // TQ) % N_SIZE_CLASSES == 0
    x2 = x.reshape(S, D_MODEL)

    gate_cols = lambda a: jnp.pad(
        a.reshape(a.shape[:-1] + (NSA_GROUPS, NSA_HPG * 3)),
        [(0, 0)] * (a.ndim - 1) + [(0, 0), (0, GATE_ROWS - NSA_HPG * 3)]).reshape(a.shape[:-1] + (-1,))
    g0 = sum(PROJ_SIZES[:7])
    g1 = g0 + NSA_HEADS * 3
    gate_seg = gate_cols(w_in[:, g0:g1])
    gate_seg = jnp.pad(gate_seg, ((0, 0), (0, LANE - gate_seg.shape[1])))
    w_hi = jnp.concatenate([w_in[:, :g0], gate_seg, w_in[:, g1:]], axis=1).astype(BF16)
    bias = gate_cols(b_nsa_gate)
    bias = jnp.pad(bias, (0, LANE - bias.shape[0])).reshape(1, LANE)

    (qT, kc, vc, ks, vsT, kw, vwT, gT, zn, qr, kr, vr, zr, ma, mb) = _inproj(
        x2, norm_pre.reshape(1, D_MODEL), w_hi, bias, _rotary_frequencies())

    kc4 = _compress(kc, cmp_pe_k, cmp_w1_k, cmp_w2_k, keys=True)
    vcT = _compress(vc, cmp_pe_v, cmp_w1_v, cmp_w2_v, keys=False)

    oa = _nsa(qT, gT, kc4, vcT, ks, vsT, kw, vwT)
    ob = _retention(qr, kr, vr)

    out = _post(x2, oa, zn, ob, zr, ma, mb, w_nsa_o.astype(BF16), w_ret_o.astype(BF16),
                w_out.astype(BF16), norm_post.reshape(1, D_MODEL))
    return out.reshape(B, S, D_MODEL)
```

```python
import functools
import math

import jax
import jax.numpy as jnp
from jax import lax
from jax.experimental import pallas as pl
from jax.experimental.pallas import tpu as pltpu

F32 = jnp.float32
BF16 = jnp.bfloat16

D_MODEL = 1024
NSA_HEADS = 8
NSA_GROUPS = 2
NSA_HPG = NSA_HEADS // NSA_GROUPS
NSA_DH = 64
CMP_LEN = 32
CMP_STRIDE = 16
CMP_HIDDEN = 256
SEL_BLOCK = 64
SEL_TOPN = 16
WINDOW = 512
ROPE_THETA = 500000.0
ROPE_DIM = NSA_DH // 4
NSA_SCALE = NSA_DH ** -0.5
RET_HEADS = 4
RET_DK = 128
RET_DV = 256
RET_ROPE_BASE = 10000.0
RET_SCALE = RET_DK ** -0.5
RMS_EPS = 1e-6
GN_EPS = 1e-6
PROJ_SIZES = (512, 128, 128, 128, 128, 128, 128, 24, 512, 512, 512, 1024, 1024, 1024, 1024)

LANE = 128
V7X_VMEM_LIMIT_BYTES = 56 * 1024 * 1024
VMEM_TEMP_FLOOR_BYTES = 8 * 1024 * 1024

ROW_TILE = 256
POST_ROW_TILE = 512
TQ = 128
TK = 512
BLK_PER_TILE = TK // SEL_BLOCK
WIN_KEYS = WINDOW + TQ
RET_CHUNK = 256
RET_CHUNKS_PER_STEP = 4
V_ROWS = 80
NEG = -1e30
LOG2E = math.log2(math.e)
K_LANES = LANE
GATE_ROWS = 16
SC_PAD = 8
N_SIZE_CLASSES = 8

_PAD_SIZES = tuple(128 if s == 24 else s for s in PROJ_SIZES)
_OFF = [0]
for _s in _PAD_SIZES:
    _OFF.append(_OFF[-1] + _s)


def _dot(a, b):
    return jnp.dot(a, b, preferred_element_type=F32)


def _split(a):
    hi = a.astype(BF16)
    lo = (a - hi.astype(F32)).astype(BF16)
    return hi, lo


def _dot3(a_hi, a_lo, b_hi, b_lo):
    return _dot(a_hi, b_hi) + _dot(a_lo, b_hi) + _dot(a_hi, b_lo)


def _nbytes(shape, dtype):
    return math.prod(d for d in shape if d is not None) * jnp.dtype(dtype).itemsize


def _params(sem, blocks, scratch=()):
    est = 2 * sum(_nbytes(*b) for b in blocks) + sum(_nbytes(*b) for b in scratch)
    limit = min(V7X_VMEM_LIMIT_BYTES, est + est // 4 + VMEM_TEMP_FLOOR_BYTES)
    return pltpu.CompilerParams(dimension_semantics=sem, vmem_limit_bytes=limit)


def _inproj_kernel(x_ref, g_ref, whi_ref, bias_ref, freq_ref,
                   qT_ref, kc_ref, vc_ref, ks_ref, vsT_ref, kw_ref, vwT_ref, gT_ref,
                   zn_ref, qr_ref, kr_ref, vr_ref, zr_ref, ma_ref, mb_ref):
    tm = x_ref.shape[0]
    x = x_ref[...]
    ms = jnp.mean(x * x, axis=-1, keepdims=True)
    h = x * lax.rsqrt(ms + RMS_EPS) * g_ref[...]
    h_hi = h.astype(BF16)

    def mm(lo, hi):
        return _dot(h_hi, whi_ref[:, lo:hi])

    lane = lax.broadcasted_iota(jnp.int32, (tm, LANE), 1)
    row = pl.program_id(0) * tm + lax.broadcasted_iota(jnp.int32, (tm, LANE), 0)
    first_group = lane < NSA_DH
    pos = row.astype(F32)

    half = ROPE_DIM // 2
    dim = jnp.bitwise_and(lane, NSA_DH - 1)
    ang = pos * freq_ref[0:1, :]
    c_all, s_all = jnp.cos(ang), jnp.sin(ang)
    c_sw, s_sw = pltpu.roll(c_all, LANE // 2, 1), pltpu.roll(s_all, LANE // 2, 1)
    cr = jnp.where(first_group, c_all, c_sw)
    sr = jnp.where(first_group, -s_all, s_sw)
    lo_half, hi_half = dim < half, (dim >= half) & (dim < ROPE_DIM)
    c_lo, s_lo = jnp.where(first_group, c_sw, c_all), jnp.where(first_group, s_sw, s_all)
    c_hi, s_hi = pltpu.roll(c_lo, half, 1), pltpu.roll(s_lo, half, 1)
    cn = jnp.where(lo_half, c_lo, jnp.where(hi_half, c_hi, 1.0))
    san = jnp.where(hi_half, s_hi, 0.0)
    sbn = jnp.where(lo_half, -s_lo, 0.0)

    def rot_nsa(p):
        return p * cn + pltpu.roll(p, half, 1) * san + pltpu.roll(p, LANE - half, 1) * sbn

    def rot_ret(p):
        return p * cr + pltpu.roll(p, LANE // 2, 1) * sr

    def key_rows(p, fill):
        swapped = pltpu.roll(p, NSA_DH, 1)
        return [jnp.where(first_group, pg, fill).astype(BF16) for pg in (p, swapped)]

    ones_row = jnp.where(lax.broadcasted_iota(jnp.int32, (V_ROWS - NSA_DH, tm), 0) == 0, 1.0, 0.0)

    def value_rows(p):
        pT = p.T
        return [jnp.concatenate([pT[g * NSA_DH:(g + 1) * NSA_DH, :], ones_row], axis=0).astype(BF16)
                for g in range(NSA_GROUPS)]

    p = mm(0, _OFF[2])
    for c in range(4):
        qT_ref[c * LANE:(c + 1) * LANE, :] = (rot_nsa(p[:, c * LANE:(c + 1) * LANE]) * NSA_SCALE).T
    kc_ref[...] = rot_nsa(p[:, 4 * LANE:5 * LANE])

    p = mm(_OFF[2], _OFF[8])
    vc_ref[...] = p[:, 0 * LANE:1 * LANE]
    blk_in_tile = jnp.right_shift(jnp.bitwise_and(row, TK - 1), SEL_BLOCK.bit_length() - 1)
    onehot = jnp.where(lane - NSA_DH == blk_in_tile, 1.0, 0.0)
    ks = key_rows(rot_nsa(p[:, 1 * LANE:2 * LANE]), onehot)
    vs = value_rows(p[:, 2 * LANE:3 * LANE])
    kw = key_rows(rot_nsa(p[:, 3 * LANE:4 * LANE]), 0.0)
    vw = value_rows(p[:, 4 * LANE:5 * LANE])
    gT = jax.nn.sigmoid(p[:, 5 * LANE:6 * LANE] + bias_ref[...]).T
    for g in range(NSA_GROUPS):
        ks_ref[g], vsT_ref[g], kw_ref[g], vwT_ref[g] = ks[g], vs[g], kw[g], vw[g]
        gT_ref[g] = gT[g * GATE_ROWS:(g + 1) * GATE_ROWS, :]

    z = mm(_OFF[8], _OFF[9])
    zn_ref[...] = (z * jax.nn.sigmoid(z)).astype(BF16)

    p = mm(_OFF[9], _OFF[11])
    for c in range(4):
        qr_ref[:, c * LANE:(c + 1) * LANE] = rot_ret(p[:, c * LANE:(c + 1) * LANE]).astype(BF16)
    for c in range(4):
        kr_ref[:, c * LANE:(c + 1) * LANE] = (
            rot_ret(p[:, (4 + c) * LANE:(5 + c) * LANE]) * RET_SCALE).astype(BF16)

    vr_ref[...] = mm(_OFF[11], _OFF[12]).astype(BF16)
    z = mm(_OFF[12], _OFF[13])
    zr_ref[...] = (z * jax.nn.sigmoid(z)).astype(BF16)
    ma_ref[...] = jax.nn.sigmoid(mm(_OFF[13], _OFF[14])).astype(BF16)
    mb_ref[...] = jax.nn.sigmoid(mm(_OFF[14], _OFF[15])).astype(BF16)


def _inproj(x2, norm_pre, w_hi, bias, freqs):
    S = x2.shape[0]
    tm = ROW_TILE
    G = NSA_GROUPS
    row = lambda w: pl.BlockSpec((tm, w), lambda i: (i, 0))
    whole = lambda a: pl.BlockSpec(a.shape, lambda i: (0,) * a.ndim)
    col = lambda r: pl.BlockSpec((r, tm), lambda i: (0, i))
    g_row = lambda w: pl.BlockSpec((G, tm, w), lambda i: (0, i, 0))
    g_col = lambda r: pl.BlockSpec((G, r, tm), lambda i: (0, 0, i))
    sds = jax.ShapeDtypeStruct
    outs = [
        (col(NSA_HEADS * NSA_DH), sds((NSA_HEADS * NSA_DH, S), F32)),
        (row(LANE), sds((S, LANE), F32)),
        (row(LANE), sds((S, LANE), F32)),
        (g_row(K_LANES), sds((G, S, K_LANES), BF16)),
        (g_col(V_ROWS), sds((G, V_ROWS, S), BF16)),
        (g_row(K_LANES), sds((G, S, K_LANES), BF16)),
        (g_col(V_ROWS), sds((G, V_ROWS, S), BF16)),
        (g_col(GATE_ROWS), sds((G, GATE_ROWS, S), F32)),
        (row(512), sds((S, 512), BF16)),
        (row(512), sds((S, 512), BF16)), (row(512), sds((S, 512), BF16)),
        (row(1024), sds((S, 1024), BF16)), (row(1024), sds((S, 1024), BF16)),
        (row(1024), sds((S, 1024), BF16)), (row(1024), sds((S, 1024), BF16)),
    ]
    return pl.pallas_call(
        _inproj_kernel,
        grid=(S // tm,),
        in_specs=[row(D_MODEL), whole(norm_pre), whole(w_hi), whole(bias), whole(freqs)],
        out_specs=[o[0] for o in outs],
        out_shape=[o[1] for o in outs],
        compiler_params=_params(
            ("arbitrary",),
            [((tm, D_MODEL), F32), (w_hi.shape, BF16)] + [(o[0].block_shape, o[1].dtype) for o in outs]),
        name="inproj",
    )(x2, norm_pre, w_hi, bias, freqs)


def _compress_kernel(x_ref, pe_ref, w1_ref, w2_ref, o_ref, *, keys):
    NC = x_ref.shape[0] // CMP_STRIDE
    g = pl.program_id(0)
    first_half = lax.broadcasted_iota(jnp.int32, (NC, LANE), 1) < NSA_DH
    tokens = []
    for u in range(CMP_STRIDE // 2):
        a = x_ref[pl.ds(2 * u, NC, stride=CMP_STRIDE), :]
        b = x_ref[pl.ds(2 * u + 1, NC, stride=CMP_STRIDE), :]
        ar, br = pltpu.roll(a, NSA_DH, 1), pltpu.roll(b, NSA_DH, 1)
        tokens.append(jnp.where(first_half, jnp.where(g == 0, a, ar), jnp.where(g == 0, br, b)))
    xg = jnp.concatenate(tokens, axis=1)
    prod = lambda a, b: _dot3(*_split(a), *_split(b))
    half = xg.shape[1]
    first = prod(xg, w1_ref[0:half, :])
    second = prod(xg, w1_ref[half:2 * half, :])
    nxt = pltpu.roll(second, NC - 1, 0)
    pe_term = prod(pe_ref[...], w1_ref[...])[0:1, :]
    hid = first + nxt + pe_term
    act = hid * (0.5 * (1.0 + jnp.tanh(math.sqrt(2.0 / math.pi) * (hid + 0.044715 * (hid * hid * hid)))))
    out = prod(act, w2_ref[...])
    if keys:
        hi, lo = _split(out)
        o_ref[...] = jnp.concatenate([hi, hi, lo, lo], axis=1)
    else:
        ones_row = jnp.where(lax.broadcasted_iota(jnp.int32, (V_ROWS - NSA_DH, NC), 0) == 0, 1.0, 0.0)
        o_ref[...] = jnp.concatenate([out.T, ones_row], axis=0).astype(BF16)


def _compress(raw, pe, w1, w2, keys):
    S = raw.shape[0]
    NC = S // CMP_STRIDE
    G = NSA_GROUPS
    half = CMP_STRIDE * NSA_DH
    pe8 = jnp.pad(pe.reshape(1, 2 * half), ((0, 7), (0, 0)))
    if keys:
        out_spec = pl.BlockSpec((None, NC, 4 * NSA_DH), lambda g: (g, 0, 0))
        out_shape = jax.ShapeDtypeStruct((G, NC, 4 * NSA_DH), BF16)
    else:
        out_spec = pl.BlockSpec((None, V_ROWS, NC), lambda g: (g, 0, 0))
        out_shape = jax.ShapeDtypeStruct((G, V_ROWS, NC), BF16)
    whole = lambda a: pl.BlockSpec(a.shape, lambda g: (0,) * a.ndim)
    return pl.pallas_call(
        functools.partial(_compress_kernel, keys=keys),
        grid=(G,),
        in_specs=[whole(raw), whole(pe8), whole(w1), whole(w2)],
        out_specs=out_spec,
        out_shape=out_shape,
        compiler_params=_params(
            ("arbitrary",),
            [(raw.shape, F32), (w1.shape, F32), (w2.shape, F32), (out_shape.shape[1:], BF16)],
            [((NC, 2 * CMP_HIDDEN), F32), ((NC, half), F32)]),
        name="compress_k" if keys else "compress_v",
    )(raw, pe8, w1, w2)


def _nsa_kernel(qT_ref, gT_ref, kc4_ref, vcT_ref, ks_ref, vsT_ref, kw_ref, vwT_ref, o_ref,
                sb_ref, sa_ref, sbuf_ref, sc_ref, oc_ref, ow_ref, m_ref, acc_ref):
    i = pl.program_id(1)
    NC = kc4_ref.shape[0]
    NB = sb_ref.shape[0]
    n_q = pl.num_programs(1)
    heads = range(NSA_HPG)
    lanes = lambda a, h: a[:, h * TQ:(h + 1) * TQ]
    colmax = lambda a: jnp.max(a, axis=0, keepdims=True)
    colsum = lambda a: jnp.sum(a, axis=0, keepdims=True)

    qT = qT_ref[...] * LOG2E
    Q = jnp.concatenate([qT[h * NSA_DH:(h + 1) * NSA_DH, :] for h in heads], axis=1)
    q_hi, q_lo = _split(Q)
    q4 = jnp.concatenate([q_hi, q_lo, q_hi, q_lo], axis=0)
    t = i * TQ + lax.broadcasted_iota(jnp.int32, (1, TQ), 1)
    cur = jnp.right_shift(t, SEL_BLOCK.bit_length() - 1)

    def window_branch():
        w0 = pl.multiple_of(jnp.maximum(i * TQ - WINDOW, 0), TQ)
        q_pad = jnp.concatenate([q_hi, jnp.zeros((K_LANES - NSA_DH, NSA_HPG * TQ), BF16)], axis=0)
        s = _dot(kw_ref[pl.ds(w0, WIN_KEYS), :], q_pad)
        key = w0 + lax.broadcasted_iota(jnp.int32, (WIN_KEYS, TQ), 0)
        mask_w = (key <= t) & (key > t - WINDOW)
        p_heads = []
        for h in heads:
            sm = jnp.where(mask_w, lanes(s, h), NEG)
            p_heads.append(jnp.exp2(sm - colmax(sm)).astype(BF16))
        accw = _dot(vwT_ref[:, pl.ds(w0, WIN_KEYS)], jnp.concatenate(p_heads, axis=1))
        owT = accw[0:NSA_DH, :] * (1.0 / accw[NSA_DH:NSA_DH + 1, :])
        ow_ref[...] = owT

    def compressed_and_select(ncp, nbp):
        s = _dot(kc4_ref[0:ncp, :], q4)
        c_end = lax.broadcasted_iota(jnp.int32, (ncp, TQ), 0) * CMP_STRIDE + (CMP_LEN - 1)
        mask_c = c_end <= t
        p_heads, inv = [], []
        for h in heads:
            sm = jnp.where(mask_c, lanes(s, h), -jnp.inf)
            m = colmax(sm)
            m = jnp.where(m == -jnp.inf, 0.0, m)
            p = jnp.exp2(sm - m)
            inv.append(1.0 / jnp.maximum(colsum(p), jnp.finfo(F32).tiny))
            sc_ref[h, 0:SC_PAD, :] = jnp.zeros((SC_PAD, TQ), F32)
            sc_ref[h, SC_PAD:SC_PAD + ncp, :] = p
            p_heads.append(p.astype(BF16))
        oc_ref[...] = (_dot(vcT_ref[:, 0:ncp], jnp.concatenate(p_heads, axis=1))
                       * jnp.concatenate(inv, axis=1))
        per_sel = SEL_BLOCK // CMP_STRIDE
        imp = jnp.zeros((nbp, TQ), F32)
        for h in heads:
            tot = sc_ref[h, pl.ds(SC_PAD - 1, nbp, stride=per_sel), :]
            for r in range(per_sel):
                tot = tot + sc_ref[h, pl.ds(SC_PAD + r, nbp, stride=per_sel), :]
            imp = imp + tot * inv[h]
        blk = lax.broadcasted_iota(jnp.int32, (nbp, TQ), 0)
        blk_f = blk.astype(F32)
        valid = blk <= cur
        forced = (blk == 0) | (blk == cur) | (blk == cur - 1)
        free = valid & jnp.logical_not(forced)
        score = jnp.where(free, imp, -jnp.inf)
        for _ in range(min(SEL_TOPN, NB) - 3):
            mx = colmax(score)
            idx = jnp.min(jnp.where(score == mx, blk_f, float(NB)), axis=0, keepdims=True)
            score = jnp.where(blk_f == idx, -jnp.inf, score)
        picked = valid & (forced | (score == -jnp.inf))
        sb_ref[0:nbp, :] = jnp.where(picked, 0.0, NEG)
        if nbp < NB:
            sb_ref[nbp:NB, :] = jnp.full((NB - nbp, TQ), NEG, F32)
        window_branch()

    size_class = lax.div(i * N_SIZE_CLASSES, n_q)
    for k in range(N_SIZE_CLASSES):
        pl.when(size_class == k)(functools.partial(
            compressed_and_select, NC * (k + 1) // N_SIZE_CLASSES, NB * (k + 1) // N_SIZE_CLASSES))

    pad_rows = jnp.zeros((K_LANES - NSA_DH - 2 * BLK_PER_TILE, NSA_HPG * TQ), BF16)

    def scores(j):
        k0 = pl.multiple_of(j * TK, TK)
        sbt = sb_ref[pl.ds(pl.multiple_of(j * BLK_PER_TILE, BLK_PER_TILE), BLK_PER_TILE), :]
        rows = jnp.concatenate([sbt, jnp.zeros_like(sbt)], axis=0)
        rows = jnp.concatenate([rows] * NSA_HPG, axis=1).astype(BF16)
        w = jnp.concatenate([q_hi, rows, pad_rows], axis=0)
        return _dot(ks_ref[pl.ds(k0, TK), :], w)

    def attend(j, s, carry, causal):
        m, acc = carry
        k0 = pl.multiple_of(j * TK, TK)
        if causal:
            keep = k0 + lax.broadcasted_iota(jnp.int32, (TK, TQ), 0) <= t
            s = jnp.concatenate([jnp.where(keep, lanes(s, h), NEG) for h in heads], axis=1)
        m_new = jnp.maximum(m, colmax(s))
        alpha = jnp.exp2(m - m_new)
        p = jnp.exp2(s - m_new).astype(BF16)
        acc = acc * alpha + _dot(vsT_ref[:, pl.ds(k0, TK)], p)
        return m_new, acc

    jd = lax.div(i, TK // TQ)
    n_pairs = lax.div(jd, 2)
    sa_ref[...] = scores(0)

    def pair(pp, carry):
        sbuf_ref[...] = scores(2 * pp + 1)
        carry = attend(2 * pp, sa_ref[...], carry, False)
        sa_ref[...] = scores(2 * pp + 2)
        return attend(2 * pp + 1, sbuf_ref[...], carry, False)

    carry = (jnp.full((1, NSA_HPG * TQ), NEG, F32), jnp.zeros((V_ROWS, NSA_HPG * TQ), F32))
    quad = lambda qq, c: pair(2 * qq + 1, pair(2 * qq, c))
    octo = lambda oo, c: quad(2 * oo + 1, quad(2 * oo, c))
    n_hexs, n_octs, n_quads = lax.div(n_pairs, 8), lax.div(n_pairs, 4), lax.div(n_pairs, 2)
    carry = lax.fori_loop(0, n_hexs, lambda hh, c: octo(2 * hh + 1, octo(2 * hh, c)), carry)
    carry = lax.fori_loop(2 * n_hexs, n_octs, octo, carry)
    carry = lax.fori_loop(2 * n_octs, n_quads, quad, carry)
    carry = lax.fori_loop(2 * n_quads, n_pairs, pair, carry)
    sbuf_ref[...] = scores(2 * n_pairs + 1)

    m_ref[...], acc_ref[...] = attend(2 * n_pairs, sa_ref[...], carry, True)

    @pl.when(jd != 2 * n_pairs)
    def _():
        m_ref[...], acc_ref[...] = attend(
            2 * n_pairs + 1, sbuf_ref[...], (m_ref[...], acc_ref[...]), True)

    acc = acc_ref[...]
    osT = acc[0:NSA_DH, :] * (1.0 / acc[NSA_DH:NSA_DH + 1, :])

    gT = gT_ref[...]
    ocT, owT = oc_ref[...], ow_ref[...]
    outs = []
    for h in heads:
        g_c, g_s, g_w = (gT[3 * h + b:3 * h + b + 1, :] for b in range(3))
        outs.append(g_c * lanes(ocT, h)[0:NSA_DH, :] + g_s * lanes(osT, h) + g_w * lanes(owT, h))
    o_ref[...] = jnp.concatenate(outs, axis=0).T.astype(BF16)


def _nsa(qT, gT, kc4, vcT, ks, vsT, kw, vwT):
    G, S = ks.shape[0], ks.shape[1]
    NC, NB = kc4.shape[1], S // SEL_BLOCK
    per_g = lambda a: pl.BlockSpec((None,) + a.shape[1:], lambda g, i: (g,) + (0,) * (a.ndim - 1))
    wide = NSA_HPG * TQ
    scratch = [(NB, TQ),
               (TK, wide), (TK, wide),
               (NSA_HPG, SC_PAD + NC, TQ),
               (V_ROWS, wide), (NSA_DH, wide),
               (1, wide), (V_ROWS, wide)]
    return pl.pallas_call(
        _nsa_kernel,
        grid=(G, S // TQ),
        in_specs=[
            pl.BlockSpec((NSA_HPG * NSA_DH, TQ), lambda g, i: (g, i)),
            pl.BlockSpec((None, GATE_ROWS, TQ), lambda g, i: (g, 0, i)),
            per_g(kc4), per_g(vcT), per_g(ks), per_g(vsT), per_g(kw), per_g(vwT),
        ],
        out_specs=pl.BlockSpec((TQ, NSA_HPG * NSA_DH), lambda g, i: (i, g)),
        out_shape=jax.ShapeDtypeStruct((S, NSA_HEADS * NSA_DH), BF16),
        scratch_shapes=[pltpu.VMEM(shape, F32) for shape in scratch],
        compiler_params=_params(
            ("arbitrary", "arbitrary"),
            [(a.shape[1:], a.dtype) for a in (kc4, vcT, ks, vsT, kw, vwT)]
            + [((NSA_HPG * NSA_DH, TQ), F32), ((TQ, NSA_HPG * NSA_DH), BF16)],
            [(shape, F32) for shape in scratch]),
        name="nsa",
    )(qT, gT, kc4, vcT, ks, vsT, kw, vwT)


def _ret_kernel(q_ref, k_ref, v_ref, o_ref, r_ref, dm_ref, qd_ref, kd_ref):
    C = RET_CHUNK
    log_g = [math.log(1.0 - 2.0 ** (-5.0 - h)) for h in range(RET_HEADS)]

    @pl.when(pl.program_id(0) == 0)
    def _():
        r_ref[...] = jnp.zeros(r_ref.shape, F32)
        diff = (lax.broadcasted_iota(jnp.int32, (C, C), 0)
                - lax.broadcasted_iota(jnp.int32, (C, C), 1)).astype(F32)
        n = lax.broadcasted_iota(jnp.int32, (C, RET_DK), 0).astype(F32)
        for h in range(RET_HEADS):
            dm_ref[h] = jnp.where(diff >= 0.0, jnp.exp(jnp.maximum(diff, 0.0) * log_g[h]), 0.0)
            qd_ref[h] = jnp.exp((n + 1.0) * log_g[h])
            kd_ref[h] = jnp.exp((C - 1.0 - n) * log_g[h])

    for sub, h in [(sub, h) for sub in range(RET_CHUNKS_PER_STEP) for h in range(RET_HEADS)]:
        rows = slice(sub * C, (sub + 1) * C)
        q = q_ref[rows, h * RET_DK:(h + 1) * RET_DK]
        k = k_ref[rows, h * RET_DK:(h + 1) * RET_DK]
        v = v_ref[rows, h * RET_DV:(h + 1) * RET_DV]
        att = lax.dot_general(q, k, (((1,), (1,)), ((), ())), preferred_element_type=F32)
        o = _dot((att * dm_ref[h]).astype(BF16), v)
        r = r_ref[h]
        qd = (q.astype(F32) * qd_ref[h]).astype(BF16)
        o = o + _dot(qd, r.astype(BF16))
        kd = (k.astype(F32) * kd_ref[h]).astype(BF16)
        r_ref[h] = math.exp(C * log_g[h]) * r + lax.dot_general(
            kd, v, (((0,), (0,)), ((), ())), preferred_element_type=F32)
        mu = jnp.mean(o, axis=-1, keepdims=True)
        d = o - mu
        var = jnp.mean(d * d, axis=-1, keepdims=True)
        o_ref[rows, h * RET_DV:(h + 1) * RET_DV] = (d * lax.rsqrt(var + GN_EPS)).astype(BF16)


def _retention(qr, kr, vr):
    S = qr.shape[0]
    C = RET_CHUNK
    step_rows = C * RET_CHUNKS_PER_STEP
    row = lambda w: pl.BlockSpec((step_rows, w), lambda n: (n, 0))
    scratch = [(RET_HEADS, RET_DK, RET_DV),
               (RET_HEADS, C, C),
               (RET_HEADS, C, RET_DK), (RET_HEADS, C, RET_DK)]
    return pl.pallas_call(
        _ret_kernel,
        grid=(S // step_rows,),
        in_specs=[row(RET_HEADS * RET_DK), row(RET_HEADS * RET_DK), row(RET_HEADS * RET_DV)],
        out_specs=row(RET_HEADS * RET_DV),
        out_shape=jax.ShapeDtypeStruct((S, RET_HEADS * RET_DV), BF16),
        scratch_shapes=[pltpu.VMEM(shape, F32) for shape in scratch],
        compiler_params=_params(
            ("arbitrary",),
            [((step_rows, w), BF16) for w in (RET_HEADS * RET_DK, RET_HEADS * RET_DK, 2 * RET_HEADS * RET_DV)],
            [(shape, F32) for shape in scratch]),
        name="retention",
    )(qr, kr, vr)


def _post_kernel(x_ref, oa_ref, zn_ref, ob_ref, zr_ref, ma_ref, mb_ref,
                 wa_ref, wb_ref, wo_ref, g_ref, out_ref):
    f32 = lambda r: r[...].astype(F32)
    ya = _dot((f32(oa_ref) * f32(zn_ref)).astype(BF16), wa_ref[...])
    yb = _dot((f32(ob_ref) * f32(zr_ref)).astype(BF16), wb_ref[...])
    merged = f32(ma_ref) * ya + f32(mb_ref) * yb
    y = _dot(merged.astype(BF16), wo_ref[...])
    ms = jnp.mean(y * y, axis=-1, keepdims=True)
    out_ref[...] = x_ref[...] + y * lax.rsqrt(ms + RMS_EPS) * g_ref[...]


def _post(x2, oa, zn, ob, zr, ma, mb, wa, wb, wo, g_post):
    S = x2.shape[0]
    tm = POST_ROW_TILE
    row = lambda a: pl.BlockSpec((tm, a.shape[1]), lambda i: (i, 0))
    whole = lambda a: pl.BlockSpec(a.shape, lambda i: (0, 0))
    rows = (x2, oa, zn, ob, zr, ma, mb)
    consts = (wa, wb, wo, g_post)
    return pl.pallas_call(
        _post_kernel,
        grid=(S // tm,),
        in_specs=[row(a) for a in rows] + [whole(a) for a in consts],
        out_specs=row(x2),
        out_shape=jax.ShapeDtypeStruct(x2.shape, x2.dtype),
        compiler_params=_params(
            ("arbitrary",),
            [((tm, a.shape[1]), a.dtype) for a in rows + (x2,)] + [(a.shape, a.dtype) for a in consts]),
        name="post",
    )(*rows, *consts)


def _rotary_frequencies():
    nsa_inv = 1.0 / (ROPE_THETA ** (jnp.arange(0, ROPE_DIM, 2, dtype=F32) / ROPE_DIM))
    ret_inv = 1.0 / (RET_ROPE_BASE ** jnp.linspace(0.0, 1.0, RET_DK // 2, dtype=F32))
    row = jnp.concatenate([ret_inv, nsa_inv, jnp.zeros((LANE - RET_DK // 2 - ROPE_DIM // 2,), F32)])
    return jnp.pad(row[None, :], ((0, 7), (0, 0)))


def kernel(x, norm_pre, w_in, b_nsa_gate, cmp_pe_k, cmp_w1_k, cmp_w2_k, cmp_pe_v, cmp_w1_v,
           cmp_w2_v, w_nsa_o, w_ret_o, w_out, norm_post):
    B, S, _ = x.shape
    assert B == 1 and S % (2 * TK) == 0 and S >= WIN_KEYS
    assert S % (SEL_BLOCK * 8 * N_SIZE_CLASSES) == 0 and (S // TQ) % N_SIZE_CLASSES == 0
    x2 = x.reshape(S, D_MODEL)

    gate_cols = lambda a: jnp.pad(
        a.reshape(a.shape[:-1] + (NSA_GROUPS, NSA_HPG * 3)),
        [(0, 0)] * (a.ndim - 1) + [(0, 0), (0, GATE_ROWS - NSA_HPG * 3)]).reshape(a.shape[:-1] + (-1,))
    g0 = sum(PROJ_SIZES[:7])
    g1 = g0 + NSA_HEADS * 3
    gate_seg = gate_cols(w_in[:, g0:g1])
    gate_seg = jnp.pad(gate_seg, ((0, 0), (0, LANE - gate_seg.shape[1])))
    w_hi = jnp.concatenate([w_in[:, :g0], gate_seg, w_in[:, g1:]], axis=1).astype(BF16)
    bias = gate_cols(b_nsa_gate)
    bias = jnp.pad(bias, (0, LANE - bias.shape[0])).reshape(1, LANE)

    (qT, kc, vc, ks, vsT, kw, vwT, gT, zn, qr, kr, vr, zr, ma, mb) = _inproj(
        x2, norm_pre.reshape(1, D_MODEL), w_hi, bias, _rotary_frequencies())

    kc4 = _compress(kc, cmp_pe_k, cmp_w1_k, cmp_w2_k, keys=True)
    vcT = _compress(vc, cmp_pe_v, cmp_w1_v, cmp_w2_v, keys=False)

    oa = _nsa(qT, gT, kc4, vcT, ks, vsT, kw, vwT)
    ob = _retention(qr, kr, vr)

    out = _post(x2, oa, zn, ob, zr, ma, mb, w_nsa_o.astype(BF16), w_ret_o.astype(BF16),
                w_out.astype(BF16), norm_post.reshape(1, D_MODEL))
    return out.reshape(B, S, D_MODEL)
```

```python
import functools
import math

import jax
import jax.numpy as jnp
from jax import lax
from jax.experimental import pallas as pl
from jax.experimental.pallas import tpu as pltpu

F32 = jnp.float32
BF16 = jnp.bfloat16

D_MODEL = 1024
NSA_HEADS = 8
NSA_GROUPS = 2
NSA_HPG = NSA_HEADS // NSA_GROUPS
NSA_DH = 64
CMP_LEN = 32
CMP_STRIDE = 16
CMP_HIDDEN = 256
SEL_BLOCK = 64
SEL_TOPN = 16
WINDOW = 512
ROPE_THETA = 500000.0
ROPE_DIM = NSA_DH // 4
NSA_SCALE = NSA_DH ** -0.5
RET_HEADS = 4
RET_DK = 128
RET_DV = 256
RET_ROPE_BASE = 10000.0
RET_SCALE = RET_DK ** -0.5
RMS_EPS = 1e-6
GN_EPS = 1e-6
PROJ_SIZES = (512, 128, 128, 128, 128, 128, 128, 24, 512, 512, 512, 1024, 1024, 1024, 1024)

LANE = 128
V7X_VMEM_LIMIT_BYTES = 56 * 1024 * 1024
VMEM_TEMP_FLOOR_BYTES = 8 * 1024 * 1024

ROW_TILE = 256
POST_ROW_TILE = 512
TQ = 128
Q_TILES_PER_STEP = 2
TK = 512
BLK_PER_TILE = TK // SEL_BLOCK
WIN_KEYS = WINDOW + TQ
RET_CHUNK = 256
RET_CHUNKS_PER_STEP = 4
V_ROWS = 80
NEG = -1e30
LOG2E = math.log2(math.e)
K_LANES = LANE
GATE_ROWS = 16
SC_PAD = 8
N_SIZE_CLASSES = 8

_PAD_SIZES = tuple(128 if s == 24 else s for s in PROJ_SIZES)
_OFF = [0]
for _s in _PAD_SIZES:
    _OFF.append(_OFF[-1] + _s)


def _dot(a, b):
    return jnp.dot(a, b, preferred_element_type=F32)


def _split(a):
    hi = a.astype(BF16)
    lo = (a - hi.astype(F32)).astype(BF16)
    return hi, lo


def _dot3(a_hi, a_lo, b_hi, b_lo):
    return _dot(a_hi, b_hi) + _dot(a_lo, b_hi) + _dot(a_hi, b_lo)


def _nbytes(shape, dtype):
    return math.prod(d for d in shape if d is not None) * jnp.dtype(dtype).itemsize


def _params(sem, blocks, scratch=()):
    est = 2 * sum(_nbytes(*b) for b in blocks) + sum(_nbytes(*b) for b in scratch)
    limit = min(V7X_VMEM_LIMIT_BYTES, est + est // 4 + VMEM_TEMP_FLOOR_BYTES)
    return pltpu.CompilerParams(dimension_semantics=sem, vmem_limit_bytes=limit)


def _inproj_kernel(x_ref, g_ref, whi_ref, bias_ref, freq_ref,
                   qT_ref, kc_ref, vc_ref, ks_ref, vsT_ref, kw_ref, vwT_ref, gT_ref,
                   zn_ref, qr_ref, kr_ref, vr_ref, zr_ref, ma_ref, mb_ref):
    tm = x_ref.shape[0]
    x = x_ref[...]
    ms = jnp.mean(x * x, axis=-1, keepdims=True)
    h = x * lax.rsqrt(ms + RMS_EPS) * g_ref[...]
    h_hi = h.astype(BF16)

    def mm(lo, hi):
        return _dot(h_hi, whi_ref[:, lo:hi])

    lane = lax.broadcasted_iota(jnp.int32, (tm, LANE), 1)
    row = pl.program_id(0) * tm + lax.broadcasted_iota(jnp.int32, (tm, LANE), 0)
    first_group = lane < NSA_DH
    pos = row.astype(F32)

    half = ROPE_DIM // 2
    dim = jnp.bitwise_and(lane, NSA_DH - 1)
    ang = pos * freq_ref[0:1, :]
    c_all, s_all = jnp.cos(ang), jnp.sin(ang)
    c_sw, s_sw = pltpu.roll(c_all, LANE // 2, 1), pltpu.roll(s_all, LANE // 2, 1)
    cr = jnp.where(first_group, c_all, c_sw)
    sr = jnp.where(first_group, -s_all, s_sw)
    lo_half, hi_half = dim < half, (dim >= half) & (dim < ROPE_DIM)
    c_lo, s_lo = jnp.where(first_group, c_sw, c_all), jnp.where(first_group, s_sw, s_all)
    c_hi, s_hi = pltpu.roll(c_lo, half, 1), pltpu.roll(s_lo, half, 1)
    cn = jnp.where(lo_half, c_lo, jnp.where(hi_half, c_hi, 1.0))
    san = jnp.where(hi_half, s_hi, 0.0)
    sbn = jnp.where(lo_half, -s_lo, 0.0)

    def rot_nsa(p):
        return p * cn + pltpu.roll(p, half, 1) * san + pltpu.roll(p, LANE - half, 1) * sbn

    def rot_ret(p):
        return p * cr + pltpu.roll(p, LANE // 2, 1) * sr

    def key_rows(p, fill):
        swapped = pltpu.roll(p, NSA_DH, 1)
        return [jnp.where(first_group, pg, fill).astype(BF16) for pg in (p, swapped)]

    ones_row = jnp.where(lax.broadcasted_iota(jnp.int32, (V_ROWS - NSA_DH, tm), 0) == 0, 1.0, 0.0)

    def value_rows(p):
        pT = p.T
        return [jnp.concatenate([pT[g * NSA_DH:(g + 1) * NSA_DH, :], ones_row], axis=0).astype(BF16)
                for g in range(NSA_GROUPS)]

    p = mm(0, _OFF[2])
    for c in range(4):
        qT_ref[c * LANE:(c + 1) * LANE, :] = (rot_nsa(p[:, c * LANE:(c + 1) * LANE]) * NSA_SCALE).T
    kc_ref[...] = rot_nsa(p[:, 4 * LANE:5 * LANE])

    p = mm(_OFF[2], _OFF[8])
    vc_ref[...] = p[:, 0 * LANE:1 * LANE]
    blk_in_tile = jnp.right_shift(jnp.bitwise_and(row, TK - 1), SEL_BLOCK.bit_length() - 1)
    onehot = jnp.where(lane - NSA_DH == blk_in_tile, 1.0, 0.0)
    ks = key_rows(rot_nsa(p[:, 1 * LANE:2 * LANE]), onehot)
    vs = value_rows(p[:, 2 * LANE:3 * LANE])
    kw = key_rows(rot_nsa(p[:, 3 * LANE:4 * LANE]), 0.0)
    vw = value_rows(p[:, 4 * LANE:5 * LANE])
    gT = jax.nn.sigmoid(p[:, 5 * LANE:6 * LANE] + bias_ref[...]).T
    for g in range(NSA_GROUPS):
        ks_ref[g], vsT_ref[g], kw_ref[g], vwT_ref[g] = ks[g], vs[g], kw[g], vw[g]
        gT_ref[g] = gT[g * GATE_ROWS:(g + 1) * GATE_ROWS, :]

    z = mm(_OFF[8], _OFF[9])
    zn_ref[...] = (z * jax.nn.sigmoid(z)).astype(BF16)

    p = mm(_OFF[9], _OFF[11])
    for c in range(4):
        qr_ref[:, c * LANE:(c + 1) * LANE] = rot_ret(p[:, c * LANE:(c + 1) * LANE]).astype(BF16)
    for c in range(4):
        kr_ref[:, c * LANE:(c + 1) * LANE] = (
            rot_ret(p[:, (4 + c) * LANE:(5 + c) * LANE]) * RET_SCALE).astype(BF16)

    vr_ref[...] = mm(_OFF[11], _OFF[12]).astype(BF16)
    z = mm(_OFF[12], _OFF[13])
    zr_ref[...] = (z * jax.nn.sigmoid(z)).astype(BF16)
    ma_ref[...] = jax.nn.sigmoid(mm(_OFF[13], _OFF[14])).astype(BF16)
    mb_ref[...] = jax.nn.sigmoid(mm(_OFF[14], _OFF[15])).astype(BF16)


def _inproj(x2, norm_pre, w_hi, bias, freqs):
    S = x2.shape[0]
    tm = ROW_TILE
    G = NSA_GROUPS
    row = lambda w: pl.BlockSpec((tm, w), lambda i: (i, 0))
    whole = lambda a: pl.BlockSpec(a.shape, lambda i: (0,) * a.ndim)
    col = lambda r: pl.BlockSpec((r, tm), lambda i: (0, i))
    g_row = lambda w: pl.BlockSpec((G, tm, w), lambda i: (0, i, 0))
    g_col = lambda r: pl.BlockSpec((G, r, tm), lambda i: (0, 0, i))
    sds = jax.ShapeDtypeStruct
    outs = [
        (col(NSA_HEADS * NSA_DH), sds((NSA_HEADS * NSA_DH, S), F32)),
        (row(LANE), sds((S, LANE), F32)),
        (row(LANE), sds((S, LANE), F32)),
        (g_row(K_LANES), sds((G, S, K_LANES), BF16)),
        (g_col(V_ROWS), sds((G, V_ROWS, S), BF16)),
        (g_row(K_LANES), sds((G, S, K_LANES), BF16)),
        (g_col(V_ROWS), sds((G, V_ROWS, S), BF16)),
        (g_col(GATE_ROWS), sds((G, GATE_ROWS, S), F32)),
        (row(512), sds((S, 512), BF16)),
        (row(512), sds((S, 512), BF16)), (row(512), sds((S, 512), BF16)),
        (row(1024), sds((S, 1024), BF16)), (row(1024), sds((S, 1024), BF16)),
        (row(1024), sds((S, 1024), BF16)), (row(1024), sds((S, 1024), BF16)),
    ]
    return pl.pallas_call(
        _inproj_kernel,
        grid=(S // tm,),
        in_specs=[row(D_MODEL), whole(norm_pre), whole(w_hi), whole(bias), whole(freqs)],
        out_specs=[o[0] for o in outs],
        out_shape=[o[1] for o in outs],
        compiler_params=_params(
            ("arbitrary",),
            [((tm, D_MODEL), F32), (w_hi.shape, BF16)] + [(o[0].block_shape, o[1].dtype) for o in outs]),
        name="inproj",
    )(x2, norm_pre, w_hi, bias, freqs)


def _compress_kernel(x_ref, pe_ref, w1_ref, w2_ref, o_ref, *, keys):
    NC = x_ref.shape[0] // CMP_STRIDE
    g = pl.program_id(0)
    first_half = lax.broadcasted_iota(jnp.int32, (NC, LANE), 1) < NSA_DH
    tokens = []
    for u in range(CMP_STRIDE // 2):
        a = x_ref[pl.ds(2 * u, NC, stride=CMP_STRIDE), :]
        b = x_ref[pl.ds(2 * u + 1, NC, stride=CMP_STRIDE), :]
        ar, br = pltpu.roll(a, NSA_DH, 1), pltpu.roll(b, NSA_DH, 1)
        tokens.append(jnp.where(first_half, jnp.where(g == 0, a, ar), jnp.where(g == 0, br, b)))
    xg = jnp.concatenate(tokens, axis=1)
    prod = lambda a, b: _dot3(*_split(a), *_split(b))
    half = xg.shape[1]
    first = prod(xg, w1_ref[0:half, :])
    second = prod(xg, w1_ref[half:2 * half, :])
    nxt = pltpu.roll(second, NC - 1, 0)
    pe_term = prod(pe_ref[...], w1_ref[...])[0:1, :]
    hid = first + nxt + pe_term
    act = hid * (0.5 * (1.0 + jnp.tanh(math.sqrt(2.0 / math.pi) * (hid + 0.044715 * (hid * hid * hid)))))
    out = prod(act, w2_ref[...])
    if keys:
        hi, lo = _split(out)
        o_ref[...] = jnp.concatenate([hi, hi, lo, lo], axis=1)
    else:
        ones_row = jnp.where(lax.broadcasted_iota(jnp.int32, (V_ROWS - NSA_DH, NC), 0) == 0, 1.0, 0.0)
        o_ref[...] = jnp.concatenate([out.T, ones_row], axis=0).astype(BF16)


def _compress(raw, pe, w1, w2, keys):
    S = raw.shape[0]
    NC = S // CMP_STRIDE
    G = NSA_GROUPS
    half = CMP_STRIDE * NSA_DH
    pe8 = jnp.pad(pe.reshape(1, 2 * half), ((0, 7), (0, 0)))
    if keys:
        out_spec = pl.BlockSpec((None, NC, 4 * NSA_DH), lambda g: (g, 0, 0))
        out_shape = jax.ShapeDtypeStruct((G, NC, 4 * NSA_DH), BF16)
    else:
        out_spec = pl.BlockSpec((None, V_ROWS, NC), lambda g: (g, 0, 0))
        out_shape = jax.ShapeDtypeStruct((G, V_ROWS, NC), BF16)
    whole = lambda a: pl.BlockSpec(a.shape, lambda g: (0,) * a.ndim)
    return pl.pallas_call(
        functools.partial(_compress_kernel, keys=keys),
        grid=(G,),
        in_specs=[whole(raw), whole(pe8), whole(w1), whole(w2)],
        out_specs=out_spec,
        out_shape=out_shape,
        compiler_params=_params(
            ("arbitrary",),
            [(raw.shape, F32), (w1.shape, F32), (w2.shape, F32), (out_shape.shape[1:], BF16)],
            [((NC, 2 * CMP_HIDDEN), F32), ((NC, half), F32)]),
        name="compress_k" if keys else "compress_v",
    )(raw, pe8, w1, w2)


def _nsa_tile(sub, qT_ref, gT_ref, kc4_ref, vcT_ref, ks_ref, vsT_ref, kw_ref, vwT_ref, o_ref,
              sb_ref, sa_ref, sbuf_ref, sc_ref, oc_ref, ow_ref, m_ref, acc_ref):
    i = pl.program_id(1) * Q_TILES_PER_STEP + sub
    col0 = pl.multiple_of(sub * TQ, TQ)
    NC = kc4_ref.shape[0]
    NB = sb_ref.shape[0]
    n_q = pl.num_programs(1) * Q_TILES_PER_STEP
    heads = range(NSA_HPG)
    lanes = lambda a, h: a[:, h * TQ:(h + 1) * TQ]
    colmax = lambda a: jnp.max(a, axis=0, keepdims=True)
    colsum = lambda a: jnp.sum(a, axis=0, keepdims=True)

    qT = qT_ref[:, pl.ds(col0, TQ)] * LOG2E
    Q = jnp.concatenate([qT[h * NSA_DH:(h + 1) * NSA_DH, :] for h in heads], axis=1)
    q_hi, q_lo = _split(Q)
    q4 = jnp.concatenate([q_hi, q_lo, q_hi, q_lo], axis=0)
    t = i * TQ + lax.broadcasted_iota(jnp.int32, (1, TQ), 1)
    cur = jnp.right_shift(t, SEL_BLOCK.bit_length() - 1)

    def window_branch():
        w0 = pl.multiple_of(jnp.maximum(i * TQ - WINDOW, 0), TQ)
        q_pad = jnp.concatenate([q_hi, jnp.zeros((K_LANES - NSA_DH, NSA_HPG * TQ), BF16)], axis=0)
        s = _dot(kw_ref[pl.ds(w0, WIN_KEYS), :], q_pad)
        key = w0 + lax.broadcasted_iota(jnp.int32, (WIN_KEYS, TQ), 0)
        mask_w = (key <= t) & (key > t - WINDOW)
        p_heads = []
        for h in heads:
            sm = jnp.where(mask_w, lanes(s, h), NEG)
            p_heads.append(jnp.exp2(sm - colmax(sm)).astype(BF16))
        accw = _dot(vwT_ref[:, pl.ds(w0, WIN_KEYS)], jnp.concatenate(p_heads, axis=1))
        owT = accw[0:NSA_DH, :] * (1.0 / accw[NSA_DH:NSA_DH + 1, :])
        ow_ref[...] = owT

    def compressed_and_select(ncp, nbp):
        s = _dot(kc4_ref[0:ncp, :], q4)
        c_end = lax.broadcasted_iota(jnp.int32, (ncp, TQ), 0) * CMP_STRIDE + (CMP_LEN - 1)
        mask_c = c_end <= t
        p_heads, inv = [], []
        for h in heads:
            sm = jnp.where(mask_c, lanes(s, h), -jnp.inf)
            m = colmax(sm)
            m = jnp.where(m == -jnp.inf, 0.0, m)
            p = jnp.exp2(sm - m)
            inv.append(1.0 / jnp.maximum(colsum(p), jnp.finfo(F32).tiny))
            sc_ref[h, 0:SC_PAD, :] = jnp.zeros((SC_PAD, TQ), F32)
            sc_ref[h, SC_PAD:SC_PAD + ncp, :] = p
            p_heads.append(p.astype(BF16))
        oc_ref[...] = (_dot(vcT_ref[:, 0:ncp], jnp.concatenate(p_heads, axis=1))
                       * jnp.concatenate(inv, axis=1))
        per_sel = SEL_BLOCK // CMP_STRIDE
        imp = jnp.zeros((nbp, TQ), F32)
        for h in heads:
            tot = sc_ref[h, pl.ds(SC_PAD - 1, nbp, stride=per_sel), :]
            for r in range(per_sel):
                tot = tot + sc_ref[h, pl.ds(SC_PAD + r, nbp, stride=per_sel), :]
            imp = imp + tot * inv[h]
        blk = lax.broadcasted_iota(jnp.int32, (nbp, TQ), 0)
        blk_f = blk.astype(F32)
        valid = blk <= cur
        forced = (blk == 0) | (blk == cur) | (blk == cur - 1)
        free = valid & jnp.logical_not(forced)
        score = jnp.where(free, imp, -jnp.inf)
        for _ in range(min(SEL_TOPN, NB) - 3):
            mx = colmax(score)
            idx = jnp.min(jnp.where(score == mx, blk_f, float(NB)), axis=0, keepdims=True)
            score = jnp.where(blk_f == idx, -jnp.inf, score)
        picked = valid & (forced | (score == -jnp.inf))
        sb_ref[0:nbp, :] = jnp.where(picked, 0.0, NEG)
        if nbp < NB:
            sb_ref[nbp:NB, :] = jnp.full((NB - nbp, TQ), NEG, F32)
        window_branch()

    size_class = lax.div(i * N_SIZE_CLASSES, n_q)
    for k in range(N_SIZE_CLASSES):
        pl.when(size_class == k)(functools.partial(
            compressed_and_select, NC * (k + 1) // N_SIZE_CLASSES, NB * (k + 1) // N_SIZE_CLASSES))

    pad_rows = jnp.zeros((K_LANES - NSA_DH - 2 * BLK_PER_TILE, NSA_HPG * TQ), BF16)

    def scores(j):
        k0 = pl.multiple_of(j * TK, TK)
        sbt = sb_ref[pl.ds(pl.multiple_of(j * BLK_PER_TILE, BLK_PER_TILE), BLK_PER_TILE), :]
        rows = jnp.concatenate([sbt, jnp.zeros_like(sbt)], axis=0)
        rows = jnp.concatenate([rows] * NSA_HPG, axis=1).astype(BF16)
        w = jnp.concatenate([q_hi, rows, pad_rows], axis=0)
        return _dot(ks_ref[pl.ds(k0, TK), :], w)

    def attend(j, s, carry, causal):
        m, acc = carry
        k0 = pl.multiple_of(j * TK, TK)
        if causal:
            keep = k0 + lax.broadcasted_iota(jnp.int32, (TK, TQ), 0) <= t
            s = jnp.concatenate([jnp.where(keep, lanes(s, h), NEG) for h in heads], axis=1)
        m_new = jnp.maximum(m, colmax(s))
        alpha = jnp.exp2(m - m_new)
        p = jnp.exp2(s - m_new).astype(BF16)
        acc = acc * alpha + _dot(vsT_ref[:, pl.ds(k0, TK)], p)
        return m_new, acc

    jd = lax.div(i, TK // TQ)
    n_pairs = lax.div(jd, 2)
    sa_ref[...] = scores(0)

    def pair(pp, carry):
        sbuf_ref[...] = scores(2 * pp + 1)
        carry = attend(2 * pp, sa_ref[...], carry, False)
        sa_ref[...] = scores(2 * pp + 2)
        return attend(2 * pp + 1, sbuf_ref[...], carry, False)

    carry = (jnp.full((1, NSA_HPG * TQ), NEG, F32), jnp.zeros((V_ROWS, NSA_HPG * TQ), F32))
    quad = lambda qq, c: pair(2 * qq + 1, pair(2 * qq, c))
    octo = lambda oo, c: quad(2 * oo + 1, quad(2 * oo, c))
    n_hexs, n_octs, n_quads = lax.div(n_pairs, 8), lax.div(n_pairs, 4), lax.div(n_pairs, 2)
    carry = lax.fori_loop(0, n_hexs, lambda hh, c: octo(2 * hh + 1, octo(2 * hh, c)), carry)
    carry = lax.fori_loop(2 * n_hexs, n_octs, octo, carry)
    carry = lax.fori_loop(2 * n_octs, n_quads, quad, carry)
    carry = lax.fori_loop(2 * n_quads, n_pairs, pair, carry)
    sbuf_ref[...] = scores(2 * n_pairs + 1)

    m_ref[...], acc_ref[...] = attend(2 * n_pairs, sa_ref[...], carry, True)

    @pl.when(jd != 2 * n_pairs)
    def _():
        m_ref[...], acc_ref[...] = attend(
            2 * n_pairs + 1, sbuf_ref[...], (m_ref[...], acc_ref[...]), True)

    acc = acc_ref[...]
    osT = acc[0:NSA_DH, :] * (1.0 / acc[NSA_DH:NSA_DH + 1, :])

    gT = gT_ref[:, pl.ds(col0, TQ)]
    ocT, owT = oc_ref[...], ow_ref[...]
    outs = []
    for h in heads:
        g_c, g_s, g_w = (gT[3 * h + b:3 * h + b + 1, :] for b in range(3))
        outs.append(g_c * lanes(ocT, h)[0:NSA_DH, :] + g_s * lanes(osT, h) + g_w * lanes(owT, h))
    o_ref[pl.ds(col0, TQ), :] = jnp.concatenate(outs, axis=0).T.astype(BF16)


def _nsa_kernel(*refs):
    def query_tile(sub, carry):
        _nsa_tile(sub, *refs)
        return carry
    lax.fori_loop(0, Q_TILES_PER_STEP, query_tile, 0)


def _nsa(qT, gT, kc4, vcT, ks, vsT, kw, vwT):
    G, S = ks.shape[0], ks.shape[1]
    NC, NB = kc4.shape[1], S // SEL_BLOCK
    per_g = lambda a: pl.BlockSpec((None,) + a.shape[1:], lambda g, i: (g,) + (0,) * (a.ndim - 1))
    wide = NSA_HPG * TQ
    scratch = [(NB, TQ),
               (TK, wide), (TK, wide),
               (NSA_HPG, SC_PAD + NC, TQ),
               (V_ROWS, wide), (NSA_DH, wide),
               (1, wide), (V_ROWS, wide)]
    return pl.pallas_call(
        _nsa_kernel,
        grid=(G, S // (TQ * Q_TILES_PER_STEP)),
        in_specs=[
            pl.BlockSpec((NSA_HPG * NSA_DH, TQ * Q_TILES_PER_STEP), lambda g, i: (g, i)),
            pl.BlockSpec((None, GATE_ROWS, TQ * Q_TILES_PER_STEP), lambda g, i: (g, 0, i)),
            per_g(kc4), per_g(vcT), per_g(ks), per_g(vsT), per_g(kw), per_g(vwT),
        ],
        out_specs=pl.BlockSpec((TQ * Q_TILES_PER_STEP, NSA_HPG * NSA_DH), lambda g, i: (i, g)),
        out_shape=jax.ShapeDtypeStruct((S, NSA_HEADS * NSA_DH), BF16),
        scratch_shapes=[pltpu.VMEM(shape, F32) for shape in scratch],
        compiler_params=_params(
            ("arbitrary", "arbitrary"),
            [(a.shape[1:], a.dtype) for a in (kc4, vcT, ks, vsT, kw, vwT)]
            + [((NSA_HPG * NSA_DH, TQ * Q_TILES_PER_STEP), F32),
               ((TQ * Q_TILES_PER_STEP, NSA_HPG * NSA_DH), BF16)],
            [(shape, F32) for shape in scratch]),
        name="nsa",
    )(qT, gT, kc4, vcT, ks, vsT, kw, vwT)


def _ret_kernel(q_ref, k_ref, v_ref, o_ref, r_ref, dm_ref, qd_ref, kd_ref):
    C = RET_CHUNK
    log_g = [math.log(1.0 - 2.0 ** (-5.0 - h)) for h in range(RET_HEADS)]

    @pl.when(pl.program_id(0) == 0)
    def _():
        r_ref[...] = jnp.zeros(r_ref.shape, F32)
        diff = (lax.broadcasted_iota(jnp.int32, (C, C), 0)
                - lax.broadcasted_iota(jnp.int32, (C, C), 1)).astype(F32)
        n = lax.broadcasted_iota(jnp.int32, (C, RET_DK), 0).astype(F32)
        for h in range(RET_HEADS):
            dm_ref[h] = jnp.where(diff >= 0.0, jnp.exp(jnp.maximum(diff, 0.0) * log_g[h]), 0.0)
            qd_ref[h] = jnp.exp((n + 1.0) * log_g[h])
            kd_ref[h] = jnp.exp((C - 1.0 - n) * log_g[h])

    for sub, h in [(sub, h) for sub in range(RET_CHUNKS_PER_STEP) for h in range(RET_HEADS)]:
        rows = slice(sub * C, (sub + 1) * C)
        q = q_ref[rows, h * RET_DK:(h + 1) * RET_DK]
        k = k_ref[rows, h * RET_DK:(h + 1) * RET_DK]
        v = v_ref[rows, h * RET_DV:(h + 1) * RET_DV]
        att = lax.dot_general(q, k, (((1,), (1,)), ((), ())), preferred_element_type=F32)
        o = _dot((att * dm_ref[h]).astype(BF16), v)
        r = r_ref[h]
        qd = (q.astype(F32) * qd_ref[h]).astype(BF16)
        o = o + _dot(qd, r.astype(BF16))
        kd = (k.astype(F32) * kd_ref[h]).astype(BF16)
        r_ref[h] = math.exp(C * log_g[h]) * r + lax.dot_general(
            kd, v, (((0,), (0,)), ((), ())), preferred_element_type=F32)
        mu = jnp.mean(o, axis=-1, keepdims=True)
        d = o - mu
        var = jnp.mean(d * d, axis=-1, keepdims=True)
        o_ref[rows, h * RET_DV:(h + 1) * RET_DV] = (d * lax.rsqrt(var + GN_EPS)).astype(BF16)


def _retention(qr, kr, vr):
    S = qr.shape[0]
    C = RET_CHUNK
    step_rows = C * RET_CHUNKS_PER_STEP
    row = lambda w: pl.BlockSpec((step_rows, w), lambda n: (n, 0))
    scratch = [(RET_HEADS, RET_DK, RET_DV),
               (RET_HEADS, C, C),
               (RET_HEADS, C, RET_DK), (RET_HEADS, C, RET_DK)]
    return pl.pallas_call(
        _ret_kernel,
        grid=(S // step_rows,),
        in_specs=[row(RET_HEADS * RET_DK), row(RET_HEADS * RET_DK), row(RET_HEADS * RET_DV)],
        out_specs=row(RET_HEADS * RET_DV),
        out_shape=jax.ShapeDtypeStruct((S, RET_HEADS * RET_DV), BF16),
        scratch_shapes=[pltpu.VMEM(shape, F32) for shape in scratch],
        compiler_params=_params(
            ("arbitrary",),
            [((step_rows, w), BF16) for w in (RET_HEADS * RET_DK, RET_HEADS * RET_DK, 2 * RET_HEADS * RET_DV)],
            [(shape, F32) for shape in scratch]),
        name="retention",
    )(qr, kr, vr)


def _post_kernel(x_ref, oa_ref, zn_ref, ob_ref, zr_ref, ma_ref, mb_ref,
                 wa_ref, wb_ref, wo_ref, g_ref, out_ref):
    f32 = lambda r: r[...].astype(F32)
    ya = _dot((f32(oa_ref) * f32(zn_ref)).astype(BF16), wa_ref[...])
    yb = _dot((f32(ob_ref) * f32(zr_ref)).astype(BF16), wb_ref[...])
    merged = f32(ma_ref) * ya + f32(mb_ref) * yb
    y = _dot(merged.astype(BF16), wo_ref[...])
    ms = jnp.mean(y * y, axis=-1, keepdims=True)
    out_ref[...] = x_ref[...] + y * lax.rsqrt(ms + RMS_EPS) * g_ref[...]


def _post(x2, oa, zn, ob, zr, ma, mb, wa, wb, wo, g_post):
    S = x2.shape[0]
    tm = POST_ROW_TILE
    row = lambda a: pl.BlockSpec((tm, a.shape[1]), lambda i: (i, 0))
    whole = lambda a: pl.BlockSpec(a.shape, lambda i: (0, 0))
    rows = (x2, oa, zn, ob, zr, ma, mb)
    consts = (wa, wb, wo, g_post)
    return pl.pallas_call(
        _post_kernel,
        grid=(S // tm,),
        in_specs=[row(a) for a in rows] + [whole(a) for a in consts],
        out_specs=row(x2),
        out_shape=jax.ShapeDtypeStruct(x2.shape, x2.dtype),
        compiler_params=_params(
            ("arbitrary",),
            [((tm, a.shape[1]), a.dtype) for a in rows + (x2,)] + [(a.shape, a.dtype) for a in consts]),
        name="post",
    )(*rows, *consts)


def _rotary_frequencies():
    nsa_inv = 1.0 / (ROPE_THETA ** (jnp.arange(0, ROPE_DIM, 2, dtype=F32) / ROPE_DIM))
    ret_inv = 1.0 / (RET_ROPE_BASE ** jnp.linspace(0.0, 1.0, RET_DK // 2, dtype=F32))
    row = jnp.concatenate([ret_inv, nsa_inv, jnp.zeros((LANE - RET_DK // 2 - ROPE_DIM // 2,), F32)])
    return jnp.pad(row[None, :], ((0, 7), (0, 0)))


def kernel(x, norm_pre, w_in, b_nsa_gate, cmp_pe_k, cmp_w1_k, cmp_w2_k, cmp_pe_v, cmp_w1_v,
           cmp_w2_v, w_nsa_o, w_ret_o, w_out, norm_post):
    B, S, _ = x.shape
    assert B == 1 and S % (2 * TK) == 0 and S >= WIN_KEYS
    assert S % (SEL_BLOCK * 8 * N_SIZE_CLASSES) == 0 and (S // TQ) % N_SIZE_CLASSES == 0
    x2 = x.reshape(S, D_MODEL)

    gate_cols = lambda a: jnp.pad(
        a.reshape(a.shape[:-1] + (NSA_GROUPS, NSA_HPG * 3)),
        [(0, 0)] * (a.ndim - 1) + [(0, 0), (0, GATE_ROWS - NSA_HPG * 3)]).reshape(a.shape[:-1] + (-1,))
    g0 = sum(PROJ_SIZES[:7])
    g1 = g0 + NSA_HEADS * 3
    gate_seg = gate_cols(w_in[:, g0:g1])
    gate_seg = jnp.pad(gate_seg, ((0, 0), (0, LANE - gate_seg.shape[1])))
    w_hi = jnp.concatenate([w_in[:, :g0], gate_seg, w_in[:, g1:]], axis=1).astype(BF16)
    bias = gate_cols(b_nsa_gate)
    bias = jnp.pad(bias, (0, LANE - bias.shape[0])).reshape(1, LANE)

    (qT, kc, vc, ks, vsT, kw, vwT, gT, zn, qr, kr, vr, zr, ma, mb) = _inproj(
        x2, norm_pre.reshape(1, D_MODEL), w_hi, bias, _rotary_frequencies())

    kc4 = _compress(kc, cmp_pe_k, cmp_w1_k, cmp_w2_k, keys=True)
    vcT = _compress(vc, cmp_pe_v, cmp_w1_v, cmp_w2_v, keys=False)

    oa = _nsa(qT, gT, kc4, vcT, ks, vsT, kw, vwT)
    ob = _retention(qr, kr, vr)

    out = _post(x2, oa, zn, ob, zr, ma, mb, w_nsa_o.astype(BF16), w_ret_o.astype(BF16),
                w_out.astype(BF16), norm_post.reshape(1, D_MODEL))
    return out.reshape(B, S, D_MODEL)
```

```python
import functools
import math

import jax
import jax.numpy as jnp
from jax import lax
from jax.experimental import pallas as pl
from jax.experimental.pallas import tpu as pltpu

F32 = jnp.float32
BF16 = jnp.bfloat16

D_MODEL = 1024
NSA_HEADS = 8
NSA_GROUPS = 2
NSA_HPG = NSA_HEADS // NSA_GROUPS
NSA_DH = 64
CMP_LEN = 32
CMP_STRIDE = 16
CMP_HIDDEN = 256
SEL_BLOCK = 64
SEL_TOPN = 16
WINDOW = 512
ROPE_THETA = 500000.0
ROPE_DIM = NSA_DH // 4
NSA_SCALE = NSA_DH ** -0.5
RET_HEADS = 4
RET_DK = 128
RET_DV = 256
RET_ROPE_BASE = 10000.0
RET_SCALE = RET_DK ** -0.5
RMS_EPS = 1e-6
GN_EPS = 1e-6
PROJ_SIZES = (512, 128, 128, 128, 128, 128, 128, 24, 512, 512, 512, 1024, 1024, 1024, 1024)

LANE = 128
V7X_VMEM_LIMIT_BYTES = 56 * 1024 * 1024
VMEM_TEMP_FLOOR_BYTES = 8 * 1024 * 1024

ROW_TILE = 256
POST_ROW_TILE = 512
TQ = 128
Q_TILES_PER_STEP = 8
TK = 512
BLK_PER_TILE = TK // SEL_BLOCK
WIN_KEYS = WINDOW + TQ
RET_CHUNK = 256
RET_CHUNKS_PER_STEP = 4
V_ROWS = 80
NEG = -1e30
LOG2E = math.log2(math.e)
K_LANES = LANE
GATE_ROWS = 16
SC_PAD = 8
N_SIZE_CLASSES = 8

_PAD_SIZES = tuple(128 if s == 24 else s for s in PROJ_SIZES)
_OFF = [0]
for _s in _PAD_SIZES:
    _OFF.append(_OFF[-1] + _s)


def _dot(a, b):
    return jnp.dot(a, b, preferred_element_type=F32)


def _split(a):
    hi = a.astype(BF16)
    lo = (a - hi.astype(F32)).astype(BF16)
    return hi, lo


def _dot3(a_hi, a_lo, b_hi, b_lo):
    return _dot(a_hi, b_hi) + _dot(a_lo, b_hi) + _dot(a_hi, b_lo)


def _nbytes(shape, dtype):
    return math.prod(d for d in shape if d is not None) * jnp.dtype(dtype).itemsize


def _params(sem, blocks, scratch=()):
    est = 2 * sum(_nbytes(*b) for b in blocks) + sum(_nbytes(*b) for b in scratch)
    limit = min(V7X_VMEM_LIMIT_BYTES, est + est // 4 + VMEM_TEMP_FLOOR_BYTES)
    return pltpu.CompilerParams(dimension_semantics=sem, vmem_limit_bytes=limit)


def _inproj_kernel(x_ref, g_ref, whi_ref, bias_ref, freq_ref,
                   qT_ref, kc_ref, vc_ref, ks_ref, vsT_ref, kw_ref, vwT_ref, gT_ref,
                   zn_ref, qr_ref, kr_ref, vr_ref, zr_ref, ma_ref, mb_ref):
    tm = x_ref.shape[0]
    x = x_ref[...]
    ms = jnp.mean(x * x, axis=-1, keepdims=True)
    h = x * lax.rsqrt(ms + RMS_EPS) * g_ref[...]
    h_hi = h.astype(BF16)

    def mm(lo, hi):
        return _dot(h_hi, whi_ref[:, lo:hi])

    lane = lax.broadcasted_iota(jnp.int32, (tm, LANE), 1)
    row = pl.program_id(0) * tm + lax.broadcasted_iota(jnp.int32, (tm, LANE), 0)
    first_group = lane < NSA_DH
    pos = row.astype(F32)

    half = ROPE_DIM // 2
    dim = jnp.bitwise_and(lane, NSA_DH - 1)
    ang = pos * freq_ref[0:1, :]
    c_all, s_all = jnp.cos(ang), jnp.sin(ang)
    c_sw, s_sw = pltpu.roll(c_all, LANE // 2, 1), pltpu.roll(s_all, LANE // 2, 1)
    cr = jnp.where(first_group, c_all, c_sw)
    sr = jnp.where(first_group, -s_all, s_sw)
    lo_half, hi_half = dim < half, (dim >= half) & (dim < ROPE_DIM)
    c_lo, s_lo = jnp.where(first_group, c_sw, c_all), jnp.where(first_group, s_sw, s_all)
    c_hi, s_hi = pltpu.roll(c_lo, half, 1), pltpu.roll(s_lo, half, 1)
    cn = jnp.where(lo_half, c_lo, jnp.where(hi_half, c_hi, 1.0))
    san = jnp.where(hi_half, s_hi, 0.0)
    sbn = jnp.where(lo_half, -s_lo, 0.0)

    def rot_nsa(p):
        return p * cn + pltpu.roll(p, half, 1) * san + pltpu.roll(p, LANE - half, 1) * sbn

    def rot_ret(p):
        return p * cr + pltpu.roll(p, LANE // 2, 1) * sr

    def key_rows(p, fill):
        swapped = pltpu.roll(p, NSA_DH, 1)
        return [jnp.where(first_group, pg, fill).astype(BF16) for pg in (p, swapped)]

    ones_row = jnp.where(lax.broadcasted_iota(jnp.int32, (V_ROWS - NSA_DH, tm), 0) == 0, 1.0, 0.0)

    def value_rows(p):
        pT = p.T
        return [jnp.concatenate([pT[g * NSA_DH:(g + 1) * NSA_DH, :], ones_row], axis=0).astype(BF16)
                for g in range(NSA_GROUPS)]

    p = mm(0, _OFF[2])
    for c in range(4):
        qT_ref[c * LANE:(c + 1) * LANE, :] = (rot_nsa(p[:, c * LANE:(c + 1) * LANE]) * NSA_SCALE).T
    kc_ref[...] = rot_nsa(p[:, 4 * LANE:5 * LANE])

    p = mm(_OFF[2], _OFF[8])
    vc_ref[...] = p[:, 0 * LANE:1 * LANE]
    blk_in_tile = jnp.right_shift(jnp.bitwise_and(row, TK - 1), SEL_BLOCK.bit_length() - 1)
    onehot = jnp.where(lane - NSA_DH == blk_in_tile, 1.0, 0.0)
    ks = key_rows(rot_nsa(p[:, 1 * LANE:2 * LANE]), onehot)
    vs = value_rows(p[:, 2 * LANE:3 * LANE])
    kw = key_rows(rot_nsa(p[:, 3 * LANE:4 * LANE]), 0.0)
    vw = value_rows(p[:, 4 * LANE:5 * LANE])
    gT = jax.nn.sigmoid(p[:, 5 * LANE:6 * LANE] + bias_ref[...]).T
    for g in range(NSA_GROUPS):
        ks_ref[g], vsT_ref[g], kw_ref[g], vwT_ref[g] = ks[g], vs[g], kw[g], vw[g]
        gT_ref[g] = gT[g * GATE_ROWS:(g + 1) * GATE_ROWS, :]

    z = mm(_OFF[8], _OFF[9])
    zn_ref[...] = (z * jax.nn.sigmoid(z)).astype(BF16)

    p = mm(_OFF[9], _OFF[11])
    for c in range(4):
        qr_ref[:, c * LANE:(c + 1) * LANE] = rot_ret(p[:, c * LANE:(c + 1) * LANE]).astype(BF16)
    for c in range(4):
        kr_ref[:, c * LANE:(c + 1) * LANE] = (
            rot_ret(p[:, (4 + c) * LANE:(5 + c) * LANE]) * RET_SCALE).astype(BF16)

    vr_ref[...] = mm(_OFF[11], _OFF[12]).astype(BF16)
    z = mm(_OFF[12], _OFF[13])
    zr_ref[...] = (z * jax.nn.sigmoid(z)).astype(BF16)
    ma_ref[...] = jax.nn.sigmoid(mm(_OFF[13], _OFF[14])).astype(BF16)
    mb_ref[...] = jax.nn.sigmoid(mm(_OFF[14], _OFF[15])).astype(BF16)


def _inproj(x2, norm_pre, w_hi, bias, freqs):
    S = x2.shape[0]
    tm = ROW_TILE
    G = NSA_GROUPS
    row = lambda w: pl.BlockSpec((tm, w), lambda i: (i, 0))
    whole = lambda a: pl.BlockSpec(a.shape, lambda i: (0,) * a.ndim)
    col = lambda r: pl.BlockSpec((r, tm), lambda i: (0, i))
    g_row = lambda w: pl.BlockSpec((G, tm, w), lambda i: (0, i, 0))
    g_col = lambda r: pl.BlockSpec((G, r, tm), lambda i: (0, 0, i))
    sds = jax.ShapeDtypeStruct
    outs = [
        (col(NSA_HEADS * NSA_DH), sds((NSA_HEADS * NSA_DH, S), F32)),
        (row(LANE), sds((S, LANE), F32)),
        (row(LANE), sds((S, LANE), F32)),
        (g_row(K_LANES), sds((G, S, K_LANES), BF16)),
        (g_col(V_ROWS), sds((G, V_ROWS, S), BF16)),
        (g_row(K_LANES), sds((G, S, K_LANES), BF16)),
        (g_col(V_ROWS), sds((G, V_ROWS, S), BF16)),
        (g_col(GATE_ROWS), sds((G, GATE_ROWS, S), F32)),
        (row(512), sds((S, 512), BF16)),
        (row(512), sds((S, 512), BF16)), (row(512), sds((S, 512), BF16)),
        (row(1024), sds((S, 1024), BF16)), (row(1024), sds((S, 1024), BF16)),
        (row(1024), sds((S, 1024), BF16)), (row(1024), sds((S, 1024), BF16)),
    ]
    return pl.pallas_call(
        _inproj_kernel,
        grid=(S // tm,),
        in_specs=[row(D_MODEL), whole(norm_pre), whole(w_hi), whole(bias), whole(freqs)],
        out_specs=[o[0] for o in outs],
        out_shape=[o[1] for o in outs],
        compiler_params=_params(
            ("arbitrary",),
            [((tm, D_MODEL), F32), (w_hi.shape, BF16)] + [(o[0].block_shape, o[1].dtype) for o in outs]),
        name="inproj",
    )(x2, norm_pre, w_hi, bias, freqs)


def _compress_kernel(x_ref, pe_ref, w1_ref, w2_ref, o_ref, *, keys):
    NC = x_ref.shape[0] // CMP_STRIDE
    g = pl.program_id(0)
    first_half = lax.broadcasted_iota(jnp.int32, (NC, LANE), 1) < NSA_DH
    tokens = []
    for u in range(CMP_STRIDE // 2):
        a = x_ref[pl.ds(2 * u, NC, stride=CMP_STRIDE), :]
        b = x_ref[pl.ds(2 * u + 1, NC, stride=CMP_STRIDE), :]
        ar, br = pltpu.roll(a, NSA_DH, 1), pltpu.roll(b, NSA_DH, 1)
        tokens.append(jnp.where(first_half, jnp.where(g == 0, a, ar), jnp.where(g == 0, br, b)))
    xg = jnp.concatenate(tokens, axis=1)
    prod = lambda a, b: _dot3(*_split(a), *_split(b))
    half = xg.shape[1]
    first = prod(xg, w1_ref[0:half, :])
    second = prod(xg, w1_ref[half:2 * half, :])
    nxt = pltpu.roll(second, NC - 1, 0)
    pe_term = prod(pe_ref[...], w1_ref[...])[0:1, :]
    hid = first + nxt + pe_term
    act = hid * (0.5 * (1.0 + jnp.tanh(math.sqrt(2.0 / math.pi) * (hid + 0.044715 * (hid * hid * hid)))))
    out = prod(act, w2_ref[...])
    if keys:
        hi, lo = _split(out)
        o_ref[...] = jnp.concatenate([hi, hi, lo, lo], axis=1)
    else:
        ones_row = jnp.where(lax.broadcasted_iota(jnp.int32, (V_ROWS - NSA_DH, NC), 0) == 0, 1.0, 0.0)
        o_ref[...] = jnp.concatenate([out.T, ones_row], axis=0).astype(BF16)


def _compress(raw, pe, w1, w2, keys):
    S = raw.shape[0]
    NC = S // CMP_STRIDE
    G = NSA_GROUPS
    half = CMP_STRIDE * NSA_DH
    pe8 = jnp.pad(pe.reshape(1, 2 * half), ((0, 7), (0, 0)))
    if keys:
        out_spec = pl.BlockSpec((None, NC, 4 * NSA_DH), lambda g: (g, 0, 0))
        out_shape = jax.ShapeDtypeStruct((G, NC, 4 * NSA_DH), BF16)
    else:
        out_spec = pl.BlockSpec((None, V_ROWS, NC), lambda g: (g, 0, 0))
        out_shape = jax.ShapeDtypeStruct((G, V_ROWS, NC), BF16)
    whole = lambda a: pl.BlockSpec(a.shape, lambda g: (0,) * a.ndim)
    return pl.pallas_call(
        functools.partial(_compress_kernel, keys=keys),
        grid=(G,),
        in_specs=[whole(raw), whole(pe8), whole(w1), whole(w2)],
        out_specs=out_spec,
        out_shape=out_shape,
        compiler_params=_params(
            ("arbitrary",),
            [(raw.shape, F32), (w1.shape, F32), (w2.shape, F32), (out_shape.shape[1:], BF16)],
            [((NC, 2 * CMP_HIDDEN), F32), ((NC, half), F32)]),
        name="compress_k" if keys else "compress_v",
    )(raw, pe8, w1, w2)


def _nsa_tile(sub, qT_ref, gT_ref, kc4_ref, vcT_ref, ks_ref, vsT_ref, kw_ref, vwT_ref, o_ref,
              sb_ref, sa_ref, sbuf_ref, sc_ref, oc_ref, ow_ref, m_ref, acc_ref):
    i = pl.program_id(1) * Q_TILES_PER_STEP + sub
    col0 = pl.multiple_of(sub * TQ, TQ)
    NC = kc4_ref.shape[0]
    NB = sb_ref.shape[0]
    n_q = pl.num_programs(1) * Q_TILES_PER_STEP
    heads = range(NSA_HPG)
    lanes = lambda a, h: a[:, h * TQ:(h + 1) * TQ]
    colmax = lambda a: jnp.max(a, axis=0, keepdims=True)
    colsum = lambda a: jnp.sum(a, axis=0, keepdims=True)

    qT = qT_ref[:, pl.ds(col0, TQ)] * LOG2E
    Q = jnp.concatenate([qT[h * NSA_DH:(h + 1) * NSA_DH, :] for h in heads], axis=1)
    q_hi, q_lo = _split(Q)
    q4 = jnp.concatenate([q_hi, q_lo, q_hi, q_lo], axis=0)
    t = i * TQ + lax.broadcasted_iota(jnp.int32, (1, TQ), 1)
    cur = jnp.right_shift(t, SEL_BLOCK.bit_length() - 1)

    def window_branch():
        w0 = pl.multiple_of(jnp.maximum(i * TQ - WINDOW, 0), TQ)
        q_pad = jnp.concatenate([q_hi, jnp.zeros((K_LANES - NSA_DH, NSA_HPG * TQ), BF16)], axis=0)
        s = _dot(kw_ref[pl.ds(w0, WIN_KEYS), :], q_pad)
        key = w0 + lax.broadcasted_iota(jnp.int32, (WIN_KEYS, TQ), 0)
        mask_w = (key <= t) & (key > t - WINDOW)
        p_heads = []
        for h in heads:
            sm = jnp.where(mask_w, lanes(s, h), NEG)
            p_heads.append(jnp.exp2(sm - colmax(sm)).astype(BF16))
        accw = _dot(vwT_ref[:, pl.ds(w0, WIN_KEYS)], jnp.concatenate(p_heads, axis=1))
        owT = accw[0:NSA_DH, :] * (1.0 / accw[NSA_DH:NSA_DH + 1, :])
        ow_ref[...] = owT

    def compressed_and_select(ncp, nbp):
        s = _dot(kc4_ref[0:ncp, :], q4)
        c_end = lax.broadcasted_iota(jnp.int32, (ncp, TQ), 0) * CMP_STRIDE + (CMP_LEN - 1)
        mask_c = c_end <= t
        p_heads, inv = [], []
        for h in heads:
            sm = jnp.where(mask_c, lanes(s, h), -jnp.inf)
            m = colmax(sm)
            m = jnp.where(m == -jnp.inf, 0.0, m)
            p = jnp.exp2(sm - m)
            inv.append(1.0 / jnp.maximum(colsum(p), jnp.finfo(F32).tiny))
            sc_ref[h, 0:SC_PAD, :] = jnp.zeros((SC_PAD, TQ), F32)
            sc_ref[h, SC_PAD:SC_PAD + ncp, :] = p
            p_heads.append(p.astype(BF16))
        oc_ref[...] = (_dot(vcT_ref[:, 0:ncp], jnp.concatenate(p_heads, axis=1))
                       * jnp.concatenate(inv, axis=1))
        per_sel = SEL_BLOCK // CMP_STRIDE
        imp = jnp.zeros((nbp, TQ), F32)
        for h in heads:
            tot = sc_ref[h, pl.ds(SC_PAD - 1, nbp, stride=per_sel), :]
            for r in range(per_sel):
                tot = tot + sc_ref[h, pl.ds(SC_PAD + r, nbp, stride=per_sel), :]
            imp = imp + tot * inv[h]
        blk = lax.broadcasted_iota(jnp.int32, (nbp, TQ), 0)
        blk_f = blk.astype(F32)
        valid = blk <= cur
        forced = (blk == 0) | (blk == cur) | (blk == cur - 1)
        free = valid & jnp.logical_not(forced)
        score = jnp.where(free, imp, -jnp.inf)
        for _ in range(min(SEL_TOPN, NB) - 3):
            mx = colmax(score)
            idx = jnp.min(jnp.where(score == mx, blk_f, float(NB)), axis=0, keepdims=True)
            score = jnp.where(blk_f == idx, -jnp.inf, score)
        picked = valid & (forced | (score == -jnp.inf))
        sb_ref[0:nbp, :] = jnp.where(picked, 0.0, NEG)
        if nbp < NB:
            sb_ref[nbp:NB, :] = jnp.full((NB - nbp, TQ), NEG, F32)
        window_branch()

    size_class = lax.div(i * N_SIZE_CLASSES, n_q)
    for k in range(N_SIZE_CLASSES):
        pl.when(size_class == k)(functools.partial(
            compressed_and_select, NC * (k + 1) // N_SIZE_CLASSES, NB * (k + 1) // N_SIZE_CLASSES))

    pad_rows = jnp.zeros((K_LANES - NSA_DH - 2 * BLK_PER_TILE, NSA_HPG * TQ), BF16)

    def scores(j):
        k0 = pl.multiple_of(j * TK, TK)
        sbt = sb_ref[pl.ds(pl.multiple_of(j * BLK_PER_TILE, BLK_PER_TILE), BLK_PER_TILE), :]
        rows = jnp.concatenate([sbt, jnp.zeros_like(sbt)], axis=0)
        rows = jnp.concatenate([rows] * NSA_HPG, axis=1).astype(BF16)
        w = jnp.concatenate([q_hi, rows, pad_rows], axis=0)
        return _dot(ks_ref[pl.ds(k0, TK), :], w)

    def attend(j, s, carry, causal):
        m, acc = carry
        k0 = pl.multiple_of(j * TK, TK)
        if causal:
            keep = k0 + lax.broadcasted_iota(jnp.int32, (TK, TQ), 0) <= t
            s = jnp.concatenate([jnp.where(keep, lanes(s, h), NEG) for h in heads], axis=1)
        m_new = jnp.maximum(m, colmax(s))
        alpha = jnp.exp2(m - m_new)
        p = jnp.exp2(s - m_new).astype(BF16)
        acc = acc * alpha + _dot(vsT_ref[:, pl.ds(k0, TK)], p)
        return m_new, acc

    jd = lax.div(i, TK // TQ)
    n_pairs = lax.div(jd, 2)
    sa_ref[...] = scores(0)

    def pair(pp, carry):
        sbuf_ref[...] = scores(2 * pp + 1)
        carry = attend(2 * pp, sa_ref[...], carry, False)
        sa_ref[...] = scores(2 * pp + 2)
        return attend(2 * pp + 1, sbuf_ref[...], carry, False)

    carry = (jnp.full((1, NSA_HPG * TQ), NEG, F32), jnp.zeros((V_ROWS, NSA_HPG * TQ), F32))
    quad = lambda qq, c: pair(2 * qq + 1, pair(2 * qq, c))
    octo = lambda oo, c: quad(2 * oo + 1, quad(2 * oo, c))
    n_hexs, n_octs, n_quads = lax.div(n_pairs, 8), lax.div(n_pairs, 4), lax.div(n_pairs, 2)
    carry = lax.fori_loop(0, n_hexs, lambda hh, c: octo(2 * hh + 1, octo(2 * hh, c)), carry)
    carry = lax.fori_loop(2 * n_hexs, n_octs, octo, carry)
    carry = lax.fori_loop(2 * n_octs, n_quads, quad, carry)
    carry = lax.fori_loop(2 * n_quads, n_pairs, pair, carry)
    sbuf_ref[...] = scores(2 * n_pairs + 1)

    m_ref[...], acc_ref[...] = attend(2 * n_pairs, sa_ref[...], carry, True)

    @pl.when(jd != 2 * n_pairs)
    def _():
        m_ref[...], acc_ref[...] = attend(
            2 * n_pairs + 1, sbuf_ref[...], (m_ref[...], acc_ref[...]), True)

    acc = acc_ref[...]
    osT = acc[0:NSA_DH, :] * (1.0 / acc[NSA_DH:NSA_DH + 1, :])

    gT = gT_ref[:, pl.ds(col0, TQ)]
    ocT, owT = oc_ref[...], ow_ref[...]
    outs = []
    for h in heads:
        g_c, g_s, g_w = (gT[3 * h + b:3 * h + b + 1, :] for b in range(3))
        outs.append(g_c * lanes(ocT, h)[0:NSA_DH, :] + g_s * lanes(osT, h) + g_w * lanes(owT, h))
    o_ref[pl.ds(col0, TQ), :] = jnp.concatenate(outs, axis=0).T.astype(BF16)


def _nsa_kernel(*refs):
    def query_tile(sub, carry):
        _nsa_tile(sub, *refs)
        return carry
    lax.fori_loop(0, Q_TILES_PER_STEP, query_tile, 0)


def _nsa(qT, gT, kc4, vcT, ks, vsT, kw, vwT):
    G, S = ks.shape[0], ks.shape[1]
    NC, NB = kc4.shape[1], S // SEL_BLOCK
    per_g = lambda a: pl.BlockSpec((None,) + a.shape[1:], lambda g, i: (g,) + (0,) * (a.ndim - 1))
    wide = NSA_HPG * TQ
    scratch = [(NB, TQ),
               (TK, wide), (TK, wide),
               (NSA_HPG, SC_PAD + NC, TQ),
               (V_ROWS, wide), (NSA_DH, wide),
               (1, wide), (V_ROWS, wide)]
    return pl.pallas_call(
        _nsa_kernel,
        grid=(G, S // (TQ * Q_TILES_PER_STEP)),
        in_specs=[
            pl.BlockSpec((NSA_HPG * NSA_DH, TQ * Q_TILES_PER_STEP), lambda g, i: (g, i)),
            pl.BlockSpec((None, GATE_ROWS, TQ * Q_TILES_PER_STEP), lambda g, i: (g, 0, i)),
            per_g(kc4), per_g(vcT), per_g(ks), per_g(vsT), per_g(kw), per_g(vwT),
        ],
        out_specs=pl.BlockSpec((TQ * Q_TILES_PER_STEP, NSA_HPG * NSA_DH), lambda g, i: (i, g)),
        out_shape=jax.ShapeDtypeStruct((S, NSA_HEADS * NSA_DH), BF16),
        scratch_shapes=[pltpu.VMEM(shape, F32) for shape in scratch],
        compiler_params=_params(
            ("arbitrary", "arbitrary"),
            [(a.shape[1:], a.dtype) for a in (kc4, vcT, ks, vsT, kw, vwT)]
            + [((NSA_HPG * NSA_DH, TQ * Q_TILES_PER_STEP), F32),
               ((TQ * Q_TILES_PER_STEP, NSA_HPG * NSA_DH), BF16)],
            [(shape, F32) for shape in scratch]),
        name="nsa",
    )(qT, gT, kc4, vcT, ks, vsT, kw, vwT)


def _ret_kernel(q_ref, k_ref, v_ref, o_ref, r_ref, dm_ref, qd_ref, kd_ref):
    C = RET_CHUNK
    log_g = [math.log(1.0 - 2.0 ** (-5.0 - h)) for h in range(RET_HEADS)]

    @pl.when(pl.program_id(0) == 0)
    def _():
        r_ref[...] = jnp.zeros(r_ref.shape, F32)
        diff = (lax.broadcasted_iota(jnp.int32, (C, C), 0)
                - lax.broadcasted_iota(jnp.int32, (C, C), 1)).astype(F32)
        n = lax.broadcasted_iota(jnp.int32, (C, RET_DK), 0).astype(F32)
        for h in range(RET_HEADS):
            dm_ref[h] = jnp.where(diff >= 0.0, jnp.exp(jnp.maximum(diff, 0.0) * log_g[h]), 0.0)
            qd_ref[h] = jnp.exp((n + 1.0) * log_g[h])
            kd_ref[h] = jnp.exp((C - 1.0 - n) * log_g[h])

    for sub, h in [(sub, h) for sub in range(RET_CHUNKS_PER_STEP) for h in range(RET_HEADS)]:
        rows = slice(sub * C, (sub + 1) * C)
        q = q_ref[rows, h * RET_DK:(h + 1) * RET_DK]
        k = k_ref[rows, h * RET_DK:(h + 1) * RET_DK]
        v = v_ref[rows, h * RET_DV:(h + 1) * RET_DV]
        att = lax.dot_general(q, k, (((1,), (1,)), ((), ())), preferred_element_type=F32)
        o = _dot((att * dm_ref[h]).astype(BF16), v)
        r = r_ref[h]
        qd = (q.astype(F32) * qd_ref[h]).astype(BF16)
        o = o + _dot(qd, r.astype(BF16))
        kd = (k.astype(F32) * kd_ref[h]).astype(BF16)
        r_ref[h] = math.exp(C * log_g[h]) * r + lax.dot_general(
            kd, v, (((0,), (0,)), ((), ())), preferred_element_type=F32)
        mu = jnp.mean(o, axis=-1, keepdims=True)
        d = o - mu
        var = jnp.mean(d * d, axis=-1, keepdims=True)
        o_ref[rows, h * RET_DV:(h + 1) * RET_DV] = (d * lax.rsqrt(var + GN_EPS)).astype(BF16)


def _retention(qr, kr, vr):
    S = qr.shape[0]
    C = RET_CHUNK
    step_rows = C * RET_CHUNKS_PER_STEP
    row = lambda w: pl.BlockSpec((step_rows, w), lambda n: (n, 0))
    scratch = [(RET_HEADS, RET_DK, RET_DV),
               (RET_HEADS, C, C),
               (RET_HEADS, C, RET_DK), (RET_HEADS, C, RET_DK)]
    return pl.pallas_call(
        _ret_kernel,
        grid=(S // step_rows,),
        in_specs=[row(RET_HEADS * RET_DK), row(RET_HEADS * RET_DK), row(RET_HEADS * RET_DV)],
        out_specs=row(RET_HEADS * RET_DV),
        out_shape=jax.ShapeDtypeStruct((S, RET_HEADS * RET_DV), BF16),
        scratch_shapes=[pltpu.VMEM(shape, F32) for shape in scratch],
        compiler_params=_params(
            ("arbitrary",),
            [((step_rows, w), BF16) for w in (RET_HEADS * RET_DK, RET_HEADS * RET_DK, 2 * RET_HEADS * RET_DV)],
            [(shape, F32) for shape in scratch]),
        name="retention",
    )(qr, kr, vr)


def _post_kernel(x_ref, oa_ref, zn_ref, ob_ref, zr_ref, ma_ref, mb_ref,
                 wa_ref, wb_ref, wo_ref, g_ref, out_ref):
    f32 = lambda r: r[...].astype(F32)
    ya = _dot((f32(oa_ref) * f32(zn_ref)).astype(BF16), wa_ref[...])
    yb = _dot((f32(ob_ref) * f32(zr_ref)).astype(BF16), wb_ref[...])
    merged = f32(ma_ref) * ya + f32(mb_ref) * yb
    y = _dot(merged.astype(BF16), wo_ref[...])
    ms = jnp.mean(y * y, axis=-1, keepdims=True)
    out_ref[...] = x_ref[...] + y * lax.rsqrt(ms + RMS_EPS) * g_ref[...]


def _post(x2, oa, zn, ob, zr, ma, mb, wa, wb, wo, g_post):
    S = x2.shape[0]
    tm = POST_ROW_TILE
    row = lambda a: pl.BlockSpec((tm, a.shape[1]), lambda i: (i, 0))
    whole = lambda a: pl.BlockSpec(a.shape, lambda i: (0, 0))
    rows = (x2, oa, zn, ob, zr, ma, mb)
    consts = (wa, wb, wo, g_post)
    return pl.pallas_call(
        _post_kernel,
        grid=(S // tm,),
        in_specs=[row(a) for a in rows] + [whole(a) for a in consts],
        out_specs=row(x2),
        out_shape=jax.ShapeDtypeStruct(x2.shape, x2.dtype),
        compiler_params=_params(
            ("arbitrary",),
            [((tm, a.shape[1]), a.dtype) for a in rows + (x2,)] + [(a.shape, a.dtype) for a in consts]),
        name="post",
    )(*rows, *consts)


def _rotary_frequencies():
    nsa_inv = 1.0 / (ROPE_THETA ** (jnp.arange(0, ROPE_DIM, 2, dtype=F32) / ROPE_DIM))
    ret_inv = 1.0 / (RET_ROPE_BASE ** jnp.linspace(0.0, 1.0, RET_DK // 2, dtype=F32))
    row = jnp.concatenate([ret_inv, nsa_inv, jnp.zeros((LANE - RET_DK // 2 - ROPE_DIM // 2,), F32)])
    return jnp.pad(row[None, :], ((0, 7), (0, 0)))


def kernel(x, norm_pre, w_in, b_nsa_gate, cmp_pe_k, cmp_w1_k, cmp_w2_k, cmp_pe_v, cmp_w1_v,
           cmp_w2_v, w_nsa_o, w_ret_o, w_out, norm_post):
    B, S, _ = x.shape
    assert B == 1 and S % (2 * TK) == 0 and S >= WIN_KEYS
    assert S % (SEL_BLOCK * 8 * N_SIZE_CLASSES) == 0 and (S // TQ) % N_SIZE_CLASSES == 0
    x2 = x.reshape(S, D_MODEL)

    gate_cols = lambda a: jnp.pad(
        a.reshape(a.shape[:-1] + (NSA_GROUPS, NSA_HPG * 3)),
        [(0, 0)] * (a.ndim - 1) + [(0, 0), (0, GATE_ROWS - NSA_HPG * 3)]).reshape(a.shape[:-1] + (-1,))
    g0 = sum(PROJ_SIZES[:7])
    g1 = g0 + NSA_HEADS * 3
    gate_seg = gate_cols(w_in[:, g0:g1])
    gate_seg = jnp.pad(gate_seg, ((0, 0), (0, LANE - gate_seg.shape[1])))
    w_hi = jnp.concatenate([w_in[:, :g0], gate_seg, w_in[:, g1:]], axis=1).astype(BF16)
    bias = gate_cols(b_nsa_gate)
    bias = jnp.pad(bias, (0, LANE - bias.shape[0])).reshape(1, LANE)

    (qT, kc, vc, ks, vsT, kw, vwT, gT, zn, qr, kr, vr, zr, ma, mb) = _inproj(
        x2, norm_pre.reshape(1, D_MODEL), w_hi, bias, _rotary_frequencies())

    kc4 = _compress(kc, cmp_pe_k, cmp_w1_k, cmp_w2_k, keys=True)
    vcT = _compress(vc, cmp_pe_v, cmp_w1_v, cmp_w2_v, keys=False)

    oa = _nsa(qT, gT, kc4, vcT, ks, vsT, kw, vwT)
    ob = _retention(qr, kr, vr)

    out = _post(x2, oa, zn, ob, zr, ma, mb, w_nsa_o.astype(BF16), w_ret_o.astype(BF16),
                w_out.astype(BF16), norm_post.reshape(1, D_MODEL))
    return out.reshape(B, S, D_MODEL)
```

```python
import functools
import math

import jax
import jax.numpy as jnp
from jax import lax
from jax.experimental import pallas as pl
from jax.experimental.pallas import tpu as pltpu

F32 = jnp.float32
BF16 = jnp.bfloat16

D_MODEL = 1024
NSA_HEADS = 8
NSA_GROUPS = 2
NSA_HPG = NSA_HEADS // NSA_GROUPS
NSA_DH = 64
CMP_LEN = 32
CMP_STRIDE = 16
CMP_HIDDEN = 256
SEL_BLOCK = 64
SEL_TOPN = 16
WINDOW = 512
ROPE_THETA = 500000.0
ROPE_DIM = NSA_DH // 4
NSA_SCALE = NSA_DH ** -0.5
RET_HEADS = 4
RET_DK = 128
RET_DV = 256
RET_ROPE_BASE = 10000.0
RET_SCALE = RET_DK ** -0.5
RMS_EPS = 1e-6
GN_EPS = 1e-6
PROJ_SIZES = (512, 128, 128, 128, 128, 128, 128, 24, 512, 512, 512, 1024, 1024, 1024, 1024)

LANE = 128
V7X_VMEM_LIMIT_BYTES = 56 * 1024 * 1024
VMEM_TEMP_FLOOR_BYTES = 8 * 1024 * 1024

ROW_TILE = 256
POST_ROW_TILE = 512
TQ = 128
Q_TILES_PER_STEP = 2
TK = 512
BLK_PER_TILE = TK // SEL_BLOCK
WIN_KEYS = WINDOW + TQ
RET_CHUNK = 256
RET_CHUNKS_PER_STEP = 4
V_ROWS = 80
NEG = -1e30
LOG2E = math.log2(math.e)
K_LANES = LANE
GATE_ROWS = 16
SC_PAD = 32
N_SIZE_CLASSES = 8

_PAD_SIZES = tuple(128 if s == 24 else s for s in PROJ_SIZES)
_OFF = [0]
for _s in _PAD_SIZES:
    _OFF.append(_OFF[-1] + _s)


def _dot(a, b):
    return jnp.dot(a, b, preferred_element_type=F32)


def _split(a):
    hi = a.astype(BF16)
    lo = (a - hi.astype(F32)).astype(BF16)
    return hi, lo


def _dot3(a_hi, a_lo, b_hi, b_lo):
    return _dot(a_hi, b_hi) + _dot(a_lo, b_hi) + _dot(a_hi, b_lo)


def _nbytes(shape, dtype):
    return math.prod(d for d in shape if d is not None) * jnp.dtype(dtype).itemsize


def _params(sem, blocks, scratch=()):
    est = 2 * sum(_nbytes(*b) for b in blocks) + sum(_nbytes(*b) for b in scratch)
    limit = min(V7X_VMEM_LIMIT_BYTES, est + est // 4 + VMEM_TEMP_FLOOR_BYTES)
    return pltpu.CompilerParams(dimension_semantics=sem, vmem_limit_bytes=limit)


def _inproj_kernel(x_ref, g_ref, whi_ref, bias_ref, freq_ref,
                   qT_ref, kc_ref, vc_ref, ks_ref, vsT_ref, kw_ref, vwT_ref, gT_ref,
                   zn_ref, qr_ref, kr_ref, vr_ref, zr_ref, ma_ref, mb_ref):
    tm = x_ref.shape[0]
    x = x_ref[...]
    ms = jnp.mean(x * x, axis=-1, keepdims=True)
    h = x * lax.rsqrt(ms + RMS_EPS) * g_ref[...]
    h_hi = h.astype(BF16)

    def mm(lo, hi):
        return _dot(h_hi, whi_ref[:, lo:hi])

    lane = lax.broadcasted_iota(jnp.int32, (tm, LANE), 1)
    row = pl.program_id(0) * tm + lax.broadcasted_iota(jnp.int32, (tm, LANE), 0)
    first_group = lane < NSA_DH
    pos = row.astype(F32)

    half = ROPE_DIM // 2
    dim = jnp.bitwise_and(lane, NSA_DH - 1)
    ang = pos * freq_ref[0:1, :]
    c_all, s_all = jnp.cos(ang), jnp.sin(ang)
    c_sw, s_sw = pltpu.roll(c_all, LANE // 2, 1), pltpu.roll(s_all, LANE // 2, 1)
    cr = jnp.where(first_group, c_all, c_sw)
    sr = jnp.where(first_group, -s_all, s_sw)
    lo_half, hi_half = dim < half, (dim >= half) & (dim < ROPE_DIM)
    c_lo, s_lo = jnp.where(first_group, c_sw, c_all), jnp.where(first_group, s_sw, s_all)
    c_hi, s_hi = pltpu.roll(c_lo, half, 1), pltpu.roll(s_lo, half, 1)
    cn = jnp.where(lo_half, c_lo, jnp.where(hi_half, c_hi, 1.0))
    san = jnp.where(hi_half, s_hi, 0.0)
    sbn = jnp.where(lo_half, -s_lo, 0.0)

    def rot_nsa(p):
        return p * cn + pltpu.roll(p, half, 1) * san + pltpu.roll(p, LANE - half, 1) * sbn

    def rot_ret(p):
        return p * cr + pltpu.roll(p, LANE // 2, 1) * sr

    def key_rows(p, fill):
        swapped = pltpu.roll(p, NSA_DH, 1)
        return [jnp.where(first_group, pg, fill).astype(BF16) for pg in (p, swapped)]

    ones_row = jnp.where(lax.broadcasted_iota(jnp.int32, (V_ROWS - NSA_DH, tm), 0) == 0, 1.0, 0.0)

    def value_rows(p):
        pT = p.T
        return [jnp.concatenate([pT[g * NSA_DH:(g + 1) * NSA_DH, :], ones_row], axis=0).astype(BF16)
                for g in range(NSA_GROUPS)]

    p = mm(0, _OFF[2])
    for c in range(4):
        qT_ref[c * LANE:(c + 1) * LANE, :] = (rot_nsa(p[:, c * LANE:(c + 1) * LANE]) * NSA_SCALE).T
    kc_ref[...] = rot_nsa(p[:, 4 * LANE:5 * LANE])

    p = mm(_OFF[2], _OFF[8])
    vc_ref[...] = p[:, 0 * LANE:1 * LANE]
    blk_in_tile = jnp.right_shift(jnp.bitwise_and(row, TK - 1), SEL_BLOCK.bit_length() - 1)
    onehot = jnp.where(lane - NSA_DH == blk_in_tile, 1.0, 0.0)
    ks = key_rows(rot_nsa(p[:, 1 * LANE:2 * LANE]), onehot)
    vs = value_rows(p[:, 2 * LANE:3 * LANE])
    kw = key_rows(rot_nsa(p[:, 3 * LANE:4 * LANE]), 0.0)
    vw = value_rows(p[:, 4 * LANE:5 * LANE])
    gT = jax.nn.sigmoid(p[:, 5 * LANE:6 * LANE] + bias_ref[...]).T
    for g in range(NSA_GROUPS):
        ks_ref[g], vsT_ref[g], kw_ref[g], vwT_ref[g] = ks[g], vs[g], kw[g], vw[g]
        gT_ref[g] = gT[g * GATE_ROWS:(g + 1) * GATE_ROWS, :]

    z = mm(_OFF[8], _OFF[9])
    zn_ref[...] = (z * jax.nn.sigmoid(z)).astype(BF16)

    p = mm(_OFF[9], _OFF[11])
    for c in range(4):
        qr_ref[:, c * LANE:(c + 1) * LANE] = rot_ret(p[:, c * LANE:(c + 1) * LANE]).astype(BF16)
    for c in range(4):
        kr_ref[:, c * LANE:(c + 1) * LANE] = (
            rot_ret(p[:, (4 + c) * LANE:(5 + c) * LANE]) * RET_SCALE).astype(BF16)

    vr_ref[...] = mm(_OFF[11], _OFF[12]).astype(BF16)
    z = mm(_OFF[12], _OFF[13])
    zr_ref[...] = (z * jax.nn.sigmoid(z)).astype(BF16)
    ma_ref[...] = jax.nn.sigmoid(mm(_OFF[13], _OFF[14])).astype(BF16)
    mb_ref[...] = jax.nn.sigmoid(mm(_OFF[14], _OFF[15])).astype(BF16)


def _inproj(x2, norm_pre, w_hi, bias, freqs):
    S = x2.shape[0]
    tm = ROW_TILE
    G = NSA_GROUPS
    row = lambda w: pl.BlockSpec((tm, w), lambda i: (i, 0))
    whole = lambda a: pl.BlockSpec(a.shape, lambda i: (0,) * a.ndim)
    col = lambda r: pl.BlockSpec((r, tm), lambda i: (0, i))
    g_row = lambda w: pl.BlockSpec((G, tm, w), lambda i: (0, i, 0))
    g_col = lambda r: pl.BlockSpec((G, r, tm), lambda i: (0, 0, i))
    sds = jax.ShapeDtypeStruct
    outs = [
        (col(NSA_HEADS * NSA_DH), sds((NSA_HEADS * NSA_DH, S), F32)),
        (row(LANE), sds((S, LANE), F32)),
        (row(LANE), sds((S, LANE), F32)),
        (g_row(K_LANES), sds((G, S, K_LANES), BF16)),
        (g_col(V_ROWS), sds((G, V_ROWS, S), BF16)),
        (g_row(K_LANES), sds((G, S, K_LANES), BF16)),
        (g_col(V_ROWS), sds((G, V_ROWS, S), BF16)),
        (g_col(GATE_ROWS), sds((G, GATE_ROWS, S), F32)),
        (row(512), sds((S, 512), BF16)),
        (row(512), sds((S, 512), BF16)), (row(512), sds((S, 512), BF16)),
        (row(1024), sds((S, 1024), BF16)), (row(1024), sds((S, 1024), BF16)),
        (row(1024), sds((S, 1024), BF16)), (row(1024), sds((S, 1024), BF16)),
    ]
    return pl.pallas_call(
        _inproj_kernel,
        grid=(S // tm,),
        in_specs=[row(D_MODEL), whole(norm_pre), whole(w_hi), whole(bias), whole(freqs)],
        out_specs=[o[0] for o in outs],
        out_shape=[o[1] for o in outs],
        compiler_params=_params(
            ("arbitrary",),
            [((tm, D_MODEL), F32), (w_hi.shape, BF16)] + [(o[0].block_shape, o[1].dtype) for o in outs]),
        name="inproj",
    )(x2, norm_pre, w_hi, bias, freqs)


def _compress_kernel(x_ref, pe_ref, w1_ref, w2_ref, o_ref, *, keys):
    NC = x_ref.shape[0] // CMP_STRIDE
    g = pl.program_id(0)
    first_half = lax.broadcasted_iota(jnp.int32, (NC, LANE), 1) < NSA_DH
    tokens = []
    for u in range(CMP_STRIDE // 2):
        a = x_ref[pl.ds(2 * u, NC, stride=CMP_STRIDE), :]
        b = x_ref[pl.ds(2 * u + 1, NC, stride=CMP_STRIDE), :]
        ar, br = pltpu.roll(a, NSA_DH, 1), pltpu.roll(b, NSA_DH, 1)
        tokens.append(jnp.where(first_half, jnp.where(g == 0, a, ar), jnp.where(g == 0, br, b)))
    xg = jnp.concatenate(tokens, axis=1)
    prod = lambda a, b: _dot3(*_split(a), *_split(b))
    half = xg.shape[1]
    first = prod(xg, w1_ref[0:half, :])
    second = prod(xg, w1_ref[half:2 * half, :])
    nxt = pltpu.roll(second, NC - 1, 0)
    pe_term = prod(pe_ref[...], w1_ref[...])[0:1, :]
    hid = first + nxt + pe_term
    act = hid * (0.5 * (1.0 + jnp.tanh(math.sqrt(2.0 / math.pi) * (hid + 0.044715 * (hid * hid * hid)))))
    out = prod(act, w2_ref[...])
    if keys:
        hi, lo = _split(out)
        o_ref[...] = jnp.concatenate([hi, hi, lo, lo], axis=1)
    else:
        ones_row = jnp.where(lax.broadcasted_iota(jnp.int32, (V_ROWS - NSA_DH, NC), 0) == 0, 1.0, 0.0)
        o_ref[...] = jnp.concatenate([out.T, ones_row], axis=0).astype(BF16)


def _compress(raw, pe, w1, w2, keys):
    S = raw.shape[0]
    NC = S // CMP_STRIDE
    G = NSA_GROUPS
    half = CMP_STRIDE * NSA_DH
    pe8 = jnp.pad(pe.reshape(1, 2 * half), ((0, 7), (0, 0)))
    if keys:
        out_spec = pl.BlockSpec((None, NC, 4 * NSA_DH), lambda g: (g, 0, 0))
        out_shape = jax.ShapeDtypeStruct((G, NC, 4 * NSA_DH), BF16)
    else:
        out_spec = pl.BlockSpec((None, V_ROWS, NC), lambda g: (g, 0, 0))
        out_shape = jax.ShapeDtypeStruct((G, V_ROWS, NC), BF16)
    whole = lambda a: pl.BlockSpec(a.shape, lambda g: (0,) * a.ndim)
    return pl.pallas_call(
        functools.partial(_compress_kernel, keys=keys),
        grid=(G,),
        in_specs=[whole(raw), whole(pe8), whole(w1), whole(w2)],
        out_specs=out_spec,
        out_shape=out_shape,
        compiler_params=_params(
            ("arbitrary",),
            [(raw.shape, F32), (w1.shape, F32), (w2.shape, F32), (out_shape.shape[1:], BF16)],
            [((NC, 2 * CMP_HIDDEN), F32), ((NC, half), F32)]),
        name="compress_k" if keys else "compress_v",
    )(raw, pe8, w1, w2)


def _nsa_tile(sub, qT_ref, gT_ref, kc4_ref, vcT_ref, ks_ref, vsT_ref, kw_ref, vwT_ref, o_ref,
              sb_ref, sa_ref, sbuf_ref, sc_ref, oc_ref, ow_ref, m_ref, acc_ref):
    i = pl.program_id(1) * Q_TILES_PER_STEP + sub
    col0 = pl.multiple_of(sub * TQ, TQ)
    NC = kc4_ref.shape[0]
    NB = sb_ref.shape[0]
    n_q = pl.num_programs(1) * Q_TILES_PER_STEP
    heads = range(NSA_HPG)
    lanes = lambda a, h: a[:, h * TQ:(h + 1) * TQ]
    colmax = lambda a: jnp.max(a, axis=0, keepdims=True)
    colsum = lambda a: jnp.sum(a, axis=0, keepdims=True)

    qT = qT_ref[:, pl.ds(col0, TQ)] * LOG2E
    Q = jnp.concatenate([qT[h * NSA_DH:(h + 1) * NSA_DH, :] for h in heads], axis=1)
    q_hi, q_lo = _split(Q)
    q4 = jnp.concatenate([q_hi, q_lo, q_hi, q_lo], axis=0)
    t = i * TQ + lax.broadcasted_iota(jnp.int32, (1, TQ), 1)
    cur = jnp.right_shift(t, SEL_BLOCK.bit_length() - 1)

    def window_branch():
        w0 = pl.multiple_of(jnp.maximum(i * TQ - WINDOW, 0), TQ)
        q_pad = jnp.concatenate([q_hi, jnp.zeros((K_LANES - NSA_DH, NSA_HPG * TQ), BF16)], axis=0)
        s = _dot(kw_ref[pl.ds(w0, WIN_KEYS), :], q_pad)
        key = w0 + lax.broadcasted_iota(jnp.int32, (WIN_KEYS, TQ), 0)
        mask_w = (key <= t) & (key > t - WINDOW)
        p_heads = []
        for h in heads:
            sm = jnp.where(mask_w, lanes(s, h), NEG)
            p_heads.append(jnp.exp2(sm - colmax(sm)).astype(BF16))
        accw = _dot(vwT_ref[:, pl.ds(w0, WIN_KEYS)], jnp.concatenate(p_heads, axis=1))
        owT = accw[0:NSA_DH, :] * (1.0 / accw[NSA_DH:NSA_DH + 1, :])
        ow_ref[...] = owT

    def compressed_and_select(ncp, nbp):
        s = _dot(kc4_ref[0:ncp, :], q4)
        c_end = lax.broadcasted_iota(jnp.int32, (ncp, TQ), 0) * CMP_STRIDE + (CMP_LEN - 1)
        mask_c = c_end <= t
        p_heads, inv = [], []
        for h in heads:
            sm = jnp.where(mask_c, lanes(s, h), -jnp.inf)
            m = colmax(sm)
            m = jnp.where(m == -jnp.inf, 0.0, m)
            p = jnp.exp2(sm - m)
            inv.append(1.0 / jnp.maximum(colsum(p), jnp.finfo(F32).tiny))
            sc_ref[h, 0:SC_PAD, :] = jnp.zeros((SC_PAD, TQ), F32)
            sc_ref[h, SC_PAD:SC_PAD + ncp, :] = p
            p_heads.append(p.astype(BF16))
        oc_ref[...] = (_dot(vcT_ref[:, 0:ncp], jnp.concatenate(p_heads, axis=1))
                       * jnp.concatenate(inv, axis=1))
        per_sel = SEL_BLOCK // CMP_STRIDE
        imp = jnp.zeros((nbp, TQ), F32)
        for h in heads:
            tot = sc_ref[h, pl.ds(SC_PAD - 1, nbp, stride=per_sel), :]
            for r in range(per_sel):
                tot = tot + sc_ref[h, pl.ds(SC_PAD + r, nbp, stride=per_sel), :]
            imp = imp + tot * inv[h]
        blk = lax.broadcasted_iota(jnp.int32, (nbp, TQ), 0)
        blk_f = blk.astype(F32)
        valid = blk <= cur
        forced = (blk == 0) | (blk == cur) | (blk == cur - 1)
        free = valid & jnp.logical_not(forced)
        score = jnp.where(free, imp, -jnp.inf)
        for _ in range(min(SEL_TOPN, NB) - 3):
            mx = colmax(score)
            idx = jnp.min(jnp.where(score == mx, blk_f, float(NB)), axis=0, keepdims=True)
            score = jnp.where(blk_f == idx, -jnp.inf, score)
        picked = valid & (forced | (score == -jnp.inf))
        sb_ref[0:nbp, :] = jnp.where(picked, 0.0, NEG)
        if nbp < NB:
            sb_ref[nbp:NB, :] = jnp.full((NB - nbp, TQ), NEG, F32)
        window_branch()

    size_class = lax.div(i * N_SIZE_CLASSES, n_q)
    for k in range(N_SIZE_CLASSES):
        pl.when(size_class == k)(functools.partial(
            compressed_and_select, NC * (k + 1) // N_SIZE_CLASSES, NB * (k + 1) // N_SIZE_CLASSES))

    pad_rows = jnp.zeros((K_LANES - NSA_DH - 2 * BLK_PER_TILE, NSA_HPG * TQ), BF16)

    def scores(j):
        k0 = pl.multiple_of(j * TK, TK)
        sbt = sb_ref[pl.ds(pl.multiple_of(j * BLK_PER_TILE, BLK_PER_TILE), BLK_PER_TILE), :]
        rows = jnp.concatenate([sbt, jnp.zeros_like(sbt)], axis=0)
        rows = jnp.concatenate([rows] * NSA_HPG, axis=1).astype(BF16)
        w = jnp.concatenate([q_hi, rows, pad_rows], axis=0)
        return _dot(ks_ref[pl.ds(k0, TK), :], w)

    def attend(j, s, carry, causal):
        m, acc = carry
        k0 = pl.multiple_of(j * TK, TK)
        if causal:
            keep = k0 + lax.broadcasted_iota(jnp.int32, (TK, TQ), 0) <= t
            s = jnp.concatenate([jnp.where(keep, lanes(s, h), NEG) for h in heads], axis=1)
        m_new = jnp.maximum(m, colmax(s))
        alpha = jnp.exp2(m - m_new)
        p = jnp.exp2(s - m_new).astype(BF16)
        acc = acc * alpha + _dot(vsT_ref[:, pl.ds(k0, TK)], p)
        return m_new, acc

    jd = lax.div(i, TK // TQ)
    n_pairs = lax.div(jd, 2)
    sa_ref[...] = scores(0)

    def pair(pp, carry):
        sbuf_ref[...] = scores(2 * pp + 1)
        carry = attend(2 * pp, sa_ref[...], carry, False)
        sa_ref[...] = scores(2 * pp + 2)
        return attend(2 * pp + 1, sbuf_ref[...], carry, False)

    carry = (jnp.full((1, NSA_HPG * TQ), NEG, F32), jnp.zeros((V_ROWS, NSA_HPG * TQ), F32))
    quad = lambda qq, c: pair(2 * qq + 1, pair(2 * qq, c))
    octo = lambda oo, c: quad(2 * oo + 1, quad(2 * oo, c))
    n_hexs, n_octs, n_quads = lax.div(n_pairs, 8), lax.div(n_pairs, 4), lax.div(n_pairs, 2)
    carry = lax.fori_loop(0, n_hexs, lambda hh, c: octo(2 * hh + 1, octo(2 * hh, c)), carry)
    carry = lax.fori_loop(2 * n_hexs, n_octs, octo, carry)
    carry = lax.fori_loop(2 * n_octs, n_quads, quad, carry)
    carry = lax.fori_loop(2 * n_quads, n_pairs, pair, carry)
    sbuf_ref[...] = scores(2 * n_pairs + 1)

    m_ref[...], acc_ref[...] = attend(2 * n_pairs, sa_ref[...], carry, True)

    @pl.when(jd != 2 * n_pairs)
    def _():
        m_ref[...], acc_ref[...] = attend(
            2 * n_pairs + 1, sbuf_ref[...], (m_ref[...], acc_ref[...]), True)

    acc = acc_ref[...]
    osT = acc[0:NSA_DH, :] * (1.0 / acc[NSA_DH:NSA_DH + 1, :])

    gT = gT_ref[:, pl.ds(col0, TQ)]
    ocT, owT = oc_ref[...], ow_ref[...]
    outs = []
    for h in heads:
        g_c, g_s, g_w = (gT[3 * h + b:3 * h + b + 1, :] for b in range(3))
        outs.append(g_c * lanes(ocT, h)[0:NSA_DH, :] + g_s * lanes(osT, h) + g_w * lanes(owT, h))
    o_ref[pl.ds(col0, TQ), :] = jnp.concatenate(outs, axis=0).T.astype(BF16)


def _nsa_kernel(*refs):
    def query_tile(sub, carry):
        _nsa_tile(sub, *refs)
        return carry
    lax.fori_loop(0, Q_TILES_PER_STEP, query_tile, 0)


def _nsa(qT, gT, kc4, vcT, ks, vsT, kw, vwT):
    G, S = ks.shape[0], ks.shape[1]
    NC, NB = kc4.shape[1], S // SEL_BLOCK
    per_g = lambda a: pl.BlockSpec((None,) + a.shape[1:], lambda g, i: (g,) + (0,) * (a.ndim - 1))
    wide = NSA_HPG * TQ
    scratch = [(NB, TQ),
               (TK, wide), (TK, wide),
               (NSA_HPG, SC_PAD + NC, TQ),
               (V_ROWS, wide), (NSA_DH, wide),
               (1, wide), (V_ROWS, wide)]
    return pl.pallas_call(
        _nsa_kernel,
        grid=(G, S // (TQ * Q_TILES_PER_STEP)),
        in_specs=[
            pl.BlockSpec((NSA_HPG * NSA_DH, TQ * Q_TILES_PER_STEP), lambda g, i: (g, i)),
            pl.BlockSpec((None, GATE_ROWS, TQ * Q_TILES_PER_STEP), lambda g, i: (g, 0, i)),
            per_g(kc4), per_g(vcT), per_g(ks), per_g(vsT), per_g(kw), per_g(vwT),
        ],
        out_specs=pl.BlockSpec((TQ * Q_TILES_PER_STEP, NSA_HPG * NSA_DH), lambda g, i: (i, g)),
        out_shape=jax.ShapeDtypeStruct((S, NSA_HEADS * NSA_DH), BF16),
        scratch_shapes=[pltpu.VMEM(shape, F32) for shape in scratch],
        compiler_params=_params(
            ("arbitrary", "arbitrary"),
            [(a.shape[1:], a.dtype) for a in (kc4, vcT, ks, vsT, kw, vwT)]
            + [((NSA_HPG * NSA_DH, TQ * Q_TILES_PER_STEP), F32),
               ((TQ * Q_TILES_PER_STEP, NSA_HPG * NSA_DH), BF16)],
            [(shape, F32) for shape in scratch]),
        name="nsa",
    )(qT, gT, kc4, vcT, ks, vsT, kw, vwT)


def _ret_kernel(q_ref, k_ref, v_ref, o_ref, r_ref, dm_ref, qd_ref, kd_ref):
    C = RET_CHUNK
    log_g = [math.log(1.0 - 2.0 ** (-5.0 - h)) for h in range(RET_HEADS)]

    @pl.when(pl.program_id(0) == 0)
    def _():
        r_ref[...] = jnp.zeros(r_ref.shape, F32)
        diff = (lax.broadcasted_iota(jnp.int32, (C, C), 0)
                - lax.broadcasted_iota(jnp.int32, (C, C), 1)).astype(F32)
        n = lax.broadcasted_iota(jnp.int32, (C, RET_DK), 0).astype(F32)
        for h in range(RET_HEADS):
            dm_ref[h] = jnp.where(diff >= 0.0, jnp.exp(jnp.maximum(diff, 0.0) * log_g[h]), 0.0)
            qd_ref[h] = jnp.exp((n + 1.0) * log_g[h])
            kd_ref[h] = jnp.exp((C - 1.0 - n) * log_g[h])

    for sub, h in [(sub, h) for sub in range(RET_CHUNKS_PER_STEP) for h in range(RET_HEADS)]:
        rows = slice(sub * C, (sub + 1) * C)
        q = q_ref[rows, h * RET_DK:(h + 1) * RET_DK]
        k = k_ref[rows, h * RET_DK:(h + 1) * RET_DK]
        v = v_ref[rows, h * RET_DV:(h + 1) * RET_DV]
        att = lax.dot_general(q, k, (((1,), (1,)), ((), ())), preferred_element_type=F32)
        o = _dot((att * dm_ref[h]).astype(BF16), v)
        r = r_ref[h]
        qd = (q.astype(F32) * qd_ref[h]).astype(BF16)
        o = o + _dot(qd, r.astype(BF16))
        kd = (k.astype(F32) * kd_ref[h]).astype(BF16)
        r_ref[h] = math.exp(C * log_g[h]) * r + lax.dot_general(
            kd, v, (((0,), (0,)), ((), ())), preferred_element_type=F32)
        mu = jnp.mean(o, axis=-1, keepdims=True)
        d = o - mu
        var = jnp.mean(d * d, axis=-1, keepdims=True)
        o_ref[rows, h * RET_DV:(h + 1) * RET_DV] = (d * lax.rsqrt(var + GN_EPS)).astype(BF16)


def _retention(qr, kr, vr):
    S = qr.shape[0]
    C = RET_CHUNK
    step_rows = C * RET_CHUNKS_PER_STEP
    row = lambda w: pl.BlockSpec((step_rows, w), lambda n: (n, 0))
    scratch = [(RET_HEADS, RET_DK, RET_DV),
               (RET_HEADS, C, C),
               (RET_HEADS, C, RET_DK), (RET_HEADS, C, RET_DK)]
    return pl.pallas_call(
        _ret_kernel,
        grid=(S // step_rows,),
        in_specs=[row(RET_HEADS * RET_DK), row(RET_HEADS * RET_DK), row(RET_HEADS * RET_DV)],
        out_specs=row(RET_HEADS * RET_DV),
        out_shape=jax.ShapeDtypeStruct((S, RET_HEADS * RET_DV), BF16),
        scratch_shapes=[pltpu.VMEM(shape, F32) for shape in scratch],
        compiler_params=_params(
            ("arbitrary",),
            [((step_rows, w), BF16) for w in (RET_HEADS * RET_DK, RET_HEADS * RET_DK, 2 * RET_HEADS * RET_DV)],
            [(shape, F32) for shape in scratch]),
        name="retention",
    )(qr, kr, vr)


def _post_kernel(x_ref, oa_ref, zn_ref, ob_ref, zr_ref, ma_ref, mb_ref,
                 wa_ref, wb_ref, wo_ref, g_ref, out_ref):
    f32 = lambda r: r[...].astype(F32)
    ya = _dot((f32(oa_ref) * f32(zn_ref)).astype(BF16), wa_ref[...])
    yb = _dot((f32(ob_ref) * f32(zr_ref)).astype(BF16), wb_ref[...])
    merged = f32(ma_ref) * ya + f32(mb_ref) * yb
    y = _dot(merged.astype(BF16), wo_ref[...])
    ms = jnp.mean(y * y, axis=-1, keepdims=True)
    out_ref[...] = x_ref[...] + y * lax.rsqrt(ms + RMS_EPS) * g_ref[...]


def _post(x2, oa, zn, ob, zr, ma, mb, wa, wb, wo, g_post):
    S = x2.shape[0]
    tm = POST_ROW_TILE
    row = lambda a: pl.BlockSpec((tm, a.shape[1]), lambda i: (i, 0))
    whole = lambda a: pl.BlockSpec(a.shape, lambda i: (0, 0))
    rows = (x2, oa, zn, ob, zr, ma, mb)
    consts = (wa, wb, wo, g_post)
    return pl.pallas_call(
        _post_kernel,
        grid=(S // tm,),
        in_specs=[row(a) for a in rows] + [whole(a) for a in consts],
        out_specs=row(x2),
        out_shape=jax.ShapeDtypeStruct(x2.shape, x2.dtype),
        compiler_params=_params(
            ("arbitrary",),
            [((tm, a.shape[1]), a.dtype) for a in rows + (x2,)] + [(a.shape, a.dtype) for a in consts]),
        name="post",
    )(*rows, *consts)


def _rotary_frequencies():
    nsa_inv = 1.0 / (ROPE_THETA ** (jnp.arange(0, ROPE_DIM, 2, dtype=F32) / ROPE_DIM))
    ret_inv = 1.0 / (RET_ROPE_BASE ** jnp.linspace(0.0, 1.0, RET_DK // 2, dtype=F32))
    row = jnp.concatenate([ret_inv, nsa_inv, jnp.zeros((LANE - RET_DK // 2 - ROPE_DIM // 2,), F32)])
    return jnp.pad(row[None, :], ((0, 7), (0, 0)))


def kernel(x, norm_pre, w_in, b_nsa_gate, cmp_pe_k, cmp_w1_k, cmp_w2_k, cmp_pe_v, cmp_w1_v,
           cmp_w2_v, w_nsa_o, w_ret_o, w_out, norm_post):
    B, S, _ = x.shape
    assert B == 1 and S % (2 * TK) == 0 and S >= WIN_KEYS
    assert S % (SEL_BLOCK * 8 * N_SIZE_CLASSES) == 0 and (S // TQ) % N_SIZE_CLASSES == 0
    x2 = x.reshape(S, D_MODEL)

    gate_cols = lambda a: jnp.pad(
        a.reshape(a.shape[:-1] + (NSA_GROUPS, NSA_HPG * 3)),
        [(0, 0)] * (a.ndim - 1) + [(0, 0), (0, GATE_ROWS - NSA_HPG * 3)]).reshape(a.shape[:-1] + (-1,))
    g0 = sum(PROJ_SIZES[:7])
    g1 = g0 + NSA_HEADS * 3
    gate_seg = gate_cols(w_in[:, g0:g1])
    gate_seg = jnp.pad(gate_seg, ((0, 0), (0, LANE - gate_seg.shape[1])))
    w_hi = jnp.concatenate([w_in[:, :g0], gate_seg, w_in[:, g1:]], axis=1).astype(BF16)
    bias = gate_cols(b_nsa_gate)
    bias = jnp.pad(bias, (0, LANE - bias.shape[0])).reshape(1, LANE)

    (qT, kc, vc, ks, vsT, kw, vwT, gT, zn, qr, kr, vr, zr, ma, mb) = _inproj(
        x2, norm_pre.reshape(1, D_MODEL), w_hi, bias, _rotary_frequencies())

    kc4 = _compress(kc, cmp_pe_k, cmp_w1_k, cmp_w2_k, keys=True)
    vcT = _compress(vc, cmp_pe_v, cmp_w1_v, cmp_w2_v, keys=False)

    oa = _nsa(qT, gT, kc4, vcT, ks, vsT, kw, vwT)
    ob = _retention(qr, kr, vr)

    out = _post(x2, oa, zn, ob, zr, ma, mb, w_nsa_o.astype(BF16), w_ret_o.astype(BF16),
                w_out.astype(BF16), norm_post.reshape(1, D_MODEL))
    return out.reshape(B, S, D_MODEL)
```
